```python
import jax, jax.numpy as jnp
from jax import lax
import numpy as np

D_MODEL = 1024
BATCH = 4
SEQ = 4096
DEPTH = 1

ATT_HEADS = 8
ATT_HD = 64
ATT_W = ATT_HEADS * ATT_HD
IDX_HEADS = 4
IDX_HD = 64
TOPK_MAX = 256
QBLOCK = 128
GLA_HEADS = 4
GLA_DK = 64
GLA_DV = 128
GLA_W = GLA_HEADS * GLA_DV
GLA_GATE_RANK = 16
GLA_TAU = 16.0
GLA_CHUNK = 64
ROPE_THETA = 500000.0
ROT_DIM = ATT_HD // 4
NORM_EPS = 1e-6
NEG = -1e30

PROJ_SIZES = (
    ATT_W, ATT_W, ATT_W,
    IDX_HEADS * IDX_HD,
    IDX_HD,
    IDX_HEADS,
    ATT_W,
    GLA_HEADS * GLA_DK,
    GLA_HEADS * GLA_DK,
    GLA_W,
    GLA_GATE_RANK,
    GLA_W,
)
PROJ_TOTAL = sum(PROJ_SIZES)

kernel_name = "hymba_dsa_gla_sandwich_block"


def _rmsnorm(x, w):
    xf = x.astype(jnp.float32)
    y = xf * lax.rsqrt(jnp.mean(xf * xf, axis=-1, keepdims=True) + NORM_EPS)
    return (y * w.astype(jnp.float32)).astype(x.dtype)


def _rope_tables(positions):
    inv = ROPE_THETA ** (-jnp.arange(0, ROT_DIM, 2, dtype=jnp.float32) / ROT_DIM)
    ang = positions.astype(jnp.float32)[..., None] * inv
    return jnp.cos(ang)[:, :, None, :], jnp.sin(ang)[:, :, None, :]


def _partial_rope(x, cos, sin):
    half = ROT_DIM // 2
    xf = x.astype(jnp.float32)
    x1 = xf[..., :half]
    x2 = xf[..., half:ROT_DIM]
    out = jnp.concatenate([x1 * cos - x2 * sin, x2 * cos + x1 * sin, xf[..., ROT_DIM:]], axis=-1)
    return out.astype(x.dtype)


def _dsa_attention(q, k, v, q_idx, k_idx, w_idx):
    B, S, H, dh = q.shape
    nb = S // QBLOCK
    ksel = min(TOPK_MAX, S // 4)
    key_pos = jnp.arange(S)
    k_idx_f = k_idx.astype(jnp.float32)
    gather = jax.vmap(lambda tab, ix: tab[ix])

    def to_blocks(a):
        return jnp.moveaxis(a.reshape((B, nb, QBLOCK) + a.shape[2:]), 1, 0)

    def block(args):
        qb, qib, wb, bi = args
        qpos = bi * QBLOCK + jnp.arange(QBLOCK)
        causal = key_pos[None, :] <= qpos[:, None]
        rel = jax.nn.relu(jnp.einsum('bqhd,bsd->bqhs', qib.astype(jnp.float32), k_idx_f))
        score = jnp.einsum('bqh,bqhs->bqs', wb.astype(jnp.float32), rel)
        score = jnp.where(causal[None], score, NEG)
        _, idx = lax.top_k(score, ksel)
        valid = idx <= qpos[None, :, None]
        kg = gather(k, idx)
        vg = gather(v, idx)
        logits = jnp.einsum('bqhd,bqkhd->bqhk', qb, kg).astype(jnp.float32) * (dh ** -0.5)
        logits = jnp.where(valid[:, :, None, :], logits, NEG)
        p = jax.nn.softmax(logits, axis=-1).astype(v.dtype)
        return jnp.einsum('bqhk,bqkhd->bqhd', p, vg)

    out = lax.map(block, (to_blocks(q), to_blocks(q_idx), to_blocks(w_idx), jnp.arange(nb)))
    return jnp.moveaxis(out, 0, 1).reshape(B, S, H, dh)


def _gla(q, k, v, g):
    B, S, H, dk = q.shape
    C = GLA_CHUNK
    nc = S // C

    def chunks(a):
        return jnp.transpose(a.reshape(B, nc, C, H, a.shape[-1]), (1, 0, 3, 2, 4))

    qc = chunks(q.astype(jnp.float32) * (dk ** -0.5))
    kc = chunks(k.astype(jnp.float32))
    vc = chunks(v.astype(jnp.float32))
    gc = chunks(g)
    tri = jnp.tril(jnp.ones((C, C), dtype=bool))

    def step(state, inp):
        qi, ki, vi, gi = inp
        b = jnp.cumsum(gi, axis=2)
        o_inter = jnp.einsum('bhtd,bhde->bhte', qi * jnp.exp(b), state)
        diff = b[:, :, :, None, :] - b[:, :, None, :, :]
        decay = jnp.exp(jnp.where(tri[None, None, :, :, None], diff, NEG))
        a = jnp.einsum('bhtd,bhsd,bhtsd->bhts', qi, ki, decay)
        o = o_inter + jnp.einsum('bhts,bhse->bhte', a, vi)
        b_last = b[:, :, -1:, :]
        new_state = (jnp.exp(b_last[:, :, 0, :])[..., None] * state
                     + jnp.einsum('bhsd,bhse->bhde', ki * jnp.exp(b_last - b), vi))
        return new_state, o

    state0 = jnp.zeros((B, H, dk, v.shape[-1]), jnp.float32)
    _, o = lax.scan(step, state0, (qc, kc, vc, gc))
    return jnp.transpose(o, (1, 0, 3, 2, 4)).reshape(B, S, H, v.shape[-1])


def setup_inputs(seed: int = 0) -> dict:
    key = jax.random.key(seed)
    ks = jax.random.split(key, 9)
    x = jax.random.normal(ks[0], (BATCH, SEQ, D_MODEL), jnp.float32)
    positions = jnp.broadcast_to(jnp.arange(SEQ, dtype=jnp.int32)[None, :], (BATCH, SEQ))
    w_in = jax.random.normal(ks[1], (D_MODEL, PROJ_TOTAL), jnp.float32) * D_MODEL ** -0.5
    w_gla_gate_up = jax.random.normal(ks[2], (GLA_GATE_RANK, GLA_HEADS * GLA_DK), jnp.float32) * GLA_GATE_RANK ** -0.5
    b_gla_gate = 0.01 * jax.random.normal(ks[3], (GLA_HEADS * GLA_DK,), jnp.float32)
    gla_norm_w = 1.0 + 0.02 * jax.random.normal(ks[4], (GLA_DV,), jnp.float32)
    w_out = jax.random.normal(ks[5], (ATT_W + GLA_W, D_MODEL), jnp.float32) * (ATT_W + GLA_W) ** -0.5
    pre_norm_w = 1.0 + 0.02 * jax.random.normal(ks[6], (D_MODEL,), jnp.float32)
    post_norm_w = 1.0 + 0.02 * jax.random.normal(ks[7], (D_MODEL,), jnp.float32)
    return {"x": x, "positions": positions, "w_in": w_in, "w_gla_gate_up": w_gla_gate_up,
            "b_gla_gate": b_gla_gate, "gla_norm_w": gla_norm_w, "w_out": w_out,
            "pre_norm_w": pre_norm_w, "post_norm_w": post_norm_w}


def reference(x, positions, w_in, w_gla_gate_up, b_gla_gate, gla_norm_w, w_out, pre_norm_w, post_norm_w):
    B, S, _ = x.shape
    cos, sin = _rope_tables(positions)
    for _layer in range(DEPTH):
        h = _rmsnorm(x, pre_norm_w)
        proj = h @ w_in
        splits = []
        acc = 0
        for sz in PROJ_SIZES[:-1]:
            acc += sz
            splits.append(acc)
        (a_q, a_k, a_v, i_q, i_k, i_w, a_gate,
         g_q, g_k, g_v, g_lr, g_gate) = jnp.split(proj, splits, axis=-1)

        a_q = _partial_rope(a_q.reshape(B, S, ATT_HEADS, ATT_HD), cos, sin)
        a_k = _partial_rope(a_k.reshape(B, S, ATT_HEADS, ATT_HD), cos, sin)
        a_v = a_v.reshape(B, S, ATT_HEADS, ATT_HD)
        i_q = _partial_rope(i_q.reshape(B, S, IDX_HEADS, IDX_HD), cos, sin)
        i_k = _partial_rope(i_k.reshape(B, S, 1, IDX_HD), cos, sin)[:, :, 0, :]
        i_w = i_w * (IDX_HEADS ** -0.5 * IDX_HD ** -0.5)
        att = _dsa_attention(a_q, a_k, a_v, i_q, i_k, i_w).reshape(B, S, ATT_W)
        att = att * jax.nn.silu(a_gate)

        log_alpha = jax.nn.log_sigmoid((g_lr @ w_gla_gate_up + b_gla_gate).astype(jnp.float32)) / GLA_TAU
        o = _gla(g_q.reshape(B, S, GLA_HEADS, GLA_DK), g_k.reshape(B, S, GLA_HEADS, GLA_DK),
                 g_v.reshape(B, S, GLA_HEADS, GLA_DV), log_alpha.reshape(B, S, GLA_HEADS, GLA_DK))
        o = _rmsnorm(o, gla_norm_w).astype(x.dtype).reshape(B, S, GLA_W)
        gla = o * jax.nn.silu(g_gate)

        mixed = jnp.concatenate([att, gla], axis=-1) @ w_out
        x = x + _rmsnorm(mixed, post_norm_w)
    return x
```

```python
import functools

import jax
import jax.numpy as jnp
import numpy as np
from jax import lax
from jax.experimental import pallas as pl
from jax.experimental.pallas import tpu as pltpu

F32 = jnp.float32
BF16 = jnp.bfloat16
I32 = jnp.int32
HIGHEST = lax.Precision.HIGHEST

ATT_HEADS = 8
ATT_HD = 64
ATT_W = ATT_HEADS * ATT_HD
IDX_HEADS = 4
IDX_HD = 64
TOPK_MAX = 256
GLA_HEADS = 4
GLA_DK = 64
GLA_DV = 128
GLA_QKW = GLA_HEADS * GLA_DK
GLA_W = GLA_HEADS * GLA_DV
GLA_GATE_RANK = 16
GLA_TAU = 16.0
GLA_CHUNK = 64
GLA_SUB = 16
ROPE_THETA = 500000.0
ROT_DIM = ATT_HD // 4
ROT_HALF = ROT_DIM // 2
NORM_EPS = 1e-6
NEG = -1e30

LANES = 128
MXU_DIM = 256
VMEM_LIMIT = 56 * 1024 * 1024

PROJ_TM = 512
DSA_TQ = 128
DSA_TK = 512
INT_MIN = -(2 ** 31)


def _nt_dot(a, b):
    return lax.dot_general(a, b, (((1,), (1,)), ((), ())), preferred_element_type=F32)


def _tn_dot(a, b, precision=None):
    return lax.dot_general(a, b, (((0,), (0,)), ((), ())), preferred_element_type=F32,
                           precision=precision)


def _rope_table_kernel(pos_ref, inv_ref, cos_ref, sin_ref):
    ang = pos_ref[...].astype(F32) * inv_ref[...]
    cos_ref[...] = jnp.cos(ang)
    sin_ref[...] = jnp.sin(ang)


def _rope_tables(positions):
    n = positions.size
    inv = ROPE_THETA ** (-jnp.arange(0, ROT_DIM, 2, dtype=F32) / ROT_DIM)
    cos8, sin8 = pl.pallas_call(
        _rope_table_kernel,
        out_shape=(jax.ShapeDtypeStruct((ROT_HALF, n), F32),) * 2,
        name="rope_tables",
    )(positions.reshape(1, n), inv.reshape(ROT_HALF, 1))
    cos_t, sin_t = cos8.T, sin8.T
    one = jnp.ones((n, ATT_HD - ROT_DIM), F32)
    zero8 = jnp.zeros((n, ROT_HALF), F32)
    zero = jnp.zeros((n, ATT_HD - ROT_DIM), F32)
    c = jnp.concatenate([cos_t, cos_t, one], axis=1)
    s_up = jnp.concatenate([-sin_t, zero8, zero], axis=1)
    s_dn = jnp.concatenate([zero8, sin_t, zero], axis=1)
    tile2 = lambda t: jnp.concatenate([t, t], axis=1)
    return tile2(c), tile2(s_up), tile2(s_dn)


def _proj_kernel(x_ref, pnw_ref, c_ref, su_ref, sd_ref,
                 w_aq, w_ak, w_av, w_iq, w_ik4, w_iw, w_ag, w_gqk, w_gv, w_glr, w_gg,
                 aq_o, ak_o, av_o, iq_o, ik4_o, iw_o, ag_o, gqk_o, gv_o, glr_o, gg_o):
    x = x_ref[...]
    ms = jnp.mean(x * x, axis=-1, keepdims=True)
    h = (x * lax.rsqrt(ms + NORM_EPS)) * pnw_ref[...]
    hb = h.astype(BF16)

    def mm(w_ref):
        return jnp.dot(hb, w_ref[...], preferred_element_type=F32)

    c1, su1, sd1 = c_ref[...], su_ref[...], sd_ref[...]

    def rope(y):
        n = y.shape[1]
        reps = n // LANES
        c = jnp.concatenate([c1] * reps, axis=1)
        su = jnp.concatenate([su1] * reps, axis=1)
        sd = jnp.concatenate([sd1] * reps, axis=1)
        return y * c + pltpu.roll(y, n - ROT_HALF, 1) * su + pltpu.roll(y, ROT_HALF, 1) * sd

    aq_o[...] = (rope(mm(w_aq)) * (ATT_HD ** -0.5)).astype(BF16)
    ak_o[...] = rope(mm(w_ak)).astype(BF16)
    av_o[...] = mm(w_av).astype(BF16)
    iq_o[...] = rope(mm(w_iq)).astype(BF16)
    ik4_o[...] = rope(mm(w_ik4)).astype(BF16)
    iw_o[...] = mm(w_iw) * (IDX_HEADS ** -0.5 * IDX_HD ** -0.5)
    a_gate = mm(w_ag)
    ag_o[...] = a_gate * jax.nn.sigmoid(a_gate)
    gqk = mm(w_gqk)
    lane = lax.broadcasted_iota(I32, gqk.shape, 1)
    gqk_o[...] = jnp.where(lane < GLA_QKW, gqk * (GLA_DK ** -0.5), gqk)
    gv_o[...] = mm(w_gv)
    glr_o[...] = mm(w_glr)
    g_gate = mm(w_gg)
    gg_o[...] = g_gate * jax.nn.sigmoid(g_gate)


def _split_w_in(w_in):
    sizes = (ATT_W, ATT_W, ATT_W, IDX_HEADS * IDX_HD, IDX_HD, IDX_HEADS, ATT_W,
             GLA_QKW, GLA_QKW, GLA_W, GLA_GATE_RANK, GLA_W)
    offs = np.cumsum((0,) + sizes)
    seg = [w_in[:, offs[i]:offs[i + 1]] for i in range(len(sizes))]
    a_q, a_k, a_v, i_q, i_k, i_w, a_g, g_q, g_k, g_v, g_lr, g_g = seg
    d = w_in.shape[0]
    pad = lambda w, n: jnp.concatenate([w, jnp.zeros((d, n - w.shape[1]), w.dtype)], axis=1)
    ws = [a_q, a_k, a_v, i_q, jnp.concatenate([i_k] * IDX_HEADS, axis=1), pad(i_w, LANES), a_g,
          jnp.concatenate([g_q, g_k], axis=1), g_v, pad(g_lr, LANES), g_g]
    return [w.astype(BF16) for w in ws]


def _project(x2, pre_norm_w, tables, w_in):
    n, d = x2.shape
    tm = min(PROJ_TM, n)
    ws = _split_w_in(w_in)
    widths = [w.shape[1] for w in ws]
    out_dtypes = [BF16, BF16, BF16, BF16, BF16, F32, F32, F32, F32, F32, F32]
    row = lambda width: pl.BlockSpec((tm, width), lambda i: (i, 0))
    full = lambda a: pl.BlockSpec(a.shape, lambda i: (0, 0))
    return pl.pallas_call(
        _proj_kernel,
        grid=(n // tm,),
        in_specs=[row(d), full(pre_norm_w), row(LANES), row(LANES), row(LANES)] + [full(w) for w in ws],
        out_specs=[row(wd) for wd in widths],
        out_shape=[jax.ShapeDtypeStruct((n, wd), dt) for wd, dt in zip(widths, out_dtypes)],
        compiler_params=pltpu.CompilerParams(dimension_semantics=("arbitrary",),
                                             vmem_limit_bytes=VMEM_LIMIT),
        name="prenorm_in_proj",
    )(x2, pre_norm_w, *tables, *ws)


def _dsa_kernel(aq_ref, ak_ref, av_ref, iq_ref, ik4_ref, iw_ref, ag_ref, out_ref,
                keys_scr, bias_scr, *, ksel):
    tq, tk = DSA_TQ, DSA_TK
    i = pl.program_id(1)
    nck = (i * tq + tq + tk - 1) // tk
    nsub = tk // LANES

    row_pos = i * tq + lax.broadcasted_iota(I32, (tq, LANES), 0)
    lane_pos = lax.broadcasted_iota(I32, (tq, LANES), 1)

    iq = iq_ref[0]
    head_of_lane = lax.broadcasted_iota(I32, iq.shape, 1) // IDX_HD
    q_stack = jnp.concatenate(
        [jnp.where(head_of_lane == h, iq, jnp.zeros_like(iq)) for h in range(IDX_HEADS)], axis=0)
    iw = iw_ref[0]
    w_cols = [iw[:, h:h + 1] for h in range(IDX_HEADS)]

    def score_chunk(c, carry):
        kc = ik4_ref[0, pl.ds(pl.multiple_of(c * tk, tk), tk), :]
        rel = _nt_dot(q_stack, kc)
        sc = w_cols[0] * jnp.maximum(rel[0:tq], 0.0)
        for h in range(1, IDX_HEADS):
            sc = sc + w_cols[h] * jnp.maximum(rel[h * tq:(h + 1) * tq], 0.0)
        for j in range(nsub):
            s = sc[:, j * LANES:(j + 1) * LANES]
            s = jnp.where(s == 0.0, 0.0, s)
            col = c * tk + j * LANES + lane_pos
            s = jnp.where(col <= row_pos, s, NEG)
            bits = pltpu.bitcast(s, I32)
            keys_scr[c, :, j * LANES:(j + 1) * LANES] = bits ^ ((bits >> 31) & 0x7FFFFFFF)
        return carry

    lax.fori_loop(0, nck, score_chunk, 0)

    def count(pred):
        def body(c, acc):
            kc = keys_scr[c]
            for j in range(nsub):
                acc = acc + jnp.where(pred(kc[:, j * LANES:(j + 1) * LANES]), 1, 0)
            return acc
        acc = lax.fori_loop(0, nck, body, jnp.zeros((tq, LANES), I32))
        return jnp.sum(acc, axis=1, keepdims=True)

    def bisect(it, prefix):
        cand = prefix + lax.shift_left(jnp.int32(1), 31 - it)
        cnt = count(lambda k: k >= cand)
        return jnp.where(cnt >= ksel, cand, prefix)

    vstar = lax.fori_loop(0, 32, bisect, jnp.full((tq, LANES), INT_MIN, I32))
    need = ksel - count(lambda k: k > vstar)
    n_eq = count(lambda k: k == vstar)

    def bias_chunk(c, carry):
        kc = keys_scr[c]
        for j in range(nsub):
            sel = kc[:, j * LANES:(j + 1) * LANES] > vstar
            bias_scr[c, :, j * LANES:(j + 1) * LANES] = jnp.where(sel, 0.0, NEG)
        return carry

    lax.fori_loop(0, nck, bias_chunk, 0)

    r_i = lax.broadcasted_iota(I32, (LANES, 2 * LANES), 0)
    c_i = lax.broadcasted_iota(I32, (LANES, 2 * LANES), 1)
    prefix_mat = jnp.where((r_i <= c_i) | (c_i >= LANES), 1.0, 0.0).astype(BF16)
    need_f = need.astype(F32)

    def tie_chunk(c, run):
        kc = keys_scr[c]
        for j in range(nsub):
            eq = kc[:, j * LANES:(j + 1) * LANES] == vstar
            pr = jnp.dot(jnp.where(eq, 1.0, 0.0).astype(BF16), prefix_mat, preferred_element_type=F32)
            rank = pr[:, :LANES] + run
            col = c * tk + j * LANES + lane_pos
            take = eq & (rank <= need_f) & (col <= row_pos)
            old = bias_scr[c, :, j * LANES:(j + 1) * LANES]
            bias_scr[c, :, j * LANES:(j + 1) * LANES] = jnp.where(take, 0.0, old)
            run = run + pr[:, LANES:]
        return run

    lax.fori_loop(0, nck, tie_chunk, jnp.zeros((tq, LANES), F32))

    q = aq_ref[0]
    lane_head = lax.broadcasted_iota(I32, (tq, MXU_DIM), 1) // ATT_HD
    heads_per_half = MXU_DIM // ATT_HD
    halves = []
    for half in range(ATT_W // MXU_DIM):
        qh = q[:, half * MXU_DIM:(half + 1) * MXU_DIM]
        out_half = jnp.zeros((tq, MXU_DIM), F32)
        for hh in range(heads_per_half):
            qm = jnp.where(lane_head == hh, qh, jnp.zeros_like(qh))

            def att_chunk(c, carry, qm=qm, half=half):
                m, l, acc = carry
                rows = pl.ds(pl.multiple_of(c * tk, tk), tk)
                kc = ak_ref[0, rows, half * MXU_DIM:(half + 1) * MXU_DIM]
                s = _nt_dot(qm, kc) + bias_scr[c]
                m_new = jnp.maximum(m, jnp.max(s, axis=1, keepdims=True))
                alpha = jnp.exp(m - m_new)
                p = jnp.exp(s - m_new)
                l = alpha * l + jnp.sum(p, axis=1, keepdims=True)
                vc = av_ref[0, rows, half * MXU_DIM:(half + 1) * MXU_DIM]
                acc = alpha * acc + jnp.dot(p.astype(BF16), vc, preferred_element_type=F32)
                return m_new, l, acc

            m0 = jnp.full((tq, 1), NEG, F32)
            l0 = jnp.zeros((tq, 1), F32)
            a0 = jnp.zeros((tq, MXU_DIM), F32)
            _, l, acc = lax.fori_loop(0, nck, att_chunk, (m0, l0, a0))
            out_half = out_half + jnp.where(lane_head == hh, acc * (1.0 / l), 0.0)
        halves.append(out_half)
    att = jnp.concatenate(halves, axis=1)
    out_ref[0] = (att * ag_ref[0]).astype(BF16)


def _dsa_attention(aq, ak, av, iq, ik4, iw, ag):
    b, s, _ = aq.shape
    tq, tk = DSA_TQ, DSA_TK
    ksel = min(TOPK_MAX, s // 4)
    qblk = lambda width: pl.BlockSpec((1, tq, width), lambda bi, i: (bi, i, 0))
    seq = lambda width: pl.BlockSpec((1, s, width), lambda bi, i: (bi, 0, 0))
    return pl.pallas_call(
        functools.partial(_dsa_kernel, ksel=ksel),
        grid=(b, s // tq),
        in_specs=[qblk(ATT_W), seq(ATT_W), seq(ATT_W), qblk(IDX_HEADS * IDX_HD),
                  seq(IDX_HEADS * IDX_HD), qblk(LANES), qblk(ATT_W)],
        out_specs=qblk(ATT_W),
        out_shape=jax.ShapeDtypeStruct((b, s, ATT_W), BF16),
        scratch_shapes=[pltpu.VMEM((s // tk, tq, tk), I32), pltpu.VMEM((s // tk, tq, tk), F32)],
        compiler_params=pltpu.CompilerParams(dimension_semantics=("arbitrary", "arbitrary"),
                                             vmem_limit_bytes=VMEM_LIMIT),
        name="dsa_attention",
    )(aq, ak, av, iq, ik4, iw, ag)


def _gla_kernel(gqk_ref, gv_ref, glr_ref, gg_ref, wup_ref, bup_ref, nw_ref, out_ref, state_scr):
    c_len, sub = GLA_CHUNK, GLA_SUB
    nb = gqk_ref.shape[0]
    n_sub = c_len // sub

    @pl.when(pl.program_id(0) == 0)
    def _():
        state_scr[...] = jnp.zeros_like(state_scr)

    t_i = lax.broadcasted_iota(I32, (c_len, GLA_QKW), 0)
    s_i = lax.broadcasted_iota(I32, (sub, GLA_QKW), 0)
    tril = jnp.where(lax.broadcasted_iota(I32, (c_len, c_len), 0)
                     >= lax.broadcasted_iota(I32, (c_len, c_len), 1), 1.0, 0.0)
    ones_cl = jnp.ones((c_len, LANES), F32)
    lane_head = lax.broadcasted_iota(I32, (c_len, GLA_QKW), 1) // GLA_DK

    def stack_heads(a):
        return jnp.concatenate([jnp.where(lane_head == h, a, 0.0) for h in range(GLA_HEADS)],
                               axis=0).astype(BF16)

    zr = lax.broadcasted_iota(I32, (n_sub * sub * sub, GLA_QKW), 0)
    zc = lax.broadcasted_iota(I32, (n_sub * sub * sub, GLA_QKW), 1)
    pick = ((zc % c_len) == (zr // (sub * sub)) * sub + zr % sub)
    hr = lax.broadcasted_iota(I32, (GLA_QKW, GLA_QKW), 0) // GLA_DK
    hc = lax.broadcasted_iota(I32, (GLA_QKW, GLA_QKW), 1) // c_len
    head_rep = jnp.where(hr == hc, 1.0, 0.0).astype(BF16)
    sr = lax.broadcasted_iota(I32, (c_len, n_sub * sub * sub), 0)
    sc_ = lax.broadcasted_iota(I32, (c_len, n_sub * sub * sub), 1)
    row_sel = jnp.where(sr == sc_ // sub, 1.0, 0.0).astype(BF16)
    vr = lax.broadcasted_iota(I32, (GLA_HEADS * c_len, GLA_W), 0) // c_len
    vc = lax.broadcasted_iota(I32, (GLA_HEADS * c_len, GLA_W), 1) // GLA_DV
    v_diag = vr == vc
    a2_keep = (lax.broadcasted_iota(I32, (GLA_HEADS * c_len, c_len), 0) % c_len) // (2 * sub) == \
        lax.broadcasted_iota(I32, (GLA_HEADS * c_len, c_len), 1) // (2 * sub)

    for bi in range(nb):
        gqk = gqk_ref[bi]
        q, k = gqk[:, :GLA_QKW], gqk[:, GLA_QKW:]
        v = gv_ref[bi]
        vb = v.astype(BF16)
        z = jnp.dot(glr_ref[bi], wup_ref[...], preferred_element_type=F32, precision=HIGHEST) + bup_ref[...]
        g = (jnp.minimum(z, 0.0) - jnp.log1p(jnp.exp(-jnp.abs(z)))) * (1.0 / GLA_TAU)
        b = jnp.dot(tril, g, preferred_element_type=F32, precision=HIGHEST)
        b_last_col = _tn_dot(g, ones_cl, precision=HIGHEST)
        b_last = b[c_len - 1:c_len, :]

        state = state_scr[bi]
        o_stack = jnp.dot(stack_heads(q * jnp.exp(b)), state.astype(BF16), preferred_element_type=F32)

        ref1 = b[2 * sub - 1:2 * sub, :]
        q1 = q * jnp.exp(jnp.where(t_i >= 2 * sub, b - ref1, NEG))
        k1 = k * jnp.exp(jnp.where(t_i < 2 * sub, ref1 - b, NEG))
        ref2 = jnp.where(t_i < 2 * sub, b[sub - 1:sub, :], b[3 * sub - 1:3 * sub, :])
        odd = (t_i // sub) % 2 == 1
        q2 = q * jnp.exp(jnp.where(odd, b - ref2, NEG))
        k2 = k * jnp.exp(jnp.where(odd, NEG, ref2 - b))
        a_off = _nt_dot(stack_heads(q1), k1.astype(BF16)) + \
            jnp.where(a2_keep, _nt_dot(stack_heads(q2), k2.astype(BF16)), 0.0)
        a_off = a_off.astype(BF16)

        zs = []
        for t in range(c_len):
            blk = t // sub
            rs = slice(blk * sub, (blk + 1) * sub)
            dec = jnp.exp(jnp.where(s_i <= t - blk * sub, b[t:t + 1, :] - b[rs], NEG))
            zs.append(q[t:t + 1, :] * k[rs] * dec)
        zmat = jnp.concatenate(zs, axis=0).astype(BF16)
        rep = jnp.dot(zmat, head_rep, preferred_element_type=F32)
        a_diag = jnp.dot(row_sel, jnp.where(pick, rep, 0.0).astype(BF16),
                         preferred_element_type=F32).astype(BF16)
        v_bd = jnp.where(v_diag, jnp.concatenate([vb] * GLA_HEADS, axis=0), jnp.zeros((), BF16))
        o = jnp.dot(a_diag, v_bd, preferred_element_type=F32)

        o_heads = []
        for h in range(GLA_HEADS):
            rs = slice(h * c_len, (h + 1) * c_len)
            o_h = o_stack[rs] + jnp.dot(a_off[rs], vb[:, h * GLA_DV:(h + 1) * GLA_DV],
                                        preferred_element_type=F32)
            o_heads.append(o_h)
        o = o + jnp.concatenate(o_heads, axis=1)

        k_dec = (k * jnp.exp(b_last - b)).astype(BF16)
        upd = _tn_dot(k_dec, vb)
        upd = jnp.concatenate([upd[h * GLA_DK:(h + 1) * GLA_DK, h * GLA_DV:(h + 1) * GLA_DV]
                               for h in range(GLA_HEADS)], axis=0)
        state_scr[bi] = jnp.exp(b_last_col) * state + upd

        outs = []
        for h in range(GLA_HEADS):
            oh = o[:, h * GLA_DV:(h + 1) * GLA_DV]
            oh = oh * lax.rsqrt(jnp.mean(oh * oh, axis=-1, keepdims=True) + NORM_EPS)
            outs.append(oh * nw_ref[...])
        out_ref[bi] = (jnp.concatenate(outs, axis=1) * gg_ref[bi]).astype(BF16)


def _gla(gqk, gv, glr, gg, w_up, b_up, norm_w):
    b, s, _ = gqk.shape
    c_len = GLA_CHUNK
    wup_pad = jnp.concatenate([w_up, jnp.zeros((LANES - GLA_GATE_RANK, GLA_QKW), F32)], axis=0)
    blk = lambda width: pl.BlockSpec((b, c_len, width), lambda c: (0, c, 0))
    full = lambda a: pl.BlockSpec(a.shape, lambda c: (0,) * a.ndim)
    b_up2, nw2 = b_up.reshape(1, GLA_QKW), norm_w.reshape(1, GLA_DV)
    return pl.pallas_call(
        _gla_kernel,
        grid=(s // c_len,),
        in_specs=[blk(2 * GLA_QKW), blk(GLA_W), blk(LANES), blk(GLA_W), full(wup_pad), full(b_up2), full(nw2)],
        out_specs=blk(GLA_W),
        out_shape=jax.ShapeDtypeStruct((b, s, GLA_W), BF16),
        scratch_shapes=[pltpu.VMEM((b, GLA_QKW, GLA_DV), F32)],
        compiler_params=pltpu.CompilerParams(dimension_semantics=("arbitrary",),
                                             vmem_limit_bytes=VMEM_LIMIT),
        name="gla_chunked",
    )(gqk, gv, glr, gg, wup_pad, b_up2, nw2)


def _out_kernel(att_ref, gla_ref, wa_ref, wg_ref, x_ref, nw_ref, out_ref):
    mixed = jnp.dot(att_ref[...], wa_ref[...], preferred_element_type=F32) + \
        jnp.dot(gla_ref[...], wg_ref[...], preferred_element_type=F32)
    y = mixed * lax.rsqrt(jnp.mean(mixed * mixed, axis=-1, keepdims=True) + NORM_EPS)
    out_ref[...] = x_ref[...] + y * nw_ref[...]


def _out_project(att2, gla2, w_out, x2, post_norm_w):
    n, d = x2.shape
    tm = min(PROJ_TM, n)
    wa, wg = w_out[:ATT_W].astype(BF16), w_out[ATT_W:].astype(BF16)
    row = lambda width: pl.BlockSpec((tm, width), lambda i: (i, 0))
    full = lambda a: pl.BlockSpec(a.shape, lambda i: (0, 0))
    return pl.pallas_call(
        _out_kernel,
        grid=(n // tm,),
        in_specs=[row(ATT_W), row(GLA_W), full(wa), full(wg), row(d), full(post_norm_w)],
        out_specs=row(d),
        out_shape=jax.ShapeDtypeStruct((n, d), F32),
        compiler_params=pltpu.CompilerParams(dimension_semantics=("arbitrary",),
                                             vmem_limit_bytes=VMEM_LIMIT),
        name="out_proj_postnorm",
    )(att2, gla2, wa, wg, x2, post_norm_w)


def kernel(x, positions, w_in, w_gla_gate_up, b_gla_gate, gla_norm_w, w_out, pre_norm_w, post_norm_w):
    b, s, d = x.shape
    n = b * s
    x2 = x.reshape(n, d)
    tables = _rope_tables(positions)
    aq, ak, av, iq, ik4, iw, ag, gqk, gv, glr, gg = _project(x2, pre_norm_w.reshape(1, d), tables, w_in)
    r3 = lambda a: a.reshape(b, s, a.shape[-1])
    att = _dsa_attention(r3(aq), r3(ak), r3(av), r3(iq), r3(ik4), r3(iw), r3(ag))
    gla = _gla(r3(gqk), r3(gv), r3(glr), r3(gg), w_gla_gate_up, b_gla_gate, gla_norm_w)
    out = _out_project(att.reshape(n, ATT_W), gla.reshape(n, GLA_W), w_out, x2, post_norm_w.reshape(1, d))
    return out.reshape(b, s, d)
```

```python
import functools

import jax
import jax.numpy as jnp
import numpy as np
from jax import lax
from jax.experimental import pallas as pl
from jax.experimental.pallas import tpu as pltpu

F32 = jnp.float32
BF16 = jnp.bfloat16
I32 = jnp.int32
HIGHEST = lax.Precision.HIGHEST

ATT_HEADS = 8
ATT_HD = 64
ATT_W = ATT_HEADS * ATT_HD
IDX_HEADS = 4
IDX_HD = 64
IDX_QW = IDX_HEADS * IDX_HD
TOPK_MAX = 256
GLA_HEADS = 4
GLA_DK = 64
GLA_DV = 128
GLA_QKW = GLA_HEADS * GLA_DK
GLA_W = GLA_HEADS * GLA_DV
GLA_GATE_RANK = 16
GLA_TAU = 16.0
GLA_CHUNK = 64
GLA_SUB = 16
ROPE_THETA = 500000.0
ROT_DIM = ATT_HD // 4
ROT_HALF = ROT_DIM // 2
NORM_EPS = 1e-6
NEG = -1e30

SUBLANES = 8
LANES = 128
MXU_DIM = 256
VMEM_LIMIT = 56 * 1024 * 1024

PROJ_TM = 512
DSA_TQ = 256
DSA_TK = 256
INT_MIN = -(2 ** 31)


def _nt_dot(a, b):
    return lax.dot_general(a, b, (((1,), (1,)), ((), ())), preferred_element_type=F32)


def _tn_dot(a, b, precision=None):
    return lax.dot_general(a, b, (((0,), (0,)), ((), ())), preferred_element_type=F32,
                           precision=precision)


def _rope_table_kernel(pos_ref, inv_ref, cos_ref, sin_ref):
    ang = pos_ref[...].astype(F32) * inv_ref[...]
    cos_ref[...] = jnp.cos(ang)
    sin_ref[...] = jnp.sin(ang)


def _rope_tables(positions):
    n = positions.size
    inv = ROPE_THETA ** (-jnp.arange(0, ROT_DIM, 2, dtype=F32) / ROT_DIM)
    return pl.pallas_call(
        _rope_table_kernel,
        out_shape=(jax.ShapeDtypeStruct((ROT_HALF, n), F32),) * 2,
        name="rope_tables",
    )(positions.reshape(1, n), inv.reshape(ROT_HALF, 1))


def _proj_kernel(x_ref, pnw_ref, cos_ref, sin_ref,
                 wt_aq, wt_ak, wt_av, wt_iq, wt_ik, wt_iw, w_ag, w_gqk, w_gv, w_glr, w_gg,
                 aqt_o, ak_o, avt_o, iqt_o, ik_o, iwt_o, ag_o, gqk_o, gv_o, glr_o, gg_o):
    x = x_ref[0]
    ms = jnp.mean(x * x, axis=-1, keepdims=True)
    h = (x * lax.rsqrt(ms + NORM_EPS)) * pnw_ref[...]
    hb = h.astype(BF16)
    cos8, sin8 = cos_ref[...], sin_ref[...]

    def mm(w_ref):
        return jnp.dot(hb, w_ref[...], preferred_element_type=F32)

    def mm_t(wt_ref):
        return _nt_dot(wt_ref[...], hb)

    def rope_t(yt):
        pieces = []
        for hd in range(yt.shape[0] // ATT_HD):
            rows = yt[hd * ATT_HD:(hd + 1) * ATT_HD]
            x1, x2 = rows[:ROT_HALF], rows[ROT_HALF:ROT_DIM]
            pieces += [x1 * cos8 - x2 * sin8, x2 * cos8 + x1 * sin8, rows[ROT_DIM:]]
        return jnp.concatenate(pieces, axis=0)

    aqt_o[0] = (rope_t(mm_t(wt_aq)) * (ATT_HD ** -0.5)).astype(BF16)
    ak_o[0] = rope_t(mm_t(wt_ak)).T.astype(BF16)
    avt = mm_t(wt_av).astype(BF16)
    for c in range(avt_o.shape[1]):
        avt_o[0, c] = avt[:, c * DSA_TK:(c + 1) * DSA_TK]
    iqt_o[0] = rope_t(mm_t(wt_iq)).astype(BF16)
    ikt = mm_t(wt_ik)
    ik_o[0] = jnp.concatenate([rope_t(ikt[:IDX_HD]), ikt[IDX_HD:]], axis=0).T.astype(BF16)
    iwt_o[0] = mm_t(wt_iw) * (IDX_HEADS ** -0.5 * IDX_HD ** -0.5)
    a_gate = mm(w_ag)
    ag_o[0] = a_gate * jax.nn.sigmoid(a_gate)
    gqk = mm(w_gqk)
    lane = lax.broadcasted_iota(I32, gqk.shape, 1)
    gqk_o[0] = jnp.where(lane < GLA_QKW, gqk * (GLA_DK ** -0.5), gqk)
    gv_o[0] = mm(w_gv)
    glr_o[0] = mm(w_glr)
    g_gate = mm(w_gg)
    gg_o[0] = g_gate * jax.nn.sigmoid(g_gate)


def _split_w_in(w_in):
    sizes = (ATT_W, ATT_W, ATT_W, IDX_QW, IDX_HD, IDX_HEADS, ATT_W,
             GLA_QKW, GLA_QKW, GLA_W, GLA_GATE_RANK, GLA_W)
    offs = np.cumsum((0,) + sizes)
    seg = [w_in[:, offs[i]:offs[i + 1]] for i in range(len(sizes))]
    a_q, a_k, a_v, i_q, i_k, i_w, a_g, g_q, g_k, g_v, g_lr, g_g = seg
    d = w_in.shape[0]
    pad = lambda w, n: jnp.concatenate([w, jnp.zeros((d, n - w.shape[1]), w.dtype)], axis=1)
    transposed = [a_q.T, a_k.T, a_v.T, i_q.T, pad(i_k, LANES).T, pad(i_w, SUBLANES).T]
    natural = [a_g, jnp.concatenate([g_q, g_k], axis=1), g_v, pad(g_lr, LANES), g_g]
    return [w.astype(BF16) for w in transposed], [w.astype(BF16) for w in natural]


def _project(x, pre_norm_w, cos8, sin8, w_in):
    b, s, d = x.shape
    tm = min(PROJ_TM, s)
    nt = s // tm
    cpt = tm // DSA_TK
    wts, wns = _split_w_in(w_in)
    full = lambda a: pl.BlockSpec(a.shape, lambda bi, i: (0,) * a.ndim)
    nat = lambda width: pl.BlockSpec((1, tm, width), lambda bi, i: (bi, i, 0))
    tr = lambda rows: pl.BlockSpec((1, rows, tm), lambda bi, i: (bi, 0, i))
    tab = pl.BlockSpec((ROT_HALF, tm), lambda bi, i: (0, bi * nt + i))
    out_specs = [tr(ATT_W), nat(ATT_W), pl.BlockSpec((1, cpt, ATT_W, DSA_TK), lambda bi, i: (bi, i, 0, 0)),
                 tr(IDX_QW), nat(LANES), tr(SUBLANES),
                 nat(ATT_W), nat(2 * GLA_QKW), nat(GLA_W), nat(LANES), nat(GLA_W)]
    sds = jax.ShapeDtypeStruct
    out_shape = [sds((b, ATT_W, s), BF16), sds((b, s, ATT_W), BF16), sds((b, s // DSA_TK, ATT_W, DSA_TK), BF16),
                 sds((b, IDX_QW, s), BF16), sds((b, s, LANES), BF16), sds((b, SUBLANES, s), F32),
                 sds((b, s, ATT_W), F32), sds((b, s, 2 * GLA_QKW), F32), sds((b, s, GLA_W), F32),
                 sds((b, s, LANES), F32), sds((b, s, GLA_W), F32)]
    return pl.pallas_call(
        _proj_kernel,
        grid=(b, nt),
        in_specs=[nat(d), full(pre_norm_w), tab, tab] + [full(w) for w in wts + wns],
        out_specs=out_specs,
        out_shape=out_shape,
        compiler_params=pltpu.CompilerParams(dimension_semantics=("arbitrary", "arbitrary"),
                                             vmem_limit_bytes=VMEM_LIMIT),
        name="prenorm_in_proj",
    )(x, pre_norm_w, cos8, sin8, *wts, *wns)


def _key_to_f32(key):
    return pltpu.bitcast(key ^ ((key >> 31) & 0x7FFFFFFF), F32)


def _dsa_kernel(aqt_ref, ak_ref, avt_ref, iqt_ref, ik_ref, iwt_ref, ag_ref, out_ref,
                score_scr, bias_scr, qm_scr, m_scr, l_scr, acc_scr, *, ksel):
    tq, tk = DSA_TQ, DSA_TK
    i = pl.program_id(1)
    nck = (i * tq + tq + tk - 1) // tk
    groups = tk // SUBLANES

    key_iota = lax.broadcasted_iota(I32, (tk, tq), 0)
    q_pos = i * tq + lax.broadcasted_iota(I32, (tk, tq), 1)

    iqt = iqt_ref[0]
    q_cat = jnp.concatenate([iqt[h * IDX_HD:(h + 1) * IDX_HD] for h in range(IDX_HEADS)], axis=1)
    q_cat = jnp.concatenate([q_cat, jnp.zeros((LANES - IDX_HD, IDX_HEADS * tq), BF16)], axis=0)
    iwt = iwt_ref[0]
    w_rows = [iwt[h:h + 1, :] for h in range(IDX_HEADS)]

    def score_chunk(c, carry):
        kc = ik_ref[0, pl.ds(pl.multiple_of(c * tk, tk), tk), :]
        rel = jnp.dot(kc, q_cat, preferred_element_type=F32)
        sc = w_rows[0] * jnp.maximum(rel[:, 0:tq], 0.0)
        for h in range(1, IDX_HEADS):
            sc = sc + w_rows[h] * jnp.maximum(rel[:, h * tq:(h + 1) * tq], 0.0)
        score_scr[c] = jnp.where(c * tk + key_iota <= q_pos, sc, NEG)
        return carry

    lax.fori_loop(0, nck, score_chunk, 0)

    def count(pred):
        def body(c, acc):
            sc = score_scr[c].reshape(groups, SUBLANES, tq)
            return acc + jnp.sum(jnp.where(pred(sc), 1, 0), axis=0)
        acc = lax.fori_loop(0, nck, body, jnp.zeros((SUBLANES, tq), I32))
        return jnp.sum(acc, axis=0, keepdims=True)

    def bisect(it, prefix):
        cand = prefix + lax.shift_left(jnp.int32(1), 31 - it)
        cand_f = _key_to_f32(cand)
        cnt = count(lambda sc: sc >= cand_f[None])
        return jnp.where(cnt >= ksel, cand, prefix)

    vkey = lax.fori_loop(0, 32, bisect, jnp.full((SUBLANES, tq), INT_MIN, I32))
    vstar = _key_to_f32(vkey)
    need = ksel - count(lambda sc: sc > vstar[None])
    n_eq = count(lambda sc: sc == vstar[None])

    def bias_chunk(c, carry):
        bias_scr[c] = jnp.where(score_scr[c] > vstar[0:1], 0.0, NEG)
        return carry

    lax.fori_loop(0, nck, bias_chunk, 0)

    @pl.when(jnp.max(n_eq - need) > 0)
    def _():
        tri = jnp.where(lax.broadcasted_iota(I32, (tk, tk), 1) <= lax.broadcasted_iota(I32, (tk, tk), 0),
                        1.0, 0.0).astype(BF16)
        need_f = need.astype(F32)

        def tie_chunk(c, run):
            eq = score_scr[c] == vstar[0:1]
            rank = jnp.dot(tri, jnp.where(eq, 1.0, 0.0).astype(BF16), preferred_element_type=F32) + run
            take = eq & (rank <= need_f) & (c * tk + key_iota <= q_pos)
            bias_scr[c] = jnp.where(take, 0.0, bias_scr[c])
            return rank[tk - 1:tk, :]

        lax.fori_loop(0, nck, tie_chunk, jnp.zeros((1, tq), F32))

    heads_per_half = MXU_DIM // ATT_HD
    row_head = lax.broadcasted_iota(I32, (MXU_DIM, tq), 0) // ATT_HD
    for h in range(ATT_HEADS):
        half = h // heads_per_half
        qh = aqt_ref[0, half * MXU_DIM:(half + 1) * MXU_DIM, :]
        qm_scr[h] = jnp.where(row_head == h % heads_per_half, qh, jnp.zeros_like(qh))
    m_scr[...] = jnp.full_like(m_scr, NEG)
    l_scr[...] = jnp.zeros_like(l_scr)
    acc_scr[...] = jnp.zeros_like(acc_scr)

    def att_chunk(c, carry):
        rows = pl.ds(pl.multiple_of(c * tk, tk), tk)
        bias = bias_scr[c]
        for h in range(ATT_HEADS):
            half = h // heads_per_half
            kc = ak_ref[0, rows, half * MXU_DIM:(half + 1) * MXU_DIM]
            s = jnp.dot(kc, qm_scr[h], preferred_element_type=F32) + bias
            m_old = m_scr[h]
            m_new = jnp.maximum(m_old, jnp.max(s, axis=0, keepdims=True))
            alpha = jnp.exp(m_old - m_new)
            p = jnp.exp(s - m_new)
            l_scr[h] = alpha * l_scr[h] + jnp.sum(p, axis=0, keepdims=True)
            vt = avt_ref[0, c, h * ATT_HD:(h + 1) * ATT_HD, :]
            acc_scr[h] = alpha * acc_scr[h] + jnp.dot(vt, p.astype(BF16), preferred_element_type=F32)
            m_scr[h] = m_new
        return carry

    lax.fori_loop(0, nck, att_chunk, 0)

    att_t = jnp.concatenate([acc_scr[h] * (1.0 / l_scr[h]) for h in range(ATT_HEADS)], axis=0)
    out_ref[0] = (att_t.T * ag_ref[0]).astype(BF16)


def _dsa_attention(aqt, ak, avt, iqt, ik, iwt, ag):
    b, s, _ = ak.shape
    tq, tk = DSA_TQ, DSA_TK
    nck = s // tk
    ksel = min(TOPK_MAX, s // 4)
    qt = lambda rows: pl.BlockSpec((1, rows, tq), lambda bi, i: (bi, 0, i))
    seq = lambda width: pl.BlockSpec((1, s, width), lambda bi, i: (bi, 0, 0))
    qn = pl.BlockSpec((1, tq, ATT_W), lambda bi, i: (bi, i, 0))
    return pl.pallas_call(
        functools.partial(_dsa_kernel, ksel=ksel),
        grid=(b, s // tq),
        in_specs=[qt(ATT_W), seq(ATT_W), pl.BlockSpec((1, nck, ATT_W, tk), lambda bi, i: (bi, 0, 0, 0)),
                  qt(IDX_QW), seq(LANES), qt(SUBLANES), qn],
        out_specs=qn,
        out_shape=jax.ShapeDtypeStruct((b, s, ATT_W), BF16),
        scratch_shapes=[pltpu.VMEM((nck, tk, tq), F32), pltpu.VMEM((nck, tk, tq), F32),
                        pltpu.VMEM((ATT_HEADS, MXU_DIM, tq), BF16),
                        pltpu.VMEM((ATT_HEADS, 1, tq), F32), pltpu.VMEM((ATT_HEADS, 1, tq), F32),
                        pltpu.VMEM((ATT_HEADS, ATT_HD, tq), F32)],
        compiler_params=pltpu.CompilerParams(dimension_semantics=("arbitrary", "arbitrary"),
                                             vmem_limit_bytes=VMEM_LIMIT),
        name="dsa_attention",
    )(aqt, ak, avt, iqt, ik, iwt, ag)


def _gla_kernel(gqk_ref, gv_ref, glr_ref, gg_ref, wup_ref, bup_ref, nw_ref, out_ref, state_scr):
    c_len, sub = GLA_CHUNK, GLA_SUB
    nb = gqk_ref.shape[0]
    n_sub = c_len // sub

    @pl.when(pl.program_id(0) == 0)
    def _():
        state_scr[...] = jnp.zeros_like(state_scr)

    t_i = lax.broadcasted_iota(I32, (c_len, GLA_QKW), 0)
    s_i = lax.broadcasted_iota(I32, (sub, GLA_QKW), 0)
    tril = jnp.where(lax.broadcasted_iota(I32, (c_len, c_len), 0)
                     >= lax.broadcasted_iota(I32, (c_len, c_len), 1), 1.0, 0.0)
    ones_cl = jnp.ones((c_len, LANES), F32)
    lane_head = lax.broadcasted_iota(I32, (c_len, GLA_QKW), 1) // GLA_DK

    def stack_heads(a):
        return jnp.concatenate([jnp.where(lane_head == h, a, 0.0) for h in range(GLA_HEADS)],
                               axis=0).astype(BF16)

    zr = lax.broadcasted_iota(I32, (n_sub * sub * sub, GLA_QKW), 0)
    zc = lax.broadcasted_iota(I32, (n_sub * sub * sub, GLA_QKW), 1)
    pick = ((zc % c_len) == (zr // (sub * sub)) * sub + zr % sub)
    hr = lax.broadcasted_iota(I32, (GLA_QKW, GLA_QKW), 0) // GLA_DK
    hc = lax.broadcasted_iota(I32, (GLA_QKW, GLA_QKW), 1) // c_len
    head_rep = jnp.where(hr == hc, 1.0, 0.0).astype(BF16)
    sr = lax.broadcasted_iota(I32, (c_len, n_sub * sub * sub), 0)
    sc_ = lax.broadcasted_iota(I32, (c_len, n_sub * sub * sub), 1)
    row_sel = jnp.where(sr == sc_ // sub, 1.0, 0.0).astype(BF16)
    vr = lax.broadcasted_iota(I32, (GLA_HEADS * c_len, GLA_W), 0) // c_len
    vc = lax.broadcasted_iota(I32, (GLA_HEADS * c_len, GLA_W), 1) // GLA_DV
    v_diag = vr == vc
    a2_keep = (lax.broadcasted_iota(I32, (GLA_HEADS * c_len, c_len), 0) % c_len) // (2 * sub) == \
        lax.broadcasted_iota(I32, (GLA_HEADS * c_len, c_len), 1) // (2 * sub)

    for bi in range(nb):
        gqk = gqk_ref[bi]
        q, k = gqk[:, :GLA_QKW], gqk[:, GLA_QKW:]
        v = gv_ref[bi]
        vb = v.astype(BF16)
        z = jnp.dot(glr_ref[bi], wup_ref[...], preferred_element_type=F32, precision=HIGHEST) + bup_ref[...]
        g = (jnp.minimum(z, 0.0) - jnp.log1p(jnp.exp(-jnp.abs(z)))) * (1.0 / GLA_TAU)
        b = jnp.dot(tril, g, preferred_element_type=F32, precision=HIGHEST)
        b_last_col = _tn_dot(g, ones_cl, precision=HIGHEST)
        b_last = b[c_len - 1:c_len, :]

        state = state_scr[bi]
        o_stack = jnp.dot(stack_heads(q * jnp.exp(b)), state.astype(BF16), preferred_element_type=F32)

        ref1 = b[2 * sub - 1:2 * sub, :]
        q1 = q * jnp.exp(jnp.where(t_i >= 2 * sub, b - ref1, NEG))
        k1 = k * jnp.exp(jnp.where(t_i < 2 * sub, ref1 - b, NEG))
        ref2 = jnp.where(t_i < 2 * sub, b[sub - 1:sub, :], b[3 * sub - 1:3 * sub, :])
        odd = (t_i // sub) % 2 == 1
        q2 = q * jnp.exp(jnp.where(odd, b - ref2, NEG))
        k2 = k * jnp.exp(jnp.where(odd, NEG, ref2 - b))
        a_off = _nt_dot(stack_heads(q1), k1.astype(BF16)) + \
            jnp.where(a2_keep, _nt_dot(stack_heads(q2), k2.astype(BF16)), 0.0)
        a_off = a_off.astype(BF16)

        zs = []
        for t in range(c_len):
            blk = t // sub
            rs = slice(blk * sub, (blk + 1) * sub)
            dec = jnp.exp(jnp.where(s_i <= t - blk * sub, b[t:t + 1, :] - b[rs], NEG))
            zs.append(q[t:t + 1, :] * k[rs] * dec)
        zmat = jnp.concatenate(zs, axis=0).astype(BF16)
        rep = jnp.dot(zmat, head_rep, preferred_element_type=F32)
        a_diag = jnp.dot(row_sel, jnp.where(pick, rep, 0.0).astype(BF16),
                         preferred_element_type=F32).astype(BF16)
        v_bd = jnp.where(v_diag, jnp.concatenate([vb] * GLA_HEADS, axis=0), jnp.zeros((), BF16))
        o = jnp.dot(a_diag, v_bd, preferred_element_type=F32)

        o_heads = []
        for h in range(GLA_HEADS):
            rs = slice(h * c_len, (h + 1) * c_len)
            o_h = o_stack[rs] + jnp.dot(a_off[rs], vb[:, h * GLA_DV:(h + 1) * GLA_DV],
                                        preferred_element_type=F32)
            o_heads.append(o_h)
        o = o + jnp.concatenate(o_heads, axis=1)

        k_dec = (k * jnp.exp(b_last - b)).astype(BF16)
        upd = _tn_dot(k_dec, vb)
        upd = jnp.concatenate([upd[h * GLA_DK:(h + 1) * GLA_DK, h * GLA_DV:(h + 1) * GLA_DV]
                               for h in range(GLA_HEADS)], axis=0)
        state_scr[bi] = jnp.exp(b_last_col) * state + upd

        outs = []
        for h in range(GLA_HEADS):
            oh = o[:, h * GLA_DV:(h + 1) * GLA_DV]
            oh = oh * lax.rsqrt(jnp.mean(oh * oh, axis=-1, keepdims=True) + NORM_EPS)
            outs.append(oh * nw_ref[...])
        out_ref[bi] = (jnp.concatenate(outs, axis=1) * gg_ref[bi]).astype(BF16)


def _gla(gqk, gv, glr, gg, w_up, b_up, norm_w):
    b, s, _ = gqk.shape
    c_len = GLA_CHUNK
    wup_pad = jnp.concatenate([w_up, jnp.zeros((LANES - GLA_GATE_RANK, GLA_QKW), F32)], axis=0)
    blk = lambda width: pl.BlockSpec((b, c_len, width), lambda c: (0, c, 0))
    full = lambda a: pl.BlockSpec(a.shape, lambda c: (0,) * a.ndim)
    b_up2, nw2 = b_up.reshape(1, GLA_QKW), norm_w.reshape(1, GLA_DV)
    return pl.pallas_call(
        _gla_kernel,
        grid=(s // c_len,),
        in_specs=[blk(2 * GLA_QKW), blk(GLA_W), blk(LANES), blk(GLA_W), full(wup_pad), full(b_up2), full(nw2)],
        out_specs=blk(GLA_W),
        out_shape=jax.ShapeDtypeStruct((b, s, GLA_W), BF16),
        scratch_shapes=[pltpu.VMEM((b, GLA_QKW, GLA_DV), F32)],
        compiler_params=pltpu.CompilerParams(dimension_semantics=("arbitrary",),
                                             vmem_limit_bytes=VMEM_LIMIT),
        name="gla_chunked",
    )(gqk, gv, glr, gg, wup_pad, b_up2, nw2)


def _out_kernel(att_ref, gla_ref, wa_ref, wg_ref, x_ref, nw_ref, out_ref):
    mixed = jnp.dot(att_ref[...], wa_ref[...], preferred_element_type=F32) + \
        jnp.dot(gla_ref[...], wg_ref[...], preferred_element_type=F32)
    y = mixed * lax.rsqrt(jnp.mean(mixed * mixed, axis=-1, keepdims=True) + NORM_EPS)
    out_ref[...] = x_ref[...] + y * nw_ref[...]


def _out_project(att2, gla2, w_out, x2, post_norm_w):
    n, d = x2.shape
    tm = min(PROJ_TM, n)
    wa, wg = w_out[:ATT_W].astype(BF16), w_out[ATT_W:].astype(BF16)
    row = lambda width: pl.BlockSpec((tm, width), lambda i: (i, 0))
    full = lambda a: pl.BlockSpec(a.shape, lambda i: (0, 0))
    return pl.pallas_call(
        _out_kernel,
        grid=(n // tm,),
        in_specs=[row(ATT_W), row(GLA_W), full(wa), full(wg), row(d), full(post_norm_w)],
        out_specs=row(d),
        out_shape=jax.ShapeDtypeStruct((n, d), F32),
        compiler_params=pltpu.CompilerParams(dimension_semantics=("arbitrary",),
                                             vmem_limit_bytes=VMEM_LIMIT),
        name="out_proj_postnorm",
    )(att2, gla2, wa, wg, x2, post_norm_w)


def kernel(x, positions, w_in, w_gla_gate_up, b_gla_gate, gla_norm_w, w_out, pre_norm_w, post_norm_w):
    b, s, d = x.shape
    n = b * s
    cos8, sin8 = _rope_tables(positions)
    aqt, ak, avt, iqt, ik, iwt, ag, gqk, gv, glr, gg = _project(x, pre_norm_w.reshape(1, d), cos8, sin8, w_in)
    att = _dsa_attention(aqt, ak, avt, iqt, ik, iwt, ag)
    gla = _gla(gqk, gv, glr, gg, w_gla_gate_up, b_gla_gate, gla_norm_w)
    out = _out_project(att.reshape(n, ATT_W), gla.reshape(n, GLA_W), w_out, x.reshape(n, d),
                       post_norm_w.reshape(1, d))
    return out.reshape(b, s, d)
```

```python
import functools

import jax
import jax.numpy as jnp
import numpy as np
from jax import lax
from jax.experimental import pallas as pl
from jax.experimental.pallas import tpu as pltpu

F32 = jnp.float32
BF16 = jnp.bfloat16
I32 = jnp.int32
HIGHEST = lax.Precision.HIGHEST

ATT_HEADS = 8
ATT_HD = 64
ATT_W = ATT_HEADS * ATT_HD
IDX_HEADS = 4
IDX_HD = 64
IDX_QW = IDX_HEADS * IDX_HD
TOPK_MAX = 256
GLA_HEADS = 4
GLA_DK = 64
GLA_DV = 128
GLA_QKW = GLA_HEADS * GLA_DK
GLA_W = GLA_HEADS * GLA_DV
GLA_GATE_RANK = 16
GLA_TAU = 16.0
GLA_CHUNK = 64
GLA_SUB = 16
ROPE_THETA = 500000.0
ROT_DIM = ATT_HD // 4
ROT_HALF = ROT_DIM // 2
NORM_EPS = 1e-6
NEG = float(np.float32(-1e30))
LOG2E = 1.4426950408889634

SUBLANES = 8
LANES = 128
MXU_DIM = 256
VMEM_LIMIT = 56 * 1024 * 1024

PROJ_TM = 512
DSA_TQ = 256
DSA_TK = 256
ATT_SQ = 128
INT_MIN = -(2 ** 31)


def _nt_dot(a, b):
    return lax.dot_general(a, b, (((1,), (1,)), ((), ())), preferred_element_type=F32)


def _tn_dot(a, b, precision=None):
    return lax.dot_general(a, b, (((0,), (0,)), ((), ())), preferred_element_type=F32,
                           precision=precision)


def _rope_table_kernel(pos_ref, inv_ref, cos_ref, sin_ref):
    ang = pos_ref[...].astype(F32) * inv_ref[...]
    cos_ref[...] = jnp.cos(ang)
    sin_ref[...] = jnp.sin(ang)


def _rope_tables(positions):
    n = positions.size
    inv = ROPE_THETA ** (-jnp.arange(0, ROT_DIM, 2, dtype=F32) / ROT_DIM)
    return pl.pallas_call(
        _rope_table_kernel,
        out_shape=(jax.ShapeDtypeStruct((ROT_HALF, n), F32),) * 2,
        name="rope_tables",
    )(positions.reshape(1, n), inv.reshape(ROT_HALF, 1))


def _proj_kernel(x_ref, pnw_ref, cos_ref, sin_ref,
                 wt_aq, wt_ak, wt_av, wt_iq, wt_ik, wt_iw, w_ag, w_gqk, w_gv, w_glr, w_gg,
                 aqt_o, ak_o, avt_o, iqt_o, ik_o, iwt_o, ag_o, gqk_o, gv_o, glr_o, gg_o):
    x = x_ref[0]
    ms = jnp.mean(x * x, axis=-1, keepdims=True)
    h = (x * lax.rsqrt(ms + NORM_EPS)) * pnw_ref[...]
    hb = h.astype(BF16)
    cos8, sin8 = cos_ref[...], sin_ref[...]

    def mm(w_ref):
        return jnp.dot(hb, w_ref[...], preferred_element_type=F32)

    def mm_t(wt_ref):
        return _nt_dot(wt_ref[...], hb)

    def rope_t(yt):
        pieces = []
        for hd in range(yt.shape[0] // ATT_HD):
            rows = yt[hd * ATT_HD:(hd + 1) * ATT_HD]
            x1, x2 = rows[:ROT_HALF], rows[ROT_HALF:ROT_DIM]
            pieces += [x1 * cos8 - x2 * sin8, x2 * cos8 + x1 * sin8, rows[ROT_DIM:]]
        return jnp.concatenate(pieces, axis=0)

    aqt_o[0] = (rope_t(mm_t(wt_aq)) * (ATT_HD ** -0.5 * LOG2E)).astype(BF16)
    ak_o[0] = rope_t(mm_t(wt_ak)).T.astype(BF16)
    avt = mm_t(wt_av).astype(BF16)
    for c in range(avt_o.shape[1]):
        avt_o[0, c] = avt[:, c * DSA_TK:(c + 1) * DSA_TK]
    iqt_o[0] = rope_t(mm_t(wt_iq)).astype(BF16)
    ikt = mm_t(wt_ik)
    ik_o[0] = jnp.concatenate([rope_t(ikt[:IDX_HD]), ikt[IDX_HD:]], axis=0).T.astype(BF16)
    iwt_o[0] = mm_t(wt_iw) * (IDX_HEADS ** -0.5 * IDX_HD ** -0.5)
    a_gate = mm(w_ag)
    ag_o[0] = a_gate * jax.nn.sigmoid(a_gate)
    gqk = mm(w_gqk)
    lane = lax.broadcasted_iota(I32, gqk.shape, 1)
    gqk_o[0] = jnp.where(lane < GLA_QKW, gqk * (GLA_DK ** -0.5), gqk)
    gv_o[0] = mm(w_gv)
    glr_o[0] = mm(w_glr)
    g_gate = mm(w_gg)
    gg_o[0] = g_gate * jax.nn.sigmoid(g_gate)


def _split_w_in(w_in):
    sizes = (ATT_W, ATT_W, ATT_W, IDX_QW, IDX_HD, IDX_HEADS, ATT_W,
             GLA_QKW, GLA_QKW, GLA_W, GLA_GATE_RANK, GLA_W)
    offs = np.cumsum((0,) + sizes)
    seg = [w_in[:, offs[i]:offs[i + 1]] for i in range(len(sizes))]
    a_q, a_k, a_v, i_q, i_k, i_w, a_g, g_q, g_k, g_v, g_lr, g_g = seg
    d = w_in.shape[0]
    pad = lambda w, n: jnp.concatenate([w, jnp.zeros((d, n - w.shape[1]), w.dtype)], axis=1)
    transposed = [a_q.T, a_k.T, a_v.T, i_q.T, pad(i_k, LANES).T, pad(i_w, SUBLANES).T]
    natural = [a_g, jnp.concatenate([g_q, g_k], axis=1), g_v, pad(g_lr, LANES), g_g]
    return [w.astype(BF16) for w in transposed], [w.astype(BF16) for w in natural]


def _project(x, pre_norm_w, cos8, sin8, w_in):
    b, s, d = x.shape
    tm = min(PROJ_TM, s)
    nt = s // tm
    cpt = tm // DSA_TK
    wts, wns = _split_w_in(w_in)
    full = lambda a: pl.BlockSpec(a.shape, lambda bi, i: (0,) * a.ndim)
    nat = lambda width: pl.BlockSpec((1, tm, width), lambda bi, i: (bi, i, 0))
    tr = lambda rows: pl.BlockSpec((1, rows, tm), lambda bi, i: (bi, 0, i))
    tab = pl.BlockSpec((ROT_HALF, tm), lambda bi, i: (0, bi * nt + i))
    out_specs = [tr(ATT_W), nat(ATT_W), pl.BlockSpec((1, cpt, ATT_W, DSA_TK), lambda bi, i: (bi, i, 0, 0)),
                 tr(IDX_QW), nat(LANES), tr(SUBLANES),
                 nat(ATT_W), nat(2 * GLA_QKW), nat(GLA_W), nat(LANES), nat(GLA_W)]
    sds = jax.ShapeDtypeStruct
    out_shape = [sds((b, ATT_W, s), BF16), sds((b, s, ATT_W), BF16), sds((b, s // DSA_TK, ATT_W, DSA_TK), BF16),
                 sds((b, IDX_QW, s), BF16), sds((b, s, LANES), BF16), sds((b, SUBLANES, s), F32),
                 sds((b, s, ATT_W), F32), sds((b, s, 2 * GLA_QKW), F32), sds((b, s, GLA_W), F32),
                 sds((b, s, LANES), F32), sds((b, s, GLA_W), F32)]
    return pl.pallas_call(
        _proj_kernel,
        grid=(b, nt),
        in_specs=[nat(d), full(pre_norm_w), tab, tab] + [full(w) for w in wts + wns],
        out_specs=out_specs,
        out_shape=out_shape,
        compiler_params=pltpu.CompilerParams(dimension_semantics=("arbitrary", "arbitrary"),
                                             vmem_limit_bytes=VMEM_LIMIT),
        name="prenorm_in_proj",
    )(x, pre_norm_w, cos8, sin8, *wts, *wns)


def _key_to_f32(key):
    return pltpu.bitcast(key ^ ((key >> 31) & 0x7FFFFFFF), F32)


def _dsa_kernel(aqt_ref, ak_ref, avt_ref, iqt_ref, ik_ref, iwt_ref, ag_ref, out_ref,
                score_scr, bias_scr, qbd_scr, m_scr, l_scr, acc_scr, *, ksel):
    tq, tk = DSA_TQ, DSA_TK
    i = pl.program_id(1)
    nck = (i * tq + tq + tk - 1) // tk
    groups = tk // SUBLANES

    key_iota = lax.broadcasted_iota(I32, (tk, tq), 0)
    q_pos = i * tq + lax.broadcasted_iota(I32, (tk, tq), 1)

    iqt = iqt_ref[0]
    q_cat = jnp.concatenate([iqt[h * IDX_HD:(h + 1) * IDX_HD] for h in range(IDX_HEADS)], axis=1)
    q_cat = jnp.concatenate([q_cat, jnp.zeros((LANES - IDX_HD, IDX_HEADS * tq), BF16)], axis=0)
    iwt = iwt_ref[0]
    w_rows = [iwt[h:h + 1, :] for h in range(IDX_HEADS)]

    def score_chunk(c, carry):
        kc = ik_ref[0, pl.ds(pl.multiple_of(c * tk, tk), tk), :]
        rel = jnp.dot(kc, q_cat, preferred_element_type=F32)
        sc = w_rows[0] * jnp.maximum(rel[:, 0:tq], 0.0)
        for h in range(1, IDX_HEADS):
            sc = sc + w_rows[h] * jnp.maximum(rel[:, h * tq:(h + 1) * tq], 0.0)
        score_scr[c] = jnp.where(c * tk + key_iota <= q_pos, sc, NEG)
        return carry

    lax.fori_loop(0, nck, score_chunk, 0)

    def count(pred):
        def body(c, acc):
            sc = score_scr[c].reshape(groups, SUBLANES, tq)
            return acc + jnp.sum(jnp.where(pred(sc), 1, 0), axis=0)
        acc = lax.fori_loop(0, nck, body, jnp.zeros((SUBLANES, tq), I32))
        return jnp.sum(acc, axis=0, keepdims=True)

    def bisect(it, prefix):
        cand = prefix + lax.shift_left(jnp.int32(1), 31 - it)
        cand_f = _key_to_f32(cand)
        cnt = count(lambda sc: sc >= cand_f[None])
        return jnp.where(cnt >= ksel, cand, prefix)

    vkey = lax.fori_loop(0, 32, bisect, jnp.full((SUBLANES, tq), INT_MIN, I32))
    vstar = _key_to_f32(vkey)
    need = ksel - count(lambda sc: sc > vstar[None])
    n_eq = count(lambda sc: sc == vstar[None])

    def bias_chunk(c, carry):
        bias_scr[c] = jnp.where(score_scr[c] > vstar[0:1], 0.0, NEG)
        return carry

    lax.fori_loop(0, nck, bias_chunk, 0)

    @pl.when(jnp.max(n_eq - need) > 0)
    def _():
        tri = jnp.where(lax.broadcasted_iota(I32, (tk, tk), 1) <= lax.broadcasted_iota(I32, (tk, tk), 0),
                        1.0, 0.0).astype(BF16)
        need_f = need.astype(F32)

        def tie_chunk(c, run):
            eq = score_scr[c] == vstar[0:1]
            rank = jnp.dot(tri, jnp.where(eq, 1.0, 0.0).astype(BF16), preferred_element_type=F32) + run
            take = eq & (rank <= need_f) & (c * tk + key_iota <= q_pos)
            bias_scr[c] = jnp.where(take, 0.0, bias_scr[c])
            return rank[tk - 1:tk, :]

        lax.fori_loop(0, nck, tie_chunk, jnp.zeros((1, tq), F32))

    heads_per_half = MXU_DIM // ATT_HD
    n_half = ATT_W // MXU_DIM
    row_head = lax.broadcasted_iota(I32, (MXU_DIM, tq), 0) // ATT_HD
    for h in range(ATT_HEADS):
        half, hh = divmod(h, heads_per_half)
        qh = aqt_ref[0, half * MXU_DIM:(half + 1) * MXU_DIM, :]
        qbd_scr[half, :, hh * tq:(hh + 1) * tq] = jnp.where(row_head == hh, qh, jnp.zeros_like(qh))
    m_scr[...] = jnp.full_like(m_scr, NEG)
    l_scr[...] = jnp.zeros_like(l_scr)
    acc_scr[...] = jnp.zeros_like(acc_scr)

    def att_chunk(c, carry):
        rows = pl.ds(pl.multiple_of(c * tk, tk), tk)
        for half in range(n_half):
            kc = ak_ref[0, rows, half * MXU_DIM:(half + 1) * MXU_DIM]
            s_all = jnp.dot(kc, qbd_scr[half], preferred_element_type=F32)
            for hh in range(heads_per_half):
                h = half * heads_per_half + hh
                for qs in range(tq // ATT_SQ):
                    ql = slice(qs * ATT_SQ, (qs + 1) * ATT_SQ)
                    s = s_all[:, hh * tq + qs * ATT_SQ:hh * tq + (qs + 1) * ATT_SQ] + bias_scr[c, :, ql]
                    m_old = m_scr[h, :, ql]
                    m_new = jnp.maximum(m_old, jnp.max(s, axis=0, keepdims=True))
                    alpha = jnp.exp2(m_old - m_new)
                    p = jnp.exp2(s - m_new)
                    l_scr[h, :, ql] = alpha * l_scr[h, :, ql] + jnp.sum(p, axis=0, keepdims=True)
                    vt = avt_ref[0, c, h * ATT_HD:(h + 1) * ATT_HD, :]
                    acc_scr[h, :, ql] = alpha * acc_scr[h, :, ql] + \
                        jnp.dot(vt, p.astype(BF16), preferred_element_type=F32)
                    m_scr[h, :, ql] = m_new
        return carry

    lax.fori_loop(0, nck, att_chunk, 0)

    att_t = jnp.concatenate([acc_scr[h] * (1.0 / l_scr[h]) for h in range(ATT_HEADS)], axis=0)
    out_ref[0] = (att_t.T * ag_ref[0]).astype(BF16)


def _dsa_attention(aqt, ak, avt, iqt, ik, iwt, ag):
    b, s, _ = ak.shape
    tq, tk = DSA_TQ, DSA_TK
    nck = s // tk
    ksel = min(TOPK_MAX, s // 4)
    qt = lambda rows: pl.BlockSpec((1, rows, tq), lambda bi, i: (bi, 0, i))
    seq = lambda width: pl.BlockSpec((1, s, width), lambda bi, i: (bi, 0, 0))
    qn = pl.BlockSpec((1, tq, ATT_W), lambda bi, i: (bi, i, 0))
    return pl.pallas_call(
        functools.partial(_dsa_kernel, ksel=ksel),
        grid=(b, s // tq),
        in_specs=[qt(ATT_W), seq(ATT_W), pl.BlockSpec((1, nck, ATT_W, tk), lambda bi, i: (bi, 0, 0, 0)),
                  qt(IDX_QW), seq(LANES), qt(SUBLANES), qn],
        out_specs=qn,
        out_shape=jax.ShapeDtypeStruct((b, s, ATT_W), BF16),
        scratch_shapes=[pltpu.VMEM((nck, tk, tq), F32), pltpu.VMEM((nck, tk, tq), F32),
                        pltpu.VMEM((ATT_W // MXU_DIM, MXU_DIM, (MXU_DIM // ATT_HD) * tq), BF16),
                        pltpu.VMEM((ATT_HEADS, 1, tq), F32), pltpu.VMEM((ATT_HEADS, 1, tq), F32),
                        pltpu.VMEM((ATT_HEADS, ATT_HD, tq), F32)],
        compiler_params=pltpu.CompilerParams(dimension_semantics=("arbitrary", "arbitrary"),
                                             vmem_limit_bytes=VMEM_LIMIT),
        name="dsa_attention",
    )(aqt, ak, avt, iqt, ik, iwt, ag)


def _gla_kernel(gqk_ref, gv_ref, glr_ref, gg_ref, wup_ref, bup_ref, nw_ref, out_ref, state_scr):
    c_len, sub = GLA_CHUNK, GLA_SUB
    nb = gqk_ref.shape[0]
    n_sub = c_len // sub

    @pl.when(pl.program_id(0) == 0)
    def _():
        state_scr[...] = jnp.zeros_like(state_scr)

    t_i = lax.broadcasted_iota(I32, (c_len, GLA_QKW), 0)
    s_i = lax.broadcasted_iota(I32, (sub, GLA_QKW), 0)
    tril = jnp.where(lax.broadcasted_iota(I32, (c_len, c_len), 0)
                     >= lax.broadcasted_iota(I32, (c_len, c_len), 1), 1.0, 0.0)
    ones_cl = jnp.ones((c_len, LANES), F32)
    lane_head = lax.broadcasted_iota(I32, (c_len, GLA_QKW), 1) // GLA_DK

    def stack_heads(a):
        return jnp.concatenate([jnp.where(lane_head == h, a, 0.0) for h in range(GLA_HEADS)],
                               axis=0).astype(BF16)

    zr = lax.broadcasted_iota(I32, (n_sub * sub * sub, GLA_QKW), 0)
    zc = lax.broadcasted_iota(I32, (n_sub * sub * sub, GLA_QKW), 1)
    pick = ((zc % c_len) == (zr // (sub * sub)) * sub + zr % sub)
    hr = lax.broadcasted_iota(I32, (GLA_QKW, GLA_QKW), 0) // GLA_DK
    hc = lax.broadcasted_iota(I32, (GLA_QKW, GLA_QKW), 1) // c_len
    head_rep = jnp.where(hr == hc, 1.0, 0.0).astype(BF16)
    sr = lax.broadcasted_iota(I32, (c_len, n_sub * sub * sub), 0)
    sc_ = lax.broadcasted_iota(I32, (c_len, n_sub * sub * sub), 1)
    row_sel = jnp.where(sr == sc_ // sub, 1.0, 0.0).astype(BF16)
    vr = lax.broadcasted_iota(I32, (GLA_HEADS * c_len, GLA_W), 0) // c_len
    vc = lax.broadcasted_iota(I32, (GLA_HEADS * c_len, GLA_W), 1) // GLA_DV
    v_diag = vr == vc
    a2_keep = (lax.broadcasted_iota(I32, (GLA_HEADS * c_len, c_len), 0) % c_len) // (2 * sub) == \
        lax.broadcasted_iota(I32, (GLA_HEADS * c_len, c_len), 1) // (2 * sub)

    for bi in range(nb):
        gqk = gqk_ref[bi]
        q, k = gqk[:, :GLA_QKW], gqk[:, GLA_QKW:]
        v = gv_ref[bi]
        vb = v.astype(BF16)
        z = jnp.dot(glr_ref[bi], wup_ref[...], preferred_element_type=F32, precision=HIGHEST) + bup_ref[...]
        g = (jnp.minimum(z, 0.0) - jnp.log1p(jnp.exp(-jnp.abs(z)))) * (1.0 / GLA_TAU)
        b = jnp.dot(tril, g, preferred_element_type=F32, precision=HIGHEST)
        b_last_col = _tn_dot(g, ones_cl, precision=HIGHEST)
        b_last = b[c_len - 1:c_len, :]

        state = state_scr[bi]
        o_stack = jnp.dot(stack_heads(q * jnp.exp(b)), state.astype(BF16), preferred_element_type=F32)

        ref1 = b[2 * sub - 1:2 * sub, :]
        q1 = q * jnp.exp(jnp.where(t_i >= 2 * sub, b - ref1, NEG))
        k1 = k * jnp.exp(jnp.where(t_i < 2 * sub, ref1 - b, NEG))
        ref2 = jnp.where(t_i < 2 * sub, b[sub - 1:sub, :], b[3 * sub - 1:3 * sub, :])
        odd = (t_i // sub) % 2 == 1
        q2 = q * jnp.exp(jnp.where(odd, b - ref2, NEG))
        k2 = k * jnp.exp(jnp.where(odd, NEG, ref2 - b))
        a_off = _nt_dot(stack_heads(q1), k1.astype(BF16)) + \
            jnp.where(a2_keep, _nt_dot(stack_heads(q2), k2.astype(BF16)), 0.0)
        a_off = a_off.astype(BF16)

        zs = []
        for t in range(c_len):
            blk = t // sub
            rs = slice(blk * sub, (blk + 1) * sub)
            dec = jnp.exp(jnp.where(s_i <= t - blk * sub, b[t:t + 1, :] - b[rs], NEG))
            zs.append(q[t:t + 1, :] * k[rs] * dec)
        zmat = jnp.concatenate(zs, axis=0).astype(BF16)
        rep = jnp.dot(zmat, head_rep, preferred_element_type=F32)
        a_diag = jnp.dot(row_sel, jnp.where(pick, rep, 0.0).astype(BF16),
                         preferred_element_type=F32).astype(BF16)
        v_bd = jnp.where(v_diag, jnp.concatenate([vb] * GLA_HEADS, axis=0), jnp.zeros((), BF16))
        o = jnp.dot(a_diag, v_bd, preferred_element_type=F32)

        o_heads = []
        for h in range(GLA_HEADS):
            rs = slice(h * c_len, (h + 1) * c_len)
            o_h = o_stack[rs] + jnp.dot(a_off[rs], vb[:, h * GLA_DV:(h + 1) * GLA_DV],
                                        preferred_element_type=F32)
            o_heads.append(o_h)
        o = o + jnp.concatenate(o_heads, axis=1)

        k_dec = (k * jnp.exp(b_last - b)).astype(BF16)
        upd = _tn_dot(k_dec, vb)
        upd = jnp.concatenate([upd[h * GLA_DK:(h + 1) * GLA_DK, h * GLA_DV:(h + 1) * GLA_DV]
                               for h in range(GLA_HEADS)], axis=0)
        state_scr[bi] = jnp.exp(b_last_col) * state + upd

        outs = []
        for h in range(GLA_HEADS):
            oh = o[:, h * GLA_DV:(h + 1) * GLA_DV]
            oh = oh * lax.rsqrt(jnp.mean(oh * oh, axis=-1, keepdims=True) + NORM_EPS)
            outs.append(oh * nw_ref[...])
        out_ref[bi] = (jnp.concatenate(outs, axis=1) * gg_ref[bi]).astype(BF16)


def _gla(gqk, gv, glr, gg, w_up, b_up, norm_w):
    b, s, _ = gqk.shape
    c_len = GLA_CHUNK
    wup_pad = jnp.concatenate([w_up, jnp.zeros((LANES - GLA_GATE_RANK, GLA_QKW), F32)], axis=0)
    blk = lambda width: pl.BlockSpec((b, c_len, width), lambda c: (0, c, 0))
    full = lambda a: pl.BlockSpec(a.shape, lambda c: (0,) * a.ndim)
    b_up2, nw2 = b_up.reshape(1, GLA_QKW), norm_w.reshape(1, GLA_DV)
    return pl.pallas_call(
        _gla_kernel,
        grid=(s // c_len,),
        in_specs=[blk(2 * GLA_QKW), blk(GLA_W), blk(LANES), blk(GLA_W), full(wup_pad), full(b_up2), full(nw2)],
        out_specs=blk(GLA_W),
        out_shape=jax.ShapeDtypeStruct((b, s, GLA_W), BF16),
        scratch_shapes=[pltpu.VMEM((b, GLA_QKW, GLA_DV), F32)],
        compiler_params=pltpu.CompilerParams(dimension_semantics=("arbitrary",),
                                             vmem_limit_bytes=VMEM_LIMIT),
        name="gla_chunked",
    )(gqk, gv, glr, gg, wup_pad, b_up2, nw2)


def _out_kernel(att_ref, gla_ref, wa_ref, wg_ref, x_ref, nw_ref, out_ref):
    mixed = jnp.dot(att_ref[...], wa_ref[...], preferred_element_type=F32) + \
        jnp.dot(gla_ref[...], wg_ref[...], preferred_element_type=F32)
    y = mixed * lax.rsqrt(jnp.mean(mixed * mixed, axis=-1, keepdims=True) + NORM_EPS)
    out_ref[...] = x_ref[...] + y * nw_ref[...]


def _out_project(att2, gla2, w_out, x2, post_norm_w):
    n, d = x2.shape
    tm = min(PROJ_TM, n)
    wa, wg = w_out[:ATT_W].astype(BF16), w_out[ATT_W:].astype(BF16)
    row = lambda width: pl.BlockSpec((tm, width), lambda i: (i, 0))
    full = lambda a: pl.BlockSpec(a.shape, lambda i: (0, 0))
    return pl.pallas_call(
        _out_kernel,
        grid=(n // tm,),
        in_specs=[row(ATT_W), row(GLA_W), full(wa), full(wg), row(d), full(post_norm_w)],
        out_specs=row(d),
        out_shape=jax.ShapeDtypeStruct((n, d), F32),
        compiler_params=pltpu.CompilerParams(dimension_semantics=("arbitrary",),
                                             vmem_limit_bytes=VMEM_LIMIT),
        name="out_proj_postnorm",
    )(att2, gla2, wa, wg, x2, post_norm_w)


def kernel(x, positions, w_in, w_gla_gate_up, b_gla_gate, gla_norm_w, w_out, pre_norm_w, post_norm_w):
    b, s, d = x.shape
    n = b * s
    cos8, sin8 = _rope_tables(positions)
    aqt, ak, avt, iqt, ik, iwt, ag, gqk, gv, glr, gg = _project(x, pre_norm_w.reshape(1, d), cos8, sin8, w_in)
    att = _dsa_attention(aqt, ak, avt, iqt, ik, iwt, ag)
    gla = _gla(gqk, gv, glr, gg, w_gla_gate_up, b_gla_gate, gla_norm_w)
    out = _out_project(att.reshape(n, ATT_W), gla.reshape(n, GLA_W), w_out, x.reshape(n, d),
                       post_norm_w.reshape(1, d))
    return out.reshape(b, s, d)
```

```python
import functools

import jax
import jax.numpy as jnp
import numpy as np
from jax import lax
from jax.experimental import pallas as pl
from jax.experimental.pallas import tpu as pltpu

F32 = jnp.float32
BF16 = jnp.bfloat16
I32 = jnp.int32
HIGHEST = lax.Precision.HIGHEST

ATT_HEADS = 8
ATT_HD = 64
ATT_W = ATT_HEADS * ATT_HD
IDX_HEADS = 4
IDX_HD = 64
IDX_QW = IDX_HEADS * IDX_HD
TOPK_MAX = 256
GLA_HEADS = 4
GLA_DK = 64
GLA_DV = 128
GLA_QKW = GLA_HEADS * GLA_DK
GLA_W = GLA_HEADS * GLA_DV
GLA_GATE_RANK = 16
GLA_TAU = 16.0
GLA_CHUNK = 64
GLA_SUB = 16
ROPE_THETA = 500000.0
ROT_DIM = ATT_HD // 4
ROT_HALF = ROT_DIM // 2
NORM_EPS = 1e-6
NEG = float(np.float32(-1e30))
LOG2E = 1.4426950408889634

SUBLANES = 8
LANES = 128
MXU_DIM = 256
VMEM_LIMIT = 56 * 1024 * 1024

PROJ_TM = 512
DSA_TQ = 256
DSA_TK = 256
ATT_SQ = 256
COUNT_CHAINS = 4
ATT_VROWS = ATT_HD + 16
INT_MIN = -(2 ** 31)


def _nt_dot(a, b):
    return lax.dot_general(a, b, (((1,), (1,)), ((), ())), preferred_element_type=F32)


def _tn_dot(a, b, precision=None):
    return lax.dot_general(a, b, (((0,), (0,)), ((), ())), preferred_element_type=F32,
                           precision=precision)


def _rope_table_kernel(pos_ref, inv_ref, cos_ref, sin_ref):
    ang = pos_ref[...].astype(F32) * inv_ref[...]
    cos_ref[...] = jnp.cos(ang)
    sin_ref[...] = jnp.sin(ang)


def _rope_tables(positions):
    n = positions.size
    inv = ROPE_THETA ** (-jnp.arange(0, ROT_DIM, 2, dtype=F32) / ROT_DIM)
    return pl.pallas_call(
        _rope_table_kernel,
        out_shape=(jax.ShapeDtypeStruct((ROT_HALF, n), F32),) * 2,
        name="rope_tables",
    )(positions.reshape(1, n), inv.reshape(ROT_HALF, 1))


def _proj_kernel(x_ref, pnw_ref, cos_ref, sin_ref,
                 wt_aq, wt_ak, wt_av, wt_iq, wt_ik, wt_iw, w_ag, w_gqk, w_gv, w_glr, w_gg,
                 aqt_o, ak_o, avt_o, iqt_o, ik_o, iwt_o, ag_o, gqk_o, gv_o, glr_o, gg_o):
    x = x_ref[0]
    ms = jnp.mean(x * x, axis=-1, keepdims=True)
    h = (x * lax.rsqrt(ms + NORM_EPS)) * pnw_ref[...]
    hb = h.astype(BF16)
    cos8, sin8 = cos_ref[...], sin_ref[...]

    def mm(w_ref):
        return jnp.dot(hb, w_ref[...], preferred_element_type=F32)

    def mm_t(wt_ref):
        return _nt_dot(wt_ref[...], hb)

    def rope_t(yt):
        pieces = []
        for hd in range(yt.shape[0] // ATT_HD):
            rows = yt[hd * ATT_HD:(hd + 1) * ATT_HD]
            x1, x2 = rows[:ROT_HALF], rows[ROT_HALF:ROT_DIM]
            pieces += [x1 * cos8 - x2 * sin8, x2 * cos8 + x1 * sin8, rows[ROT_DIM:]]
        return jnp.concatenate(pieces, axis=0)

    aqt_o[0] = (rope_t(mm_t(wt_aq)) * (ATT_HD ** -0.5 * LOG2E)).astype(BF16)
    ak_o[0] = rope_t(mm_t(wt_ak)).T.astype(BF16)
    avt = mm_t(wt_av)
    ones_rows = jnp.ones((ATT_VROWS - ATT_HD, avt.shape[1]), F32)
    avt = jnp.concatenate([piece for hd in range(ATT_HEADS)
                           for piece in (avt[hd * ATT_HD:(hd + 1) * ATT_HD], ones_rows)], axis=0).astype(BF16)
    for c in range(avt_o.shape[1]):
        avt_o[0, c] = avt[:, c * DSA_TK:(c + 1) * DSA_TK]
    iqt_o[0] = rope_t(mm_t(wt_iq)).astype(BF16)
    ikt = mm_t(wt_ik)
    ik_o[0] = jnp.concatenate([rope_t(ikt[:IDX_HD]), ikt[IDX_HD:]], axis=0).T.astype(BF16)
    iwt_o[0] = mm_t(wt_iw) * (IDX_HEADS ** -0.5 * IDX_HD ** -0.5)
    a_gate = mm(w_ag)
    ag_o[0] = a_gate * jax.nn.sigmoid(a_gate)
    gqk = mm(w_gqk)
    lane = lax.broadcasted_iota(I32, gqk.shape, 1)
    gqk_o[0] = jnp.where(lane < GLA_QKW, gqk * (GLA_DK ** -0.5), gqk)
    gv_o[0] = mm(w_gv)
    glr_o[0] = mm(w_glr)
    g_gate = mm(w_gg)
    gg_o[0] = g_gate * jax.nn.sigmoid(g_gate)


def _split_w_in(w_in):
    sizes = (ATT_W, ATT_W, ATT_W, IDX_QW, IDX_HD, IDX_HEADS, ATT_W,
             GLA_QKW, GLA_QKW, GLA_W, GLA_GATE_RANK, GLA_W)
    offs = np.cumsum((0,) + sizes)
    seg = [w_in[:, offs[i]:offs[i + 1]] for i in range(len(sizes))]
    a_q, a_k, a_v, i_q, i_k, i_w, a_g, g_q, g_k, g_v, g_lr, g_g = seg
    d = w_in.shape[0]
    pad = lambda w, n: jnp.concatenate([w, jnp.zeros((d, n - w.shape[1]), w.dtype)], axis=1)
    transposed = [a_q.T, a_k.T, a_v.T, i_q.T, pad(i_k, LANES).T, pad(i_w, SUBLANES).T]
    natural = [a_g, jnp.concatenate([g_q, g_k], axis=1), g_v, pad(g_lr, LANES), g_g]
    return [w.astype(BF16) for w in transposed], [w.astype(BF16) for w in natural]


def _project(x, pre_norm_w, cos8, sin8, w_in):
    b, s, d = x.shape
    tm = min(PROJ_TM, s)
    nt = s // tm
    cpt = tm // DSA_TK
    wts, wns = _split_w_in(w_in)
    full = lambda a: pl.BlockSpec(a.shape, lambda bi, i: (0,) * a.ndim)
    nat = lambda width: pl.BlockSpec((1, tm, width), lambda bi, i: (bi, i, 0))
    tr = lambda rows: pl.BlockSpec((1, rows, tm), lambda bi, i: (bi, 0, i))
    tab = pl.BlockSpec((ROT_HALF, tm), lambda bi, i: (0, bi * nt + i))
    vrows = ATT_HEADS * ATT_VROWS
    out_specs = [tr(ATT_W), nat(ATT_W), pl.BlockSpec((1, cpt, vrows, DSA_TK), lambda bi, i: (bi, i, 0, 0)),
                 tr(IDX_QW), nat(LANES), tr(SUBLANES),
                 nat(ATT_W), nat(2 * GLA_QKW), nat(GLA_W), nat(LANES), nat(GLA_W)]
    sds = jax.ShapeDtypeStruct
    out_shape = [sds((b, ATT_W, s), BF16), sds((b, s, ATT_W), BF16), sds((b, s // DSA_TK, vrows, DSA_TK), BF16),
                 sds((b, IDX_QW, s), BF16), sds((b, s, LANES), BF16), sds((b, SUBLANES, s), F32),
                 sds((b, s, ATT_W), F32), sds((b, s, 2 * GLA_QKW), F32), sds((b, s, GLA_W), F32),
                 sds((b, s, LANES), F32), sds((b, s, GLA_W), F32)]
    return pl.pallas_call(
        _proj_kernel,
        grid=(b, nt),
        in_specs=[nat(d), full(pre_norm_w), tab, tab] + [full(w) for w in wts + wns],
        out_specs=out_specs,
        out_shape=out_shape,
        compiler_params=pltpu.CompilerParams(dimension_semantics=("arbitrary", "arbitrary"),
                                             vmem_limit_bytes=VMEM_LIMIT),
        name="prenorm_in_proj",
    )(x, pre_norm_w, cos8, sin8, *wts, *wns)


def _key_to_f32(key):
    return pltpu.bitcast(key ^ ((key >> 31) & 0x7FFFFFFF), F32)


def _dsa_kernel(aqt_ref, ak_ref, avt_ref, iqt_ref, ik_ref, iwt_ref, ag_ref, out_ref,
                score_scr, bias_scr, qbd_scr, m_scr, acc_scr, *, ksel):
    tq, tk = DSA_TQ, DSA_TK
    i = pl.program_id(1)
    nck = (i * tq + tq + tk - 1) // tk
    groups = tk // SUBLANES

    key_iota = lax.broadcasted_iota(I32, (tk, tq), 0)
    q_pos = i * tq + lax.broadcasted_iota(I32, (tk, tq), 1)

    iqt = iqt_ref[0]
    q_cat = jnp.concatenate([iqt[h * IDX_HD:(h + 1) * IDX_HD] for h in range(IDX_HEADS)], axis=1)
    q_cat = jnp.concatenate([q_cat, jnp.zeros((LANES - IDX_HD, IDX_HEADS * tq), BF16)], axis=0)
    iwt = iwt_ref[0]
    w_rows = [iwt[h:h + 1, :] for h in range(IDX_HEADS)]

    def score_chunk(c, carry):
        kc = ik_ref[0, pl.ds(pl.multiple_of(c * tk, tk), tk), :]
        rel = jnp.dot(kc, q_cat, preferred_element_type=F32)
        sc = w_rows[0] * jnp.maximum(rel[:, 0:tq], 0.0)
        for h in range(1, IDX_HEADS):
            sc = sc + w_rows[h] * jnp.maximum(rel[:, h * tq:(h + 1) * tq], 0.0)
        score_scr[c] = jnp.where(c * tk + key_iota <= q_pos, sc, NEG)
        return carry

    lax.fori_loop(0, nck, score_chunk, 0)

    @pl.when(nck % 2 == 1)
    def _():
        score_scr[nck] = jnp.full((tk, tq), NEG, F32)

    def count(pred):
        def body(c2, acc):
            sc = score_scr[pl.ds(2 * c2, 2)].reshape(COUNT_CHAINS, 2 * groups // COUNT_CHAINS, SUBLANES, tq)
            return acc + jnp.sum(jnp.where(pred(sc), 1, 0), axis=1)
        acc = lax.fori_loop(0, (nck + 1) // 2, body, jnp.zeros((COUNT_CHAINS, SUBLANES, tq), I32))
        return jnp.sum(acc.reshape(COUNT_CHAINS * SUBLANES, tq), axis=0, keepdims=True)

    def bisect(it, carry):
        prefix, cnt_ge = carry
        cand = prefix + lax.shift_left(jnp.int32(1), 31 - it)
        cand_f = _key_to_f32(cand)
        cnt = count(lambda sc: sc >= cand_f[None])
        take = cnt >= ksel
        return jnp.where(take, cand, prefix), jnp.where(take, cnt, cnt_ge)

    vkey, cnt_ge = lax.fori_loop(0, 32, bisect, (jnp.full((SUBLANES, tq), INT_MIN, I32),
                                                 jnp.zeros((1, tq), I32)))
    vstar = _key_to_f32(vkey)
    cnt_gt = count(lambda sc: sc > vstar[None])
    need = ksel - cnt_gt
    n_eq = cnt_ge - cnt_gt

    def bias_chunk(c, carry):
        bias_scr[c] = jnp.where(score_scr[c] > vstar[0:1], 0.0, NEG)
        return carry

    lax.fori_loop(0, nck, bias_chunk, 0)

    @pl.when(jnp.max(n_eq - need) > 0)
    def _():
        tri = jnp.where(lax.broadcasted_iota(I32, (tk, tk), 1) <= lax.broadcasted_iota(I32, (tk, tk), 0),
                        1.0, 0.0).astype(BF16)
        need_f = need.astype(F32)

        def tie_chunk(c, run):
            eq = score_scr[c] == vstar[0:1]
            rank = jnp.dot(tri, jnp.where(eq, 1.0, 0.0).astype(BF16), preferred_element_type=F32) + run
            take = eq & (rank <= need_f) & (c * tk + key_iota <= q_pos)
            bias_scr[c] = jnp.where(take, 0.0, bias_scr[c])
            return rank[tk - 1:tk, :]

        lax.fori_loop(0, nck, tie_chunk, jnp.zeros((1, tq), F32))

    heads_per_half = MXU_DIM // ATT_HD
    n_half = ATT_W // MXU_DIM
    row_head = lax.broadcasted_iota(I32, (MXU_DIM, tq), 0) // ATT_HD
    for h in range(ATT_HEADS):
        half, hh = divmod(h, heads_per_half)
        qh = aqt_ref[0, half * MXU_DIM:(half + 1) * MXU_DIM, :]
        qbd_scr[half, :, hh * tq:(hh + 1) * tq] = jnp.where(row_head == hh, qh, jnp.zeros_like(qh))
    m_scr[...] = jnp.full_like(m_scr, NEG)
    acc_scr[...] = jnp.zeros_like(acc_scr)

    @pl.when(nck % 2 == 1)
    def _():
        bias_scr[nck] = jnp.full((tk, tq), NEG, F32)

    def att_pair(c2, carry):
        s_all = {}
        for half in range(n_half):
            for sub in range(2):
                rows = pl.ds(pl.multiple_of((2 * c2 + sub) * tk, tk), tk)
                s_all[sub, half] = jnp.dot(ak_ref[0, rows, half * MXU_DIM:(half + 1) * MXU_DIM], qbd_scr[half],
                                           preferred_element_type=F32)
        for half in range(n_half):
            for hh in range(heads_per_half):
                h = half * heads_per_half + hh
                for qs in range(tq // ATT_SQ):
                    ql = slice(qs * ATT_SQ, (qs + 1) * ATT_SQ)
                    sl = slice(hh * tq + qs * ATT_SQ, hh * tq + (qs + 1) * ATT_SQ)
                    s = [s_all[sub, half][:, sl] + bias_scr[2 * c2 + sub, :, ql] for sub in range(2)]
                    m_old = m_scr[h, :, ql]
                    m_new = jnp.maximum(m_old, jnp.max(jnp.maximum(s[0], s[1]), axis=0, keepdims=True))
                    alpha = jnp.exp2(m_old - m_new)
                    pv = alpha * acc_scr[h, :, ql]
                    for sub in range(2):
                        p = jnp.exp2(s[sub] - m_new).astype(BF16)
                        vt = avt_ref[0, 2 * c2 + sub, h * ATT_VROWS:(h + 1) * ATT_VROWS, :]
                        pv = pv + jnp.dot(vt, p, preferred_element_type=F32)
                    acc_scr[h, :, ql] = pv
                    m_scr[h, :, ql] = m_new
        return carry

    lax.fori_loop(0, (nck + 1) // 2, att_pair, 0)

    att_t = jnp.concatenate([acc_scr[h, :ATT_HD] * (1.0 / acc_scr[h, ATT_HD:ATT_HD + 1])
                             for h in range(ATT_HEADS)], axis=0)
    out_ref[0] = (att_t.T * ag_ref[0]).astype(BF16)


def _dsa_attention(aqt, ak, avt, iqt, ik, iwt, ag):
    b, s, _ = ak.shape
    tq, tk = DSA_TQ, DSA_TK
    nck = s // tk
    ksel = min(TOPK_MAX, s // 4)
    qt = lambda rows: pl.BlockSpec((1, rows, tq), lambda bi, i: (bi, 0, i))
    seq = lambda width: pl.BlockSpec((1, s, width), lambda bi, i: (bi, 0, 0))
    qn = pl.BlockSpec((1, tq, ATT_W), lambda bi, i: (bi, i, 0))
    return pl.pallas_call(
        functools.partial(_dsa_kernel, ksel=ksel),
        grid=(b, s // tq),
        in_specs=[qt(ATT_W), seq(ATT_W),
                  pl.BlockSpec((1, nck, ATT_HEADS * ATT_VROWS, tk), lambda bi, i: (bi, 0, 0, 0)),
                  qt(IDX_QW), seq(LANES), qt(SUBLANES), qn],
        out_specs=qn,
        out_shape=jax.ShapeDtypeStruct((b, s, ATT_W), BF16),
        scratch_shapes=[pltpu.VMEM((nck, tk, tq), F32), pltpu.VMEM((nck, tk, tq), F32),
                        pltpu.VMEM((ATT_W // MXU_DIM, MXU_DIM, (MXU_DIM // ATT_HD) * tq), BF16),
                        pltpu.VMEM((ATT_HEADS, 1, tq), F32),
                        pltpu.VMEM((ATT_HEADS, ATT_VROWS, tq), F32)],
        compiler_params=pltpu.CompilerParams(dimension_semantics=("arbitrary", "arbitrary"),
                                             vmem_limit_bytes=VMEM_LIMIT),
        name="dsa_attention",
    )(aqt, ak, avt, iqt, ik, iwt, ag)


def _gla_kernel(gqk_ref, gv_ref, glr_ref, gg_ref, wup_ref, bup_ref, nw_ref, out_ref, state_scr):
    c_len, sub = GLA_CHUNK, GLA_SUB
    nb = gqk_ref.shape[0]
    n_sub = c_len // sub

    @pl.when(pl.program_id(0) == 0)
    def _():
        state_scr[...] = jnp.zeros_like(state_scr)

    t_i = lax.broadcasted_iota(I32, (c_len, GLA_QKW), 0)
    s_i = lax.broadcasted_iota(I32, (sub, GLA_QKW), 0)
    tril = jnp.where(lax.broadcasted_iota(I32, (c_len, c_len), 0)
                     >= lax.broadcasted_iota(I32, (c_len, c_len), 1), 1.0, 0.0)
    ones_cl = jnp.ones((c_len, LANES), F32)
    lane_head = lax.broadcasted_iota(I32, (c_len, GLA_QKW), 1) // GLA_DK

    def stack_heads(a):
        return jnp.concatenate([jnp.where(lane_head == h, a, 0.0) for h in range(GLA_HEADS)],
                               axis=0).astype(BF16)

    zr = lax.broadcasted_iota(I32, (n_sub * sub * sub, GLA_QKW), 0)
    zc = lax.broadcasted_iota(I32, (n_sub * sub * sub, GLA_QKW), 1)
    pick = ((zc % c_len) == (zr // (sub * sub)) * sub + zr % sub)
    hr = lax.broadcasted_iota(I32, (GLA_QKW, GLA_QKW), 0) // GLA_DK
    hc = lax.broadcasted_iota(I32, (GLA_QKW, GLA_QKW), 1) // c_len
    head_rep = jnp.where(hr == hc, 1.0, 0.0).astype(BF16)
    sr = lax.broadcasted_iota(I32, (c_len, n_sub * sub * sub), 0)
    sc_ = lax.broadcasted_iota(I32, (c_len, n_sub * sub * sub), 1)
    row_sel = jnp.where(sr == sc_ // sub, 1.0, 0.0).astype(BF16)
    vr = lax.broadcasted_iota(I32, (GLA_HEADS * c_len, GLA_W), 0) // c_len
    vc = lax.broadcasted_iota(I32, (GLA_HEADS * c_len, GLA_W), 1) // GLA_DV
    v_diag = vr == vc
    a2_keep = (lax.broadcasted_iota(I32, (GLA_HEADS * c_len, c_len), 0) % c_len) // (2 * sub) == \
        lax.broadcasted_iota(I32, (GLA_HEADS * c_len, c_len), 1) // (2 * sub)

    for bi in range(nb):
        gqk = gqk_ref[bi]
        q, k = gqk[:, :GLA_QKW], gqk[:, GLA_QKW:]
        v = gv_ref[bi]
        vb = v.astype(BF16)
        z = jnp.dot(glr_ref[bi], wup_ref[...], preferred_element_type=F32, precision=HIGHEST) + bup_ref[...]
        g = (jnp.minimum(z, 0.0) - jnp.log1p(jnp.exp(-jnp.abs(z)))) * (1.0 / GLA_TAU)
        b = jnp.dot(tril, g, preferred_element_type=F32, precision=HIGHEST)
        b_last_col = _tn_dot(g, ones_cl, precision=HIGHEST)
        b_last = b[c_len - 1:c_len, :]

        state = state_scr[bi]
        o_stack = jnp.dot(stack_heads(q * jnp.exp(b)), state.astype(BF16), preferred_element_type=F32)

        ref1 = b[2 * sub - 1:2 * sub, :]
        q1 = q * jnp.exp(jnp.where(t_i >= 2 * sub, b - ref1, NEG))
        k1 = k * jnp.exp(jnp.where(t_i < 2 * sub, ref1 - b, NEG))
        ref2 = jnp.where(t_i < 2 * sub, b[sub - 1:sub, :], b[3 * sub - 1:3 * sub, :])
        odd = (t_i // sub) % 2 == 1
        q2 = q * jnp.exp(jnp.where(odd, b - ref2, NEG))
        k2 = k * jnp.exp(jnp.where(odd, NEG, ref2 - b))
        a_off = _nt_dot(stack_heads(q1), k1.astype(BF16)) + \
            jnp.where(a2_keep, _nt_dot(stack_heads(q2), k2.astype(BF16)), 0.0)
        a_off = a_off.astype(BF16)

        zs = []
        for t in range(c_len):
            blk = t // sub
            rs = slice(blk * sub, (blk + 1) * sub)
            dec = jnp.exp(jnp.where(s_i <= t - blk * sub, b[t:t + 1, :] - b[rs], NEG))
            zs.append(q[t:t + 1, :] * k[rs] * dec)
        zmat = jnp.concatenate(zs, axis=0).astype(BF16)
        rep = jnp.dot(zmat, head_rep, preferred_element_type=F32)
        a_diag = jnp.dot(row_sel, jnp.where(pick, rep, 0.0).astype(BF16),
                         preferred_element_type=F32).astype(BF16)
        v_bd = jnp.where(v_diag, jnp.concatenate([vb] * GLA_HEADS, axis=0), jnp.zeros((), BF16))
        o = jnp.dot(a_diag, v_bd, preferred_element_type=F32)

        o_heads = []
        for h in range(GLA_HEADS):
            rs = slice(h * c_len, (h + 1) * c_len)
            o_h = o_stack[rs] + jnp.dot(a_off[rs], vb[:, h * GLA_DV:(h + 1) * GLA_DV],
                                        preferred_element_type=F32)
            o_heads.append(o_h)
        o = o + jnp.concatenate(o_heads, axis=1)

        k_dec = (k * jnp.exp(b_last - b)).astype(BF16)
        upd = _tn_dot(k_dec, vb)
        upd = jnp.concatenate([upd[h * GLA_DK:(h + 1) * GLA_DK, h * GLA_DV:(h + 1) * GLA_DV]
                               for h in range(GLA_HEADS)], axis=0)
        state_scr[bi] = jnp.exp(b_last_col) * state + upd

        outs = []
        for h in range(GLA_HEADS):
            oh = o[:, h * GLA_DV:(h + 1) * GLA_DV]
            oh = oh * lax.rsqrt(jnp.mean(oh * oh, axis=-1, keepdims=True) + NORM_EPS)
            outs.append(oh * nw_ref[...])
        out_ref[bi] = (jnp.concatenate(outs, axis=1) * gg_ref[bi]).astype(BF16)


def _gla(gqk, gv, glr, gg, w_up, b_up, norm_w):
    b, s, _ = gqk.shape
    c_len = GLA_CHUNK
    wup_pad = jnp.concatenate([w_up, jnp.zeros((LANES - GLA_GATE_RANK, GLA_QKW), F32)], axis=0)
    blk = lambda width: pl.BlockSpec((b, c_len, width), lambda c: (0, c, 0))
    full = lambda a: pl.BlockSpec(a.shape, lambda c: (0,) * a.ndim)
    b_up2, nw2 = b_up.reshape(1, GLA_QKW), norm_w.reshape(1, GLA_DV)
    return pl.pallas_call(
        _gla_kernel,
        grid=(s // c_len,),
        in_specs=[blk(2 * GLA_QKW), blk(GLA_W), blk(LANES), blk(GLA_W), full(wup_pad), full(b_up2), full(nw2)],
        out_specs=blk(GLA_W),
        out_shape=jax.ShapeDtypeStruct((b, s, GLA_W), BF16),
        scratch_shapes=[pltpu.VMEM((b, GLA_QKW, GLA_DV), F32)],
        compiler_params=pltpu.CompilerParams(dimension_semantics=("arbitrary",),
                                             vmem_limit_bytes=VMEM_LIMIT),
        name="gla_chunked",
    )(gqk, gv, glr, gg, wup_pad, b_up2, nw2)


def _out_kernel(att_ref, gla_ref, wa_ref, wg_ref, x_ref, nw_ref, out_ref):
    mixed = jnp.dot(att_ref[...], wa_ref[...], preferred_element_type=F32) + \
        jnp.dot(gla_ref[...], wg_ref[...], preferred_element_type=F32)
    y = mixed * lax.rsqrt(jnp.mean(mixed * mixed, axis=-1, keepdims=True) + NORM_EPS)
    out_ref[...] = x_ref[...] + y * nw_ref[...]


def _out_project(att2, gla2, w_out, x2, post_norm_w):
    n, d = x2.shape
    tm = min(PROJ_TM, n)
    wa, wg = w_out[:ATT_W].astype(BF16), w_out[ATT_W:].astype(BF16)
    row = lambda width: pl.BlockSpec((tm, width), lambda i: (i, 0))
    full = lambda a: pl.BlockSpec(a.shape, lambda i: (0, 0))
    return pl.pallas_call(
        _out_kernel,
        grid=(n // tm,),
        in_specs=[row(ATT_W), row(GLA_W), full(wa), full(wg), row(d), full(post_norm_w)],
        out_specs=row(d),
        out_shape=jax.ShapeDtypeStruct((n, d), F32),
        compiler_params=pltpu.CompilerParams(dimension_semantics=("arbitrary",),
                                             vmem_limit_bytes=VMEM_LIMIT),
        name="out_proj_postnorm",
    )(att2, gla2, wa, wg, x2, post_norm_w)


def kernel(x, positions, w_in, w_gla_gate_up, b_gla_gate, gla_norm_w, w_out, pre_norm_w, post_norm_w):
    b, s, d = x.shape
    n = b * s
    cos8, sin8 = _rope_tables(positions)
    aqt, ak, avt, iqt, ik, iwt, ag, gqk, gv, glr, gg = _project(x, pre_norm_w.reshape(1, d), cos8, sin8, w_in)
    att = _dsa_attention(aqt, ak, avt, iqt, ik, iwt, ag)
    gla = _gla(gqk, gv, glr, gg, w_gla_gate_up, b_gla_gate, gla_norm_w)
    out = _out_project(att.reshape(n, ATT_W), gla.reshape(n, GLA_W), w_out, x.reshape(n, d),
                       post_norm_w.reshape(1, d))
    return out.reshape(b, s, d)
```

```python
import functools

import jax
import jax.numpy as jnp
import numpy as np
from jax import lax
from jax.experimental import pallas as pl
from jax.experimental.pallas import tpu as pltpu

F32 = jnp.float32
BF16 = jnp.bfloat16
I32 = jnp.int32
HIGHEST = lax.Precision.HIGHEST

ATT_HEADS = 8
ATT_HD = 64
ATT_W = ATT_HEADS * ATT_HD
IDX_HEADS = 4
IDX_HD = 64
IDX_QW = IDX_HEADS * IDX_HD
TOPK_MAX = 256
GLA_HEADS = 4
GLA_DK = 64
GLA_DV = 128
GLA_QKW = GLA_HEADS * GLA_DK
GLA_W = GLA_HEADS * GLA_DV
GLA_GATE_RANK = 16
GLA_TAU = 16.0
GLA_CHUNK = 64
GLA_SUB = 8
ROPE_THETA = 500000.0
ROT_DIM = ATT_HD // 4
ROT_HALF = ROT_DIM // 2
NORM_EPS = 1e-6
NEG = float(np.float32(-1e30))
LOG2E = 1.4426950408889634

SUBLANES = 8
LANES = 128
MXU_DIM = 256
VMEM_LIMIT = 56 * 1024 * 1024

PROJ_TM = 512
DSA_TQ = 256
DSA_TK = 256
ATT_SQ = 256
COUNT_CHAINS = 4
ATT_VROWS = ATT_HD + 16
INT_MIN = -(2 ** 31)


def _nt_dot(a, b):
    return lax.dot_general(a, b, (((1,), (1,)), ((), ())), preferred_element_type=F32)


def _tn_dot(a, b, precision=None):
    return lax.dot_general(a, b, (((0,), (0,)), ((), ())), preferred_element_type=F32,
                           precision=precision)


def _rope_table_kernel(pos_ref, inv_ref, cos_ref, sin_ref):
    ang = pos_ref[...].astype(F32) * inv_ref[...]
    cos_ref[...] = jnp.cos(ang)
    sin_ref[...] = jnp.sin(ang)


def _rope_tables(positions):
    n = positions.size
    inv = ROPE_THETA ** (-jnp.arange(0, ROT_DIM, 2, dtype=F32) / ROT_DIM)
    return pl.pallas_call(
        _rope_table_kernel,
        out_shape=(jax.ShapeDtypeStruct((ROT_HALF, n), F32),) * 2,
        name="rope_tables",
    )(positions.reshape(1, n), inv.reshape(ROT_HALF, 1))


def _proj_kernel(x_ref, pnw_ref, cos_ref, sin_ref,
                 wt_aq, wt_ak, wt_av, wt_iq, wt_ik, wt_iw, w_ag, w_gqk, w_gv, w_glr, w_gg,
                 wup_hi, wup_lo, bup_ref,
                 aqt_o, ak_o, avt_o, iqt_o, ik_o, iwt_o, ag_o, gqk_o, gv_o, glog_o, gg_o):
    x = x_ref[0]
    ms = jnp.mean(x * x, axis=-1, keepdims=True)
    h = (x * lax.rsqrt(ms + NORM_EPS)) * pnw_ref[...]
    hb = h.astype(BF16)
    cos8, sin8 = cos_ref[...], sin_ref[...]

    def mm(w_ref):
        return jnp.dot(hb, w_ref[...], preferred_element_type=F32)

    def mm_t(wt_ref):
        return _nt_dot(wt_ref[...], hb)

    def rope_t(yt):
        pieces = []
        for hd in range(yt.shape[0] // ATT_HD):
            rows = yt[hd * ATT_HD:(hd + 1) * ATT_HD]
            x1, x2 = rows[:ROT_HALF], rows[ROT_HALF:ROT_DIM]
            pieces += [x1 * cos8 - x2 * sin8, x2 * cos8 + x1 * sin8, rows[ROT_DIM:]]
        return jnp.concatenate(pieces, axis=0)

    aqt_o[0] = (rope_t(mm_t(wt_aq)) * (ATT_HD ** -0.5 * LOG2E)).astype(BF16)
    ak_o[0] = rope_t(mm_t(wt_ak)).T.astype(BF16)
    avt = mm_t(wt_av)
    ones_rows = jnp.ones((ATT_VROWS - ATT_HD, avt.shape[1]), F32)
    avt = jnp.concatenate([piece for hd in range(ATT_HEADS)
                           for piece in (avt[hd * ATT_HD:(hd + 1) * ATT_HD], ones_rows)], axis=0).astype(BF16)
    for c in range(avt_o.shape[1]):
        avt_o[0, c] = avt[:, c * DSA_TK:(c + 1) * DSA_TK]
    iqt_o[0] = rope_t(mm_t(wt_iq)).astype(BF16)
    ikt = mm_t(wt_ik)
    ik_o[0] = jnp.concatenate([rope_t(ikt[:IDX_HD]), ikt[IDX_HD:]], axis=0).T.astype(BF16)
    iwt_o[0] = mm_t(wt_iw) * (IDX_HEADS ** -0.5 * IDX_HD ** -0.5)
    a_gate = mm(w_ag)
    ag_o[0] = a_gate * jax.nn.sigmoid(a_gate)
    gqk = mm(w_gqk)
    lane = lax.broadcasted_iota(I32, gqk.shape, 1)
    gqk_o[0] = jnp.where(lane < GLA_QKW, gqk * (GLA_DK ** -0.5), gqk)
    gv_o[0] = mm(w_gv)
    glr = mm(w_glr)
    glr_hi = glr.astype(BF16)
    glr_lo = (glr - glr_hi.astype(F32)).astype(BF16)
    z = (jnp.dot(glr_hi, wup_hi[...], preferred_element_type=F32)
         + jnp.dot(glr_hi, wup_lo[...], preferred_element_type=F32)
         + jnp.dot(glr_lo, wup_hi[...], preferred_element_type=F32)) + bup_ref[...]
    glog_o[0] = (jnp.minimum(z, 0.0) - jnp.log1p(jnp.exp(-jnp.abs(z)))) * (1.0 / GLA_TAU)
    g_gate = mm(w_gg)
    gg_o[0] = g_gate * jax.nn.sigmoid(g_gate)


def _split_w_in(w_in):
    sizes = (ATT_W, ATT_W, ATT_W, IDX_QW, IDX_HD, IDX_HEADS, ATT_W,
             GLA_QKW, GLA_QKW, GLA_W, GLA_GATE_RANK, GLA_W)
    offs = np.cumsum((0,) + sizes)
    seg = [w_in[:, offs[i]:offs[i + 1]] for i in range(len(sizes))]
    a_q, a_k, a_v, i_q, i_k, i_w, a_g, g_q, g_k, g_v, g_lr, g_g = seg
    d = w_in.shape[0]
    pad = lambda w, n: jnp.concatenate([w, jnp.zeros((d, n - w.shape[1]), w.dtype)], axis=1)
    transposed = [a_q.T, a_k.T, a_v.T, i_q.T, pad(i_k, LANES).T, pad(i_w, SUBLANES).T]
    natural = [a_g, jnp.concatenate([g_q, g_k], axis=1), g_v, pad(g_lr, LANES), g_g]
    return [w.astype(BF16) for w in transposed], [w.astype(BF16) for w in natural]


def _project(x, pre_norm_w, cos8, sin8, w_in, w_up, b_up):
    b, s, d = x.shape
    wup_pad = jnp.concatenate([w_up, jnp.zeros((LANES - GLA_GATE_RANK, GLA_QKW), F32)], axis=0)
    wup_hi = wup_pad.astype(BF16)
    wup_lo = (wup_pad - wup_hi.astype(F32)).astype(BF16)
    gate_ops = [wup_hi, wup_lo, b_up.reshape(1, GLA_QKW)]
    tm = min(PROJ_TM, s)
    nt = s // tm
    cpt = tm // DSA_TK
    wts, wns = _split_w_in(w_in)
    full = lambda a: pl.BlockSpec(a.shape, lambda bi, i: (0,) * a.ndim)
    nat = lambda width: pl.BlockSpec((1, tm, width), lambda bi, i: (bi, i, 0))
    tr = lambda rows: pl.BlockSpec((1, rows, tm), lambda bi, i: (bi, 0, i))
    tab = pl.BlockSpec((ROT_HALF, tm), lambda bi, i: (0, bi * nt + i))
    vrows = ATT_HEADS * ATT_VROWS
    out_specs = [tr(ATT_W), nat(ATT_W), pl.BlockSpec((1, cpt, vrows, DSA_TK), lambda bi, i: (bi, i, 0, 0)),
                 tr(IDX_QW), nat(LANES), tr(SUBLANES),
                 nat(ATT_W), nat(2 * GLA_QKW), nat(GLA_W), nat(GLA_QKW), nat(GLA_W)]
    sds = jax.ShapeDtypeStruct
    out_shape = [sds((b, ATT_W, s), BF16), sds((b, s, ATT_W), BF16), sds((b, s // DSA_TK, vrows, DSA_TK), BF16),
                 sds((b, IDX_QW, s), BF16), sds((b, s, LANES), BF16), sds((b, SUBLANES, s), F32),
                 sds((b, s, ATT_W), F32), sds((b, s, 2 * GLA_QKW), F32), sds((b, s, GLA_W), F32),
                 sds((b, s, GLA_QKW), F32), sds((b, s, GLA_W), F32)]
    return pl.pallas_call(
        _proj_kernel,
        grid=(b, nt),
        in_specs=[nat(d), full(pre_norm_w), tab, tab] + [full(w) for w in wts + wns + gate_ops],
        out_specs=out_specs,
        out_shape=out_shape,
        compiler_params=pltpu.CompilerParams(dimension_semantics=("arbitrary", "arbitrary"),
                                             vmem_limit_bytes=VMEM_LIMIT),
        name="prenorm_in_proj",
    )(x, pre_norm_w, cos8, sin8, *wts, *wns, *gate_ops)


def _key_to_f32(key):
    return pltpu.bitcast(key ^ ((key >> 31) & 0x7FFFFFFF), F32)


def _dsa_kernel(aqt_ref, ak_ref, avt_ref, iqt_ref, ik_ref, iwt_ref, ag_ref, out_ref,
                score_scr, bias_scr, qbd_scr, m_scr, acc_scr, *, ksel):
    tq, tk = DSA_TQ, DSA_TK
    i = pl.program_id(1)
    nck = (i * tq + tq + tk - 1) // tk
    groups = tk // SUBLANES

    key_iota = lax.broadcasted_iota(I32, (tk, tq), 0)
    q_pos = i * tq + lax.broadcasted_iota(I32, (tk, tq), 1)

    iqt = iqt_ref[0]
    q_cat = jnp.concatenate([iqt[h * IDX_HD:(h + 1) * IDX_HD] for h in range(IDX_HEADS)], axis=1)
    q_cat = jnp.concatenate([q_cat, jnp.zeros((LANES - IDX_HD, IDX_HEADS * tq), BF16)], axis=0)
    iwt = iwt_ref[0]
    w_rows = [iwt[h:h + 1, :] for h in range(IDX_HEADS)]

    def score_chunk(c, carry):
        kc = ik_ref[0, pl.ds(pl.multiple_of(c * tk, tk), tk), :]
        rel = jnp.dot(kc, q_cat, preferred_element_type=F32)
        sc = w_rows[0] * jnp.maximum(rel[:, 0:tq], 0.0)
        for h in range(1, IDX_HEADS):
            sc = sc + w_rows[h] * jnp.maximum(rel[:, h * tq:(h + 1) * tq], 0.0)
        score_scr[c] = jnp.where(c * tk + key_iota <= q_pos, sc, NEG)
        return carry

    lax.fori_loop(0, nck, score_chunk, 0)

    @pl.when(nck % 2 == 1)
    def _():
        score_scr[nck] = jnp.full((tk, tq), NEG, F32)

    def count(pred):
        def body(c2, acc):
            sc = score_scr[pl.ds(2 * c2, 2)].reshape(COUNT_CHAINS, 2 * groups // COUNT_CHAINS, SUBLANES, tq)
            return acc + jnp.sum(jnp.where(pred(sc), 1, 0), axis=1)
        acc = lax.fori_loop(0, (nck + 1) // 2, body, jnp.zeros((COUNT_CHAINS, SUBLANES, tq), I32))
        return jnp.sum(acc.reshape(COUNT_CHAINS * SUBLANES, tq), axis=0, keepdims=True)

    def bisect(it, carry):
        prefix, cnt_ge = carry
        cand = prefix + lax.shift_left(jnp.int32(1), 31 - it)
        cand_f = _key_to_f32(cand)
        cnt = count(lambda sc: sc >= cand_f[None])
        take = cnt >= ksel
        return jnp.where(take, cand, prefix), jnp.where(take, cnt, cnt_ge)

    vkey, cnt_ge = lax.fori_loop(0, 32, bisect, (jnp.full((SUBLANES, tq), INT_MIN, I32),
                                                 jnp.zeros((1, tq), I32)))
    vstar = _key_to_f32(vkey)
    cnt_gt = count(lambda sc: sc > vstar[None])
    need = ksel - cnt_gt
    n_eq = cnt_ge - cnt_gt

    def bias_chunk(c, carry):
        bias_scr[c] = jnp.where(score_scr[c] > vstar[0:1], 0.0, NEG)
        return carry

    lax.fori_loop(0, nck, bias_chunk, 0)

    @pl.when(jnp.max(n_eq - need) > 0)
    def _():
        tri = jnp.where(lax.broadcasted_iota(I32, (tk, tk), 1) <= lax.broadcasted_iota(I32, (tk, tk), 0),
                        1.0, 0.0).astype(BF16)
        need_f = need.astype(F32)

        def tie_chunk(c, run):
            eq = score_scr[c] == vstar[0:1]
            rank = jnp.dot(tri, jnp.where(eq, 1.0, 0.0).astype(BF16), preferred_element_type=F32) + run
            take = eq & (rank <= need_f) & (c * tk + key_iota <= q_pos)
            bias_scr[c] = jnp.where(take, 0.0, bias_scr[c])
            return rank[tk - 1:tk, :]

        lax.fori_loop(0, nck, tie_chunk, jnp.zeros((1, tq), F32))

    heads_per_half = MXU_DIM // ATT_HD
    n_half = ATT_W // MXU_DIM
    row_head = lax.broadcasted_iota(I32, (MXU_DIM, tq), 0) // ATT_HD
    for h in range(ATT_HEADS):
        half, hh = divmod(h, heads_per_half)
        qh = aqt_ref[0, half * MXU_DIM:(half + 1) * MXU_DIM, :]
        qbd_scr[half, :, hh * tq:(hh + 1) * tq] = jnp.where(row_head == hh, qh, jnp.zeros_like(qh))
    m_scr[...] = jnp.full_like(m_scr, NEG)
    acc_scr[...] = jnp.zeros_like(acc_scr)

    @pl.when(nck % 2 == 1)
    def _():
        bias_scr[nck] = jnp.full((tk, tq), NEG, F32)

    def att_pair(c2, carry):
        s_all = {}
        for half in range(n_half):
            for sub in range(2):
                rows = pl.ds(pl.multiple_of((2 * c2 + sub) * tk, tk), tk)
                s_all[sub, half] = jnp.dot(ak_ref[0, rows, half * MXU_DIM:(half + 1) * MXU_DIM], qbd_scr[half],
                                           preferred_element_type=F32)
        for half in range(n_half):
            for hh in range(heads_per_half):
                h = half * heads_per_half + hh
                for qs in range(tq // ATT_SQ):
                    ql = slice(qs * ATT_SQ, (qs + 1) * ATT_SQ)
                    sl = slice(hh * tq + qs * ATT_SQ, hh * tq + (qs + 1) * ATT_SQ)
                    s = [s_all[sub, half][:, sl] + bias_scr[2 * c2 + sub, :, ql] for sub in range(2)]
                    m_old = m_scr[h, :, ql]
                    m_new = jnp.maximum(m_old, jnp.max(jnp.maximum(s[0], s[1]), axis=0, keepdims=True))
                    alpha = jnp.exp2(m_old - m_new)
                    pv = alpha * acc_scr[h, :, ql]
                    for sub in range(2):
                        p = jnp.exp2(s[sub] - m_new).astype(BF16)
                        vt = avt_ref[0, 2 * c2 + sub, h * ATT_VROWS:(h + 1) * ATT_VROWS, :]
                        pv = pv + jnp.dot(vt, p, preferred_element_type=F32)
                    acc_scr[h, :, ql] = pv
                    m_scr[h, :, ql] = m_new
        return carry

    lax.fori_loop(0, (nck + 1) // 2, att_pair, 0)

    att_t = jnp.concatenate([acc_scr[h, :ATT_HD] * (1.0 / acc_scr[h, ATT_HD:ATT_HD + 1])
                             for h in range(ATT_HEADS)], axis=0)
    out_ref[0] = (att_t.T * ag_ref[0]).astype(BF16)


def _dsa_attention(aqt, ak, avt, iqt, ik, iwt, ag):
    b, s, _ = ak.shape
    tq, tk = DSA_TQ, DSA_TK
    nck = s // tk
    ksel = min(TOPK_MAX, s // 4)
    qt = lambda rows: pl.BlockSpec((1, rows, tq), lambda bi, i: (bi, 0, i))
    seq = lambda width: pl.BlockSpec((1, s, width), lambda bi, i: (bi, 0, 0))
    qn = pl.BlockSpec((1, tq, ATT_W), lambda bi, i: (bi, i, 0))
    return pl.pallas_call(
        functools.partial(_dsa_kernel, ksel=ksel),
        grid=(b, s // tq),
        in_specs=[qt(ATT_W), seq(ATT_W),
                  pl.BlockSpec((1, nck, ATT_HEADS * ATT_VROWS, tk), lambda bi, i: (bi, 0, 0, 0)),
                  qt(IDX_QW), seq(LANES), qt(SUBLANES), qn],
        out_specs=qn,
        out_shape=jax.ShapeDtypeStruct((b, s, ATT_W), BF16),
        scratch_shapes=[pltpu.VMEM((nck, tk, tq), F32), pltpu.VMEM((nck, tk, tq), F32),
                        pltpu.VMEM((ATT_W // MXU_DIM, MXU_DIM, (MXU_DIM // ATT_HD) * tq), BF16),
                        pltpu.VMEM((ATT_HEADS, 1, tq), F32),
                        pltpu.VMEM((ATT_HEADS, ATT_VROWS, tq), F32)],
        compiler_params=pltpu.CompilerParams(dimension_semantics=("arbitrary", "arbitrary"),
                                             vmem_limit_bytes=VMEM_LIMIT),
        name="dsa_attention",
    )(aqt, ak, avt, iqt, ik, iwt, ag)


def _gla_kernel(gqk_ref, gv_ref, glog_ref, gg_ref, nw_ref, out_ref, state_scr):
    c_len, sub = GLA_CHUNK, GLA_SUB
    nb = gqk_ref.shape[0]
    n_z = c_len * sub

    @pl.when(pl.program_id(0) == 0)
    def _():
        state_scr[...] = jnp.zeros_like(state_scr)

    t_i = lax.broadcasted_iota(I32, (c_len, GLA_QKW), 0)
    s_i = lax.broadcasted_iota(I32, (sub, GLA_QKW), 0)
    j_i = lax.broadcasted_iota(I32, (c_len, GLA_QKW), 1) % c_len
    tril = jnp.where(lax.broadcasted_iota(I32, (c_len, c_len), 0)
                     >= lax.broadcasted_iota(I32, (c_len, c_len), 1), 1.0, 0.0).astype(BF16)
    lane_head = lax.broadcasted_iota(I32, (c_len, GLA_QKW), 1) // GLA_DK

    def stack_heads(a):
        return jnp.concatenate([jnp.where(lane_head == h, a, 0.0) for h in range(GLA_HEADS)],
                               axis=0).astype(BF16)

    zr = lax.broadcasted_iota(I32, (n_z, GLA_QKW), 0)
    zc = lax.broadcasted_iota(I32, (n_z, GLA_QKW), 1)
    pick = (zc % c_len) == (zr // (sub * sub)) * sub + zr % sub
    hr = lax.broadcasted_iota(I32, (GLA_QKW, GLA_QKW), 0) // GLA_DK
    hc = lax.broadcasted_iota(I32, (GLA_QKW, GLA_QKW), 1) // c_len
    head_rep = jnp.where(hr == hc, 1.0, 0.0).astype(BF16)
    sr = lax.broadcasted_iota(I32, (c_len, n_z), 0)
    sc_ = lax.broadcasted_iota(I32, (c_len, n_z), 1)
    row_sel = jnp.where(sr == sc_ // sub, 1.0, 0.0).astype(BF16)
    vr = lax.broadcasted_iota(I32, (GLA_HEADS * c_len, GLA_W), 0) // c_len
    vc = lax.broadcasted_iota(I32, (GLA_HEADS * c_len, GLA_W), 1) // GLA_DV
    v_diag = vr == vc
    spans = [c_len >> (lv + 1) for lv in range((c_len // sub).bit_length() - 1)]

    rows = range(nb)
    q, k, vb, b, b_last, state_t, o, a, rep = ({} for _ in range(9))

    for bi in rows:
        gqk = gqk_ref[bi]
        q[bi], k[bi] = gqk[:, :GLA_QKW], gqk[:, GLA_QKW:]
        vb[bi] = gv_ref[bi].astype(BF16)
        g = glog_ref[bi]
        g1 = g.astype(BF16)
        r1 = g - g1.astype(F32)
        g2 = r1.astype(BF16)
        g3 = (r1 - g2.astype(F32)).astype(BF16)
        b[bi] = (jnp.dot(tril, g1, preferred_element_type=F32) + jnp.dot(tril, g2, preferred_element_type=F32)
                 + jnp.dot(tril, g3, preferred_element_type=F32))
        b_last[bi] = b[bi][c_len - 1:c_len, :]

    for bi in rows:
        state_t[bi] = state_scr[bi]
        o_stack = _nt_dot(stack_heads(q[bi] * jnp.exp(b[bi])), state_t[bi].astype(BF16))
        o[bi] = jnp.concatenate([o_stack[h * c_len:(h + 1) * c_len] for h in range(GLA_HEADS)], axis=1)

        k_bd = stack_heads(k[bi] * jnp.exp(b_last[bi] - b[bi]))
        v_st = jnp.concatenate([vb[bi][:, h * GLA_DV:(h + 1) * GLA_DV] for h in range(GLA_HEADS)], axis=0)
        state_scr[bi] = jnp.exp(b_last[bi]) * state_t[bi] + _tn_dot(v_st, k_bd)

        for span in spans:
            blk = 2 * span
            ref = jnp.concatenate([jnp.broadcast_to(b[bi][m * blk + span - 1:m * blk + span, :], (blk, GLA_QKW))
                                   for m in range(c_len // blk)], axis=0)
            upper = (t_i % blk) >= span
            q_l = q[bi] * jnp.exp(jnp.where(upper, b[bi] - ref, NEG))
            k_l = k[bi] * jnp.exp(jnp.where(upper, NEG, ref - b[bi]))
            a_l = _nt_dot(q_l.astype(BF16), stack_heads(k_l))
            if blk < c_len:
                a_l = jnp.where(t_i // blk == j_i // blk, a_l, 0.0)
            a[bi] = a_l if bi not in a else a[bi] + a_l

    for bi in rows:
        zs = []
        for t in range(c_len):
            blk0 = (t // sub) * sub
            rs = slice(blk0, blk0 + sub)
            dec = jnp.exp(jnp.where(s_i <= t - blk0, b[bi][t:t + 1, :] - b[bi][rs], NEG))
            zs.append(q[bi][t:t + 1, :] * k[bi][rs] * dec)
        zmat = jnp.concatenate(zs, axis=0).astype(BF16)
        rep[bi] = jnp.dot(zmat, head_rep, preferred_element_type=F32)

    for bi in rows:
        a_all = a[bi] + jnp.dot(row_sel, jnp.where(pick, rep[bi], 0.0).astype(BF16), preferred_element_type=F32)
        v_bd = jnp.where(v_diag, jnp.concatenate([vb[bi]] * GLA_HEADS, axis=0), jnp.zeros((), BF16))
        o_bi = o[bi] + jnp.dot(a_all.astype(BF16), v_bd, preferred_element_type=F32)
        outs = []
        for h in range(GLA_HEADS):
            oh = o_bi[:, h * GLA_DV:(h + 1) * GLA_DV]
            oh = oh * lax.rsqrt(jnp.mean(oh * oh, axis=-1, keepdims=True) + NORM_EPS)
            outs.append(oh * nw_ref[...])
        out_ref[bi] = (jnp.concatenate(outs, axis=1) * gg_ref[bi]).astype(BF16)


def _gla(gqk, gv, glog, gg, norm_w):
    b, s, _ = gqk.shape
    c_len = GLA_CHUNK
    blk = lambda width: pl.BlockSpec((b, c_len, width), lambda c: (0, c, 0))
    nw2 = norm_w.reshape(1, GLA_DV)
    return pl.pallas_call(
        _gla_kernel,
        grid=(s // c_len,),
        in_specs=[blk(2 * GLA_QKW), blk(GLA_W), blk(GLA_QKW), blk(GLA_W),
                  pl.BlockSpec(nw2.shape, lambda c: (0, 0))],
        out_specs=blk(GLA_W),
        out_shape=jax.ShapeDtypeStruct((b, s, GLA_W), BF16),
        scratch_shapes=[pltpu.VMEM((b, GLA_DV, GLA_QKW), F32)],
        compiler_params=pltpu.CompilerParams(dimension_semantics=("arbitrary",),
                                             vmem_limit_bytes=VMEM_LIMIT),
        name="gla_chunked",
    )(gqk, gv, glog, gg, nw2)


def _out_kernel(att_ref, gla_ref, wa_ref, wg_ref, x_ref, nw_ref, out_ref):
    mixed = jnp.dot(att_ref[...], wa_ref[...], preferred_element_type=F32) + \
        jnp.dot(gla_ref[...], wg_ref[...], preferred_element_type=F32)
    y = mixed * lax.rsqrt(jnp.mean(mixed * mixed, axis=-1, keepdims=True) + NORM_EPS)
    out_ref[...] = x_ref[...] + y * nw_ref[...]


def _out_project(att2, gla2, w_out, x2, post_norm_w):
    n, d = x2.shape
    tm = min(PROJ_TM, n)
    wa, wg = w_out[:ATT_W].astype(BF16), w_out[ATT_W:].astype(BF16)
    row = lambda width: pl.BlockSpec((tm, width), lambda i: (i, 0))
    full = lambda a: pl.BlockSpec(a.shape, lambda i: (0, 0))
    return pl.pallas_call(
        _out_kernel,
        grid=(n // tm,),
        in_specs=[row(ATT_W), row(GLA_W), full(wa), full(wg), row(d), full(post_norm_w)],
        out_specs=row(d),
        out_shape=jax.ShapeDtypeStruct((n, d), F32),
        compiler_params=pltpu.CompilerParams(dimension_semantics=("arbitrary",),
                                             vmem_limit_bytes=VMEM_LIMIT),
        name="out_proj_postnorm",
    )(att2, gla2, wa, wg, x2, post_norm_w)


def kernel(x, positions, w_in, w_gla_gate_up, b_gla_gate, gla_norm_w, w_out, pre_norm_w, post_norm_w):
    b, s, d = x.shape
    n = b * s
    cos8, sin8 = _rope_tables(positions)
    aqt, ak, avt, iqt, ik, iwt, ag, gqk, gv, glog, gg = _project(x, pre_norm_w.reshape(1, d), cos8, sin8, w_in,
                                                                  w_gla_gate_up, b_gla_gate)
    att = _dsa_attention(aqt, ak, avt, iqt, ik, iwt, ag)
    gla = _gla(gqk, gv, glog, gg, gla_norm_w)
    out = _out_project(att.reshape(n, ATT_W), gla.reshape(n, GLA_W), w_out, x.reshape(n, d),
                       post_norm_w.reshape(1, d))
    return out.reshape(b, s, d)
```

```python
import functools

import jax
import jax.numpy as jnp
import numpy as np
from jax import lax
from jax.experimental import pallas as pl
from jax.experimental.pallas import tpu as pltpu

F32 = jnp.float32
BF16 = jnp.bfloat16
I32 = jnp.int32
HIGHEST = lax.Precision.HIGHEST

ATT_HEADS = 8
ATT_HD = 64
ATT_W = ATT_HEADS * ATT_HD
IDX_HEADS = 4
IDX_HD = 64
IDX_QW = IDX_HEADS * IDX_HD
TOPK_MAX = 256
GLA_HEADS = 4
GLA_DK = 64
GLA_DV = 128
GLA_QKW = GLA_HEADS * GLA_DK
GLA_W = GLA_HEADS * GLA_DV
GLA_GATE_RANK = 16
GLA_TAU = 16.0
GLA_CHUNK = 64
GLA_SUB = 8
ROPE_THETA = 500000.0
ROT_DIM = ATT_HD // 4
ROT_HALF = ROT_DIM // 2
NORM_EPS = 1e-6
NEG = float(np.float32(-1e30))
LOG2E = 1.4426950408889634

SUBLANES = 8
LANES = 128
MXU_DIM = 256
VMEM_LIMIT = 56 * 1024 * 1024

PROJ_TM = 512
DSA_TQ = 256
DSA_TK = 256
ATT_SQ = 256
COUNT_CHAINS = 4
ATT_VROWS = ATT_HD + 16
INT_MIN = -(2 ** 31)


def _nt_dot(a, b):
    return lax.dot_general(a, b, (((1,), (1,)), ((), ())), preferred_element_type=F32)


def _tn_dot(a, b, precision=None):
    return lax.dot_general(a, b, (((0,), (0,)), ((), ())), preferred_element_type=F32,
                           precision=precision)


def _rope_table_kernel(pos_ref, inv_ref, cos_ref, sin_ref):
    ang = pos_ref[...].astype(F32) * inv_ref[...]
    cos_ref[...] = jnp.cos(ang)
    sin_ref[...] = jnp.sin(ang)


def _rope_tables(positions):
    n = positions.size
    inv = ROPE_THETA ** (-jnp.arange(0, ROT_DIM, 2, dtype=F32) / ROT_DIM)
    return pl.pallas_call(
        _rope_table_kernel,
        out_shape=(jax.ShapeDtypeStruct((ROT_HALF, n), F32),) * 2,
        name="rope_tables",
    )(positions.reshape(1, n), inv.reshape(ROT_HALF, 1))


def _proj_kernel(x_ref, pnw_ref, cos_ref, sin_ref,
                 wt_aq, wt_ak, wt_av, wt_iq, wt_ik, wt_iw, w_ag, w_gqk, w_gv, w_glr, w_gg,
                 wup_hi, wup_lo, bup_ref,
                 aqt_o, ak_o, avt_o, iqt_o, ik_o, iwt_o, ag_o, gqk_o, gv_o, glog_o, gg_o):
    x = x_ref[0]
    ms = jnp.mean(x * x, axis=-1, keepdims=True)
    h = (x * lax.rsqrt(ms + NORM_EPS)) * pnw_ref[...]
    hb = h.astype(BF16)
    cos8, sin8 = cos_ref[...], sin_ref[...]

    def mm(w_ref):
        return jnp.dot(hb, w_ref[...], preferred_element_type=F32)

    def mm_t(wt_ref):
        return _nt_dot(wt_ref[...], hb)

    def rope_t(yt):
        pieces = []
        for hd in range(yt.shape[0] // ATT_HD):
            rows = yt[hd * ATT_HD:(hd + 1) * ATT_HD]
            x1, x2 = rows[:ROT_HALF], rows[ROT_HALF:ROT_DIM]
            pieces += [x1 * cos8 - x2 * sin8, x2 * cos8 + x1 * sin8, rows[ROT_DIM:]]
        return jnp.concatenate(pieces, axis=0)

    aqt_o[0] = (rope_t(mm_t(wt_aq)) * (ATT_HD ** -0.5 * LOG2E)).astype(BF16)
    ak_o[0] = rope_t(mm_t(wt_ak)).T.astype(BF16)
    avt = mm_t(wt_av)
    ones_rows = jnp.ones((ATT_VROWS - ATT_HD, avt.shape[1]), F32)
    avt = jnp.concatenate([piece for hd in range(ATT_HEADS)
                           for piece in (avt[hd * ATT_HD:(hd + 1) * ATT_HD], ones_rows)], axis=0).astype(BF16)
    for c in range(avt_o.shape[1]):
        avt_o[0, c] = avt[:, c * DSA_TK:(c + 1) * DSA_TK]
    iqt_o[0] = rope_t(mm_t(wt_iq)).astype(BF16)
    ikt = mm_t(wt_ik)
    ik_o[0] = jnp.concatenate([rope_t(ikt[:IDX_HD]), ikt[IDX_HD:]], axis=0).T.astype(BF16)
    iwt_o[0] = mm_t(wt_iw) * (IDX_HEADS ** -0.5 * IDX_HD ** -0.5)
    a_gate = mm(w_ag)
    ag_o[0] = a_gate * jax.nn.sigmoid(a_gate)
    gqk = mm(w_gqk)
    lane = lax.broadcasted_iota(I32, gqk.shape, 1)
    gqk_o[0] = jnp.where(lane < GLA_QKW, gqk * (GLA_DK ** -0.5), gqk)
    gv_o[0] = mm(w_gv)
    glr = mm(w_glr)
    glr_hi = glr.astype(BF16)
    glr_lo = (glr - glr_hi.astype(F32)).astype(BF16)
    z = (jnp.dot(glr_hi, wup_hi[...], preferred_element_type=F32)
         + jnp.dot(glr_hi, wup_lo[...], preferred_element_type=F32)
         + jnp.dot(glr_lo, wup_hi[...], preferred_element_type=F32)) + bup_ref[...]
    glog_o[0] = (jnp.minimum(z, 0.0) - jnp.log1p(jnp.exp(-jnp.abs(z)))) * (1.0 / GLA_TAU)
    g_gate = mm(w_gg)
    gg_o[0] = g_gate * jax.nn.sigmoid(g_gate)


def _split_w_in(w_in):
    sizes = (ATT_W, ATT_W, ATT_W, IDX_QW, IDX_HD, IDX_HEADS, ATT_W,
             GLA_QKW, GLA_QKW, GLA_W, GLA_GATE_RANK, GLA_W)
    offs = np.cumsum((0,) + sizes)
    seg = [w_in[:, offs[i]:offs[i + 1]] for i in range(len(sizes))]
    a_q, a_k, a_v, i_q, i_k, i_w, a_g, g_q, g_k, g_v, g_lr, g_g = seg
    d = w_in.shape[0]
    pad = lambda w, n: jnp.concatenate([w, jnp.zeros((d, n - w.shape[1]), w.dtype)], axis=1)
    transposed = [a_q.T, a_k.T, a_v.T, i_q.T, pad(i_k, LANES).T, pad(i_w, SUBLANES).T]
    natural = [a_g, jnp.concatenate([g_q, g_k], axis=1), g_v, pad(g_lr, LANES), g_g]
    return [w.astype(BF16) for w in transposed], [w.astype(BF16) for w in natural]


def _project(x, pre_norm_w, cos8, sin8, w_in, w_up, b_up):
    b, s, d = x.shape
    wup_pad = jnp.concatenate([w_up, jnp.zeros((LANES - GLA_GATE_RANK, GLA_QKW), F32)], axis=0)
    wup_hi = wup_pad.astype(BF16)
    wup_lo = (wup_pad - wup_hi.astype(F32)).astype(BF16)
    gate_ops = [wup_hi, wup_lo, b_up.reshape(1, GLA_QKW)]
    tm = min(PROJ_TM, s)
    nt = s // tm
    cpt = tm // DSA_TK
    wts, wns = _split_w_in(w_in)
    full = lambda a: pl.BlockSpec(a.shape, lambda bi, i: (0,) * a.ndim)
    nat = lambda width: pl.BlockSpec((1, tm, width), lambda bi, i: (bi, i, 0))
    tr = lambda rows: pl.BlockSpec((1, rows, tm), lambda bi, i: (bi, 0, i))
    tab = pl.BlockSpec((ROT_HALF, tm), lambda bi, i: (0, bi * nt + i))
    vrows = ATT_HEADS * ATT_VROWS
    out_specs = [tr(ATT_W), nat(ATT_W), pl.BlockSpec((1, cpt, vrows, DSA_TK), lambda bi, i: (bi, i, 0, 0)),
                 tr(IDX_QW), nat(LANES), tr(SUBLANES),
                 nat(ATT_W), nat(2 * GLA_QKW), nat(GLA_W), nat(GLA_QKW), nat(GLA_W)]
    sds = jax.ShapeDtypeStruct
    out_shape = [sds((b, ATT_W, s), BF16), sds((b, s, ATT_W), BF16), sds((b, s // DSA_TK, vrows, DSA_TK), BF16),
                 sds((b, IDX_QW, s), BF16), sds((b, s, LANES), BF16), sds((b, SUBLANES, s), F32),
                 sds((b, s, ATT_W), F32), sds((b, s, 2 * GLA_QKW), F32), sds((b, s, GLA_W), F32),
                 sds((b, s, GLA_QKW), F32), sds((b, s, GLA_W), F32)]
    return pl.pallas_call(
        _proj_kernel,
        grid=(b, nt),
        in_specs=[nat(d), full(pre_norm_w), tab, tab] + [full(w) for w in wts + wns + gate_ops],
        out_specs=out_specs,
        out_shape=out_shape,
        compiler_params=pltpu.CompilerParams(dimension_semantics=("arbitrary", "arbitrary"),
                                             vmem_limit_bytes=VMEM_LIMIT),
        name="prenorm_in_proj",
    )(x, pre_norm_w, cos8, sin8, *wts, *wns, *gate_ops)


def _key_to_f32(key):
    return pltpu.bitcast(key ^ ((key >> 31) & 0x7FFFFFFF), F32)


def _dsa_kernel(aqt_ref, ak_ref, avt_ref, iqt_ref, ik_ref, iwt_ref, ag_ref, out_ref,
                score_scr, coarse_scr, bias_scr, qbd_scr, m_scr, acc_scr, *, ksel):
    tq, tk = DSA_TQ, DSA_TK
    i = pl.program_id(1)
    nck = (i * tq + tq + tk - 1) // tk
    groups = tk // SUBLANES

    key_iota = lax.broadcasted_iota(I32, (tk, tq), 0)
    q_pos = i * tq + lax.broadcasted_iota(I32, (tk, tq), 1)

    iqt = iqt_ref[0]
    q_cat = jnp.concatenate([iqt[h * IDX_HD:(h + 1) * IDX_HD] for h in range(IDX_HEADS)], axis=1)
    q_cat = jnp.concatenate([q_cat, jnp.zeros((LANES - IDX_HD, IDX_HEADS * tq), BF16)], axis=0)
    iwt = iwt_ref[0]
    w_rows = [iwt[h:h + 1, :] for h in range(IDX_HEADS)]

    def score_chunk(c, carry):
        kc = ik_ref[0, pl.ds(pl.multiple_of(c * tk, tk), tk), :]
        rel = jnp.dot(kc, q_cat, preferred_element_type=F32)
        sc = w_rows[0] * jnp.maximum(rel[:, 0:tq], 0.0)
        for h in range(1, IDX_HEADS):
            sc = sc + w_rows[h] * jnp.maximum(rel[:, h * tq:(h + 1) * tq], 0.0)
        sc = jnp.where(c * tk + key_iota <= q_pos, sc, NEG)
        score_scr[c] = sc
        coarse_scr[c] = sc.astype(BF16)
        return carry

    lax.fori_loop(0, nck, score_chunk, 0)

    @pl.when(nck % 2 == 1)
    def _():
        score_scr[nck] = jnp.full((tk, tq), NEG, F32)
        coarse_scr[nck] = jnp.full((tk, tq), NEG, BF16)

    def count(pred):
        def body(c2, acc):
            sc = score_scr[pl.ds(2 * c2, 2)].reshape(COUNT_CHAINS, 2 * groups // COUNT_CHAINS, SUBLANES, tq)
            return acc + jnp.sum(jnp.where(pred(sc), 1, 0), axis=1)
        acc = lax.fori_loop(0, (nck + 1) // 2, body, jnp.zeros((COUNT_CHAINS, SUBLANES, tq), I32))
        return jnp.sum(acc.reshape(COUNT_CHAINS * SUBLANES, tq), axis=0, keepdims=True)

    def count_coarse(cand_b):
        packed_rows = 2 * SUBLANES
        def body(c2, acc):
            hi = coarse_scr[pl.ds(2 * c2, 2)].reshape(COUNT_CHAINS, 2 * tk // (COUNT_CHAINS * packed_rows),
                                                      packed_rows, tq)
            hit = jnp.where(hi >= cand_b[None, None], jnp.ones((), BF16), jnp.zeros((), BF16))
            for g in range(hit.shape[1]):
                acc = acc + hit[:, g]
            return acc
        acc = lax.fori_loop(0, (nck + 1) // 2, body, jnp.zeros((COUNT_CHAINS, packed_rows, tq), BF16))
        return jnp.sum(acc.astype(F32).reshape(COUNT_CHAINS * packed_rows, tq), axis=0, keepdims=True)

    def bisect_coarse(it, prefix):
        cand = prefix + lax.shift_left(jnp.int32(1), 15 - it)
        pattern = cand ^ ((cand >> 31) & 0x7FFF)
        cand_b = pltpu.bitcast(lax.shift_left(pattern, 16), F32).astype(BF16)
        cnt = count_coarse(cand_b)
        return jnp.where(cnt >= ksel, cand, prefix)

    assert (tk // (COUNT_CHAINS * SUBLANES)) * (coarse_scr.shape[0] // 2) <= 2 ** 8, "bf16 counts must stay exact"
    key16 = lax.fori_loop(0, 16, bisect_coarse, jnp.full((2 * SUBLANES, tq), -(2 ** 15), I32))[:SUBLANES]
    key_p = lax.shift_left(key16, 16) | ((key16 >> 31) & 0xFFFF)
    key_lo = key_p - 2 ** 16

    def bisect(it, carry):
        prefix, cnt_ge = carry
        cand = prefix + lax.shift_left(jnp.int32(1), 16 - it)
        cand_f = _key_to_f32(cand)
        cnt = count(lambda sc: sc >= cand_f[None])
        take = cnt >= ksel
        return jnp.where(take, cand, prefix), jnp.where(take, cnt, cnt_ge)

    vkey, cnt_ge = lax.fori_loop(0, 17, bisect, (key_lo, jnp.zeros((1, tq), I32)))
    vstar = _key_to_f32(vkey)
    cnt_gt = count(lambda sc: sc > vstar[None])
    need = ksel - cnt_gt
    n_eq = cnt_ge - cnt_gt

    def bias_chunk(c, carry):
        bias_scr[c] = jnp.where(score_scr[c] > vstar[0:1], 0.0, NEG)
        return carry

    lax.fori_loop(0, nck, bias_chunk, 0)

    @pl.when(jnp.max(n_eq - need) > 0)
    def _():
        tri = jnp.where(lax.broadcasted_iota(I32, (tk, tk), 1) <= lax.broadcasted_iota(I32, (tk, tk), 0),
                        1.0, 0.0).astype(BF16)
        need_f = need.astype(F32)

        def tie_chunk(c, run):
            eq = score_scr[c] == vstar[0:1]
            rank = jnp.dot(tri, jnp.where(eq, 1.0, 0.0).astype(BF16), preferred_element_type=F32) + run
            take = eq & (rank <= need_f) & (c * tk + key_iota <= q_pos)
            bias_scr[c] = jnp.where(take, 0.0, bias_scr[c])
            return rank[tk - 1:tk, :]

        lax.fori_loop(0, nck, tie_chunk, jnp.zeros((1, tq), F32))

    heads_per_half = MXU_DIM // ATT_HD
    n_half = ATT_W // MXU_DIM
    row_head = lax.broadcasted_iota(I32, (MXU_DIM, tq), 0) // ATT_HD
    for h in range(ATT_HEADS):
        half, hh = divmod(h, heads_per_half)
        qh = aqt_ref[0, half * MXU_DIM:(half + 1) * MXU_DIM, :]
        qbd_scr[half, :, hh * tq:(hh + 1) * tq] = jnp.where(row_head == hh, qh, jnp.zeros_like(qh))
    m_scr[...] = jnp.full_like(m_scr, NEG)
    acc_scr[...] = jnp.zeros_like(acc_scr)

    @pl.when(nck % 2 == 1)
    def _():
        bias_scr[nck] = jnp.full((tk, tq), NEG, F32)

    def att_pair(c2, carry):
        s_all = {}
        for half in range(n_half):
            for sub in range(2):
                rows = pl.ds(pl.multiple_of((2 * c2 + sub) * tk, tk), tk)
                s_all[sub, half] = jnp.dot(ak_ref[0, rows, half * MXU_DIM:(half + 1) * MXU_DIM], qbd_scr[half],
                                           preferred_element_type=F32)
        for half in range(n_half):
            for hh in range(heads_per_half):
                h = half * heads_per_half + hh
                for qs in range(tq // ATT_SQ):
                    ql = slice(qs * ATT_SQ, (qs + 1) * ATT_SQ)
                    sl = slice(hh * tq + qs * ATT_SQ, hh * tq + (qs + 1) * ATT_SQ)
                    s = [s_all[sub, half][:, sl] + bias_scr[2 * c2 + sub, :, ql] for sub in range(2)]
                    m_old = m_scr[h, :, ql]
                    m_new = jnp.maximum(m_old, jnp.max(jnp.maximum(s[0], s[1]), axis=0, keepdims=True))
                    alpha = jnp.exp2(m_old - m_new)
                    pv = alpha * acc_scr[h, :, ql]
                    for sub in range(2):
                        p = jnp.exp2(s[sub] - m_new).astype(BF16)
                        vt = avt_ref[0, 2 * c2 + sub, h * ATT_VROWS:(h + 1) * ATT_VROWS, :]
                        pv = pv + jnp.dot(vt, p, preferred_element_type=F32)
                    acc_scr[h, :, ql] = pv
                    m_scr[h, :, ql] = m_new
        return carry

    lax.fori_loop(0, (nck + 1) // 2, att_pair, 0)

    att_t = jnp.concatenate([acc_scr[h, :ATT_HD] * (1.0 / acc_scr[h, ATT_HD:ATT_HD + 1])
                             for h in range(ATT_HEADS)], axis=0)
    out_ref[0] = (att_t.T * ag_ref[0]).astype(BF16)


def _dsa_attention(aqt, ak, avt, iqt, ik, iwt, ag):
    b, s, _ = ak.shape
    tq, tk = DSA_TQ, DSA_TK
    nck = s // tk
    ksel = min(TOPK_MAX, s // 4)
    qt = lambda rows: pl.BlockSpec((1, rows, tq), lambda bi, i: (bi, 0, i))
    seq = lambda width: pl.BlockSpec((1, s, width), lambda bi, i: (bi, 0, 0))
    qn = pl.BlockSpec((1, tq, ATT_W), lambda bi, i: (bi, i, 0))
    return pl.pallas_call(
        functools.partial(_dsa_kernel, ksel=ksel),
        grid=(b, s // tq),
        in_specs=[qt(ATT_W), seq(ATT_W),
                  pl.BlockSpec((1, nck, ATT_HEADS * ATT_VROWS, tk), lambda bi, i: (bi, 0, 0, 0)),
                  qt(IDX_QW), seq(LANES), qt(SUBLANES), qn],
        out_specs=qn,
        out_shape=jax.ShapeDtypeStruct((b, s, ATT_W), BF16),
        scratch_shapes=[pltpu.VMEM((nck, tk, tq), F32), pltpu.VMEM((nck, tk, tq), BF16),
                        pltpu.VMEM((nck, tk, tq), F32),
                        pltpu.VMEM((ATT_W // MXU_DIM, MXU_DIM, (MXU_DIM // ATT_HD) * tq), BF16),
                        pltpu.VMEM((ATT_HEADS, 1, tq), F32),
                        pltpu.VMEM((ATT_HEADS, ATT_VROWS, tq), F32)],
        compiler_params=pltpu.CompilerParams(dimension_semantics=("arbitrary", "arbitrary"),
                                             vmem_limit_bytes=VMEM_LIMIT),
        name="dsa_attention",
    )(aqt, ak, avt, iqt, ik, iwt, ag)


def _gla_kernel(gqk_ref, gv_ref, glog_ref, gg_ref, nw_ref, out_ref, state_scr):
    c_len, sub = GLA_CHUNK, GLA_SUB
    nb = gqk_ref.shape[0]
    n_z = c_len * sub

    @pl.when(pl.program_id(0) == 0)
    def _():
        state_scr[...] = jnp.zeros_like(state_scr)

    t_i = lax.broadcasted_iota(I32, (c_len, GLA_QKW), 0)
    s_i = lax.broadcasted_iota(I32, (sub, GLA_QKW), 0)
    j_i = lax.broadcasted_iota(I32, (c_len, GLA_QKW), 1) % c_len
    tril = jnp.where(lax.broadcasted_iota(I32, (c_len, c_len), 0)
                     >= lax.broadcasted_iota(I32, (c_len, c_len), 1), 1.0, 0.0).astype(BF16)
    lane_head = lax.broadcasted_iota(I32, (c_len, GLA_QKW), 1) // GLA_DK

    def stack_heads(a):
        return jnp.concatenate([jnp.where(lane_head == h, a, 0.0) for h in range(GLA_HEADS)],
                               axis=0).astype(BF16)

    zr = lax.broadcasted_iota(I32, (n_z, GLA_QKW), 0)
    zc = lax.broadcasted_iota(I32, (n_z, GLA_QKW), 1)
    pick = (zc % c_len) == (zr // (sub * sub)) * sub + zr % sub
    hr = lax.broadcasted_iota(I32, (GLA_QKW, GLA_QKW), 0) // GLA_DK
    hc = lax.broadcasted_iota(I32, (GLA_QKW, GLA_QKW), 1) // c_len
    head_rep = jnp.where(hr == hc, 1.0, 0.0).astype(BF16)
    sr = lax.broadcasted_iota(I32, (c_len, n_z), 0)
    sc_ = lax.broadcasted_iota(I32, (c_len, n_z), 1)
    row_sel = jnp.where(sr == sc_ // sub, 1.0, 0.0).astype(BF16)
    vr = lax.broadcasted_iota(I32, (GLA_HEADS * c_len, GLA_W), 0) // c_len
    vc = lax.broadcasted_iota(I32, (GLA_HEADS * c_len, GLA_W), 1) // GLA_DV
    v_diag = vr == vc
    spans = [c_len >> (lv + 1) for lv in range((c_len // sub).bit_length() - 1)]

    rows = range(nb)
    q, k, vb, b, b_last, state_t, o, a, rep = ({} for _ in range(9))

    for bi in rows:
        gqk = gqk_ref[bi]
        q[bi], k[bi] = gqk[:, :GLA_QKW], gqk[:, GLA_QKW:]
        vb[bi] = gv_ref[bi].astype(BF16)
        g = glog_ref[bi]
        g1 = g.astype(BF16)
        r1 = g - g1.astype(F32)
        g2 = r1.astype(BF16)
        g3 = (r1 - g2.astype(F32)).astype(BF16)
        b[bi] = (jnp.dot(tril, g1, preferred_element_type=F32) + jnp.dot(tril, g2, preferred_element_type=F32)
                 + jnp.dot(tril, g3, preferred_element_type=F32))
        b_last[bi] = b[bi][c_len - 1:c_len, :]

    for bi in rows:
        state_t[bi] = state_scr[bi]
        o_stack = _nt_dot(stack_heads(q[bi] * jnp.exp(b[bi])), state_t[bi].astype(BF16))
        o[bi] = jnp.concatenate([o_stack[h * c_len:(h + 1) * c_len] for h in range(GLA_HEADS)], axis=1)

        k_bd = stack_heads(k[bi] * jnp.exp(b_last[bi] - b[bi]))
        v_st = jnp.concatenate([vb[bi][:, h * GLA_DV:(h + 1) * GLA_DV] for h in range(GLA_HEADS)], axis=0)
        state_scr[bi] = jnp.exp(b_last[bi]) * state_t[bi] + _tn_dot(v_st, k_bd)

        for span in spans:
            blk = 2 * span
            ref = jnp.concatenate([jnp.broadcast_to(b[bi][m * blk + span - 1:m * blk + span, :], (blk, GLA_QKW))
                                   for m in range(c_len // blk)], axis=0)
            upper = (t_i % blk) >= span
            q_l = q[bi] * jnp.exp(jnp.where(upper, b[bi] - ref, NEG))
            k_l = k[bi] * jnp.exp(jnp.where(upper, NEG, ref - b[bi]))
            a_l = _nt_dot(q_l.astype(BF16), stack_heads(k_l))
            if blk < c_len:
                a_l = jnp.where(t_i // blk == j_i // blk, a_l, 0.0)
            a[bi] = a_l if bi not in a else a[bi] + a_l

    for bi in rows:
        zs = []
        for t in range(c_len):
            blk0 = (t // sub) * sub
            rs = slice(blk0, blk0 + sub)
            dec = jnp.exp(jnp.where(s_i <= t - blk0, b[bi][t:t + 1, :] - b[bi][rs], NEG))
            zs.append(q[bi][t:t + 1, :] * k[bi][rs] * dec)
        zmat = jnp.concatenate(zs, axis=0).astype(BF16)
        rep[bi] = jnp.dot(zmat, head_rep, preferred_element_type=F32)

    for bi in rows:
        a_all = a[bi] + jnp.dot(row_sel, jnp.where(pick, rep[bi], 0.0).astype(BF16), preferred_element_type=F32)
        v_bd = jnp.where(v_diag, jnp.concatenate([vb[bi]] * GLA_HEADS, axis=0), jnp.zeros((), BF16))
        o_bi = o[bi] + jnp.dot(a_all.astype(BF16), v_bd, preferred_element_type=F32)
        outs = []
        for h in range(GLA_HEADS):
            oh = o_bi[:, h * GLA_DV:(h + 1) * GLA_DV]
            oh = oh * lax.rsqrt(jnp.mean(oh * oh, axis=-1, keepdims=True) + NORM_EPS)
            outs.append(oh * nw_ref[...])
        out_ref[bi] = (jnp.concatenate(outs, axis=1) * gg_ref[bi]).astype(BF16)


def _gla(gqk, gv, glog, gg, norm_w):
    b, s, _ = gqk.shape
    c_len = GLA_CHUNK
    blk = lambda width: pl.BlockSpec((b, c_len, width), lambda c: (0, c, 0))
    nw2 = norm_w.reshape(1, GLA_DV)
    return pl.pallas_call(
        _gla_kernel,
        grid=(s // c_len,),
        in_specs=[blk(2 * GLA_QKW), blk(GLA_W), blk(GLA_QKW), blk(GLA_W),
                  pl.BlockSpec(nw2.shape, lambda c: (0, 0))],
        out_specs=blk(GLA_W),
        out_shape=jax.ShapeDtypeStruct((b, s, GLA_W), BF16),
        scratch_shapes=[pltpu.VMEM((b, GLA_DV, GLA_QKW), F32)],
        compiler_params=pltpu.CompilerParams(dimension_semantics=("arbitrary",),
                                             vmem_limit_bytes=VMEM_LIMIT),
        name="gla_chunked",
    )(gqk, gv, glog, gg, nw2)


def _out_kernel(att_ref, gla_ref, wa_ref, wg_ref, x_ref, nw_ref, out_ref):
    mixed = jnp.dot(att_ref[...], wa_ref[...], preferred_element_type=F32) + \
        jnp.dot(gla_ref[...], wg_ref[...], preferred_element_type=F32)
    y = mixed * lax.rsqrt(jnp.mean(mixed * mixed, axis=-1, keepdims=True) + NORM_EPS)
    out_ref[...] = x_ref[...] + y * nw_ref[...]


def _out_project(att2, gla2, w_out, x2, post_norm_w):
    n, d = x2.shape
    tm = min(PROJ_TM, n)
    wa, wg = w_out[:ATT_W].astype(BF16), w_out[ATT_W:].astype(BF16)
    row = lambda width: pl.BlockSpec((tm, width), lambda i: (i, 0))
    full = lambda a: pl.BlockSpec(a.shape, lambda i: (0, 0))
    return pl.pallas_call(
        _out_kernel,
        grid=(n // tm,),
        in_specs=[row(ATT_W), row(GLA_W), full(wa), full(wg), row(d), full(post_norm_w)],
        out_specs=row(d),
        out_shape=jax.ShapeDtypeStruct((n, d), F32),
        compiler_params=pltpu.CompilerParams(dimension_semantics=("arbitrary",),
                                             vmem_limit_bytes=VMEM_LIMIT),
        name="out_proj_postnorm",
    )(att2, gla2, wa, wg, x2, post_norm_w)


def kernel(x, positions, w_in, w_gla_gate_up, b_gla_gate, gla_norm_w, w_out, pre_norm_w, post_norm_w):
    b, s, d = x.shape
    n = b * s
    cos8, sin8 = _rope_tables(positions)
    aqt, ak, avt, iqt, ik, iwt, ag, gqk, gv, glog, gg = _project(x, pre_norm_w.reshape(1, d), cos8, sin8, w_in,
                                                                  w_gla_gate_up, b_gla_gate)
    att = _dsa_attention(aqt, ak, avt, iqt, ik, iwt, ag)
    gla = _gla(gqk, gv, glog, gg, gla_norm_w)
    out = _out_project(att.reshape(n, ATT_W), gla.reshape(n, GLA_W), w_out, x.reshape(n, d),
                       post_norm_w.reshape(1, d))
    return out.reshape(b, s, d)
```

```python
import functools

import jax
import jax.numpy as jnp
import numpy as np
from jax import lax
from jax.experimental import pallas as pl
from jax.experimental.pallas import tpu as pltpu

F32 = jnp.float32
BF16 = jnp.bfloat16
I32 = jnp.int32
HIGHEST = lax.Precision.HIGHEST

ATT_HEADS = 8
ATT_HD = 64
ATT_W = ATT_HEADS * ATT_HD
IDX_HEADS = 4
IDX_HD = 64
IDX_QW = IDX_HEADS * IDX_HD
TOPK_MAX = 256
GLA_HEADS = 4
GLA_DK = 64
GLA_DV = 128
GLA_QKW = GLA_HEADS * GLA_DK
GLA_W = GLA_HEADS * GLA_DV
GLA_GATE_RANK = 16
GLA_TAU = 16.0
GLA_CHUNK = 64
GLA_SUB = 8
GLA_STEP = 256
ROPE_THETA = 500000.0
ROT_DIM = ATT_HD // 4
ROT_HALF = ROT_DIM // 2
NORM_EPS = 1e-6
NEG = float(np.float32(-1e30))
LOG2E = 1.4426950408889634

SUBLANES = 8
LANES = 128
MXU_DIM = 256
VMEM_LIMIT = 56 * 1024 * 1024

PROJ_TM = 512
OUT_TM = 1024
DSA_TQ = 256
DSA_TK = 256
COUNT_CHAINS = 4
ATT_VROWS = ATT_HD + 16
INT_MIN = -(2 ** 31)


def _nt_dot(a, b):
    return lax.dot_general(a, b, (((1,), (1,)), ((), ())), preferred_element_type=F32)


def _tn_dot(a, b, precision=None):
    return lax.dot_general(a, b, (((0,), (0,)), ((), ())), preferred_element_type=F32,
                           precision=precision)


def _rope_table_kernel(pos_ref, inv_ref, cos_ref, sin_ref):
    ang = pos_ref[...].astype(F32) * inv_ref[...]
    cos_ref[...] = jnp.cos(ang)
    sin_ref[...] = jnp.sin(ang)


def _rope_tables(positions):
    n = positions.size
    inv = ROPE_THETA ** (-jnp.arange(0, ROT_DIM, 2, dtype=F32) / ROT_DIM)
    return pl.pallas_call(
        _rope_table_kernel,
        out_shape=(jax.ShapeDtypeStruct((ROT_HALF, n), F32),) * 2,
        name="rope_tables",
    )(positions.reshape(1, n), inv.reshape(ROT_HALF, 1))


def _proj_kernel(x_ref, pnw_ref, cos_ref, sin_ref,
                 wt_aq, wt_ak, wt_av, wt_iq, wt_ik, wt_iw, w_ag, w_gqk, w_gv, w_glr, w_gg,
                 wup_hi, wup_lo, bup_ref,
                 aqt_o, ak_o, avt_o, iqt_o, ik_o, iwt_o, ag_o, gqk_o, gv_o, glog_o, gg_o):
    x = x_ref[0]
    ms = jnp.mean(x * x, axis=-1, keepdims=True)
    h = (x * lax.rsqrt(ms + NORM_EPS)) * pnw_ref[...]
    hb = h.astype(BF16)
    cos8, sin8 = cos_ref[...], sin_ref[...]

    def mm(w_ref):
        return jnp.dot(hb, w_ref[...], preferred_element_type=F32)

    def mm_t(wt_ref):
        return _nt_dot(wt_ref[...], hb)

    def rope_t(yt):
        pieces = []
        for hd in range(yt.shape[0] // ATT_HD):
            rows = yt[hd * ATT_HD:(hd + 1) * ATT_HD]
            x1, x2 = rows[:ROT_HALF], rows[ROT_HALF:ROT_DIM]
            pieces += [x1 * cos8 - x2 * sin8, x2 * cos8 + x1 * sin8, rows[ROT_DIM:]]
        return jnp.concatenate(pieces, axis=0)

    aqt_o[0] = (rope_t(mm_t(wt_aq)) * (ATT_HD ** -0.5 * LOG2E)).astype(BF16)
    ak_o[0] = rope_t(mm_t(wt_ak)).T.astype(BF16)
    avt = mm_t(wt_av)
    ones_rows = jnp.ones((ATT_VROWS - ATT_HD, avt.shape[1]), F32)
    avt = jnp.concatenate([piece for hd in range(ATT_HEADS)
                           for piece in (avt[hd * ATT_HD:(hd + 1) * ATT_HD], ones_rows)], axis=0).astype(BF16)
    for c in range(avt_o.shape[1]):
        avt_o[0, c] = avt[:, c * DSA_TK:(c + 1) * DSA_TK]
    iqt_o[0] = rope_t(mm_t(wt_iq)).astype(BF16)
    ikt = mm_t(wt_ik)
    ik_o[0] = jnp.concatenate([rope_t(ikt[:IDX_HD]), ikt[IDX_HD:]], axis=0).T.astype(BF16)
    iwt_o[0] = mm_t(wt_iw) * (IDX_HEADS ** -0.5 * IDX_HD ** -0.5)
    a_gate = mm(w_ag)
    ag_o[0] = a_gate * jax.nn.sigmoid(a_gate)
    gqk = mm(w_gqk)
    lane = lax.broadcasted_iota(I32, gqk.shape, 1)
    gqk_o[0] = jnp.where(lane < GLA_QKW, gqk * (GLA_DK ** -0.5), gqk)
    gv_o[0] = mm(w_gv)
    glr = mm(w_glr)
    glr_hi = glr.astype(BF16)
    glr_lo = (glr - glr_hi.astype(F32)).astype(BF16)
    z = (jnp.dot(glr_hi, wup_hi[...], preferred_element_type=F32)
         + jnp.dot(glr_hi, wup_lo[...], preferred_element_type=F32)
         + jnp.dot(glr_lo, wup_hi[...], preferred_element_type=F32)) + bup_ref[...]
    glog_o[0] = (jnp.minimum(z, 0.0) - jnp.log1p(jnp.exp(-jnp.abs(z)))) * (1.0 / GLA_TAU)
    g_gate = mm(w_gg)
    gg_o[0] = g_gate * jax.nn.sigmoid(g_gate)


def _split_w_in(w_in):
    sizes = (ATT_W, ATT_W, ATT_W, IDX_QW, IDX_HD, IDX_HEADS, ATT_W,
             GLA_QKW, GLA_QKW, GLA_W, GLA_GATE_RANK, GLA_W)
    offs = np.cumsum((0,) + sizes)
    seg = [w_in[:, offs[i]:offs[i + 1]] for i in range(len(sizes))]
    a_q, a_k, a_v, i_q, i_k, i_w, a_g, g_q, g_k, g_v, g_lr, g_g = seg
    d = w_in.shape[0]
    pad = lambda w, n: jnp.concatenate([w, jnp.zeros((d, n - w.shape[1]), w.dtype)], axis=1)
    transposed = [a_q.T, a_k.T, a_v.T, i_q.T, pad(i_k, LANES).T, pad(i_w, SUBLANES).T]
    natural = [a_g, jnp.concatenate([g_q, g_k], axis=1), g_v, pad(g_lr, LANES), g_g]
    return [w.astype(BF16) for w in transposed], [w.astype(BF16) for w in natural]


def _project(x, pre_norm_w, cos8, sin8, w_in, w_up, b_up):
    b, s, d = x.shape
    wup_pad = jnp.concatenate([w_up, jnp.zeros((LANES - GLA_GATE_RANK, GLA_QKW), F32)], axis=0)
    wup_hi = wup_pad.astype(BF16)
    wup_lo = (wup_pad - wup_hi.astype(F32)).astype(BF16)
    gate_ops = [wup_hi, wup_lo, b_up.reshape(1, GLA_QKW)]
    tm = min(PROJ_TM, s)
    nt = s // tm
    cpt = tm // DSA_TK
    wts, wns = _split_w_in(w_in)
    full = lambda a: pl.BlockSpec(a.shape, lambda bi, i: (0,) * a.ndim)
    nat = lambda width: pl.BlockSpec((1, tm, width), lambda bi, i: (bi, i, 0))
    tr = lambda rows: pl.BlockSpec((1, rows, tm), lambda bi, i: (bi, 0, i))
    tab = pl.BlockSpec((ROT_HALF, tm), lambda bi, i: (0, bi * nt + i))
    vrows = ATT_HEADS * ATT_VROWS
    out_specs = [tr(ATT_W), nat(ATT_W), pl.BlockSpec((1, cpt, vrows, DSA_TK), lambda bi, i: (bi, i, 0, 0)),
                 tr(IDX_QW), nat(LANES), tr(SUBLANES),
                 nat(ATT_W), nat(2 * GLA_QKW), nat(GLA_W), nat(GLA_QKW), nat(GLA_W)]
    sds = jax.ShapeDtypeStruct
    out_shape = [sds((b, ATT_W, s), BF16), sds((b, s, ATT_W), BF16), sds((b, s // DSA_TK, vrows, DSA_TK), BF16),
                 sds((b, IDX_QW, s), BF16), sds((b, s, LANES), BF16), sds((b, SUBLANES, s), F32),
                 sds((b, s, ATT_W), F32), sds((b, s, 2 * GLA_QKW), F32), sds((b, s, GLA_W), F32),
                 sds((b, s, GLA_QKW), F32), sds((b, s, GLA_W), F32)]
    return pl.pallas_call(
        _proj_kernel,
        grid=(b, nt),
        in_specs=[nat(d), full(pre_norm_w), tab, tab] + [full(w) for w in wts + wns + gate_ops],
        out_specs=out_specs,
        out_shape=out_shape,
        compiler_params=pltpu.CompilerParams(dimension_semantics=("arbitrary", "arbitrary"),
                                             vmem_limit_bytes=VMEM_LIMIT),
        name="prenorm_in_proj",
    )(x, pre_norm_w, cos8, sin8, *wts, *wns, *gate_ops)


def _key_to_f32(key):
    return pltpu.bitcast(key ^ ((key >> 31) & 0x7FFFFFFF), F32)


def _dsa_kernel(aqt_ref, ak_ref, avt_ref, iqt_ref, ik_ref, iwt_ref, ag_ref, out_ref,
                score_scr, coarse_scr, bias_scr, qbd_scr, m_scr, acc_scr, *, ksel):
    tq, tk = DSA_TQ, DSA_TK
    i = pl.program_id(1)
    nck = (i * tq + tq + tk - 1) // tk
    groups = tk // SUBLANES

    key_iota = lax.broadcasted_iota(I32, (tk, tq), 0)
    q_pos = i * tq + lax.broadcasted_iota(I32, (tk, tq), 1)

    iqt = iqt_ref[0]
    q_cat = jnp.concatenate([iqt[h * IDX_HD:(h + 1) * IDX_HD] for h in range(IDX_HEADS)], axis=1)
    q_cat = jnp.concatenate([q_cat, jnp.zeros((LANES - IDX_HD, IDX_HEADS * tq), BF16)], axis=0)
    iwt = iwt_ref[0]
    w_rows = [iwt[h:h + 1, :] for h in range(IDX_HEADS)]

    def score_chunk(c, carry):
        kc = ik_ref[0, pl.ds(pl.multiple_of(c * tk, tk), tk), :]
        rel = jnp.dot(kc, q_cat, preferred_element_type=F32)
        sc = w_rows[0] * jnp.maximum(rel[:, 0:tq], 0.0)
        for h in range(1, IDX_HEADS):
            sc = sc + w_rows[h] * jnp.maximum(rel[:, h * tq:(h + 1) * tq], 0.0)
        sc = jnp.where(c * tk + key_iota <= q_pos, sc, NEG)
        score_scr[c] = sc
        coarse_scr[c] = sc.astype(BF16)
        return carry

    lax.fori_loop(0, nck, score_chunk, 0)

    @pl.when(nck % 2 == 1)
    def _():
        score_scr[nck] = jnp.full((tk, tq), NEG, F32)
        coarse_scr[nck] = jnp.full((tk, tq), NEG, BF16)

    def count(pred):
        def body(c2, acc):
            sc = score_scr[pl.ds(2 * c2, 2)].reshape(COUNT_CHAINS, 2 * groups // COUNT_CHAINS, SUBLANES, tq)
            return acc + jnp.sum(jnp.where(pred(sc), 1, 0), axis=1)
        acc = lax.fori_loop(0, (nck + 1) // 2, body, jnp.zeros((COUNT_CHAINS, SUBLANES, tq), I32))
        return jnp.sum(acc.reshape(COUNT_CHAINS * SUBLANES, tq), axis=0, keepdims=True)

    def count_coarse(cand_b):
        packed_rows = 2 * SUBLANES
        def body(c2, acc):
            hi = coarse_scr[pl.ds(2 * c2, 2)].reshape(COUNT_CHAINS, 2 * tk // (COUNT_CHAINS * packed_rows),
                                                      packed_rows, tq)
            hit = jnp.where(hi >= cand_b[None, None], jnp.ones((), BF16), jnp.zeros((), BF16))
            for g in range(hit.shape[1]):
                acc = acc + hit[:, g]
            return acc
        acc = lax.fori_loop(0, (nck + 1) // 2, body, jnp.zeros((COUNT_CHAINS, packed_rows, tq), BF16))
        return jnp.sum(acc.astype(F32).reshape(COUNT_CHAINS * packed_rows, tq), axis=0, keepdims=True)

    def bisect_coarse(it, prefix):
        cand = prefix + lax.shift_left(jnp.int32(1), 15 - it)
        pattern = cand ^ ((cand >> 31) & 0x7FFF)
        cand_b = pltpu.bitcast(lax.shift_left(pattern, 16), F32).astype(BF16)
        cnt = count_coarse(cand_b)
        return jnp.where(cnt >= ksel, cand, prefix)

    assert (tk // (COUNT_CHAINS * SUBLANES)) * (coarse_scr.shape[0] // 2) <= 2 ** 8, "bf16 counts must stay exact"
    key16 = lax.fori_loop(0, 16, bisect_coarse, jnp.full((2 * SUBLANES, tq), -(2 ** 15), I32))[:SUBLANES]
    key_p = lax.shift_left(key16, 16) | ((key16 >> 31) & 0xFFFF)
    key_lo = key_p - 2 ** 16

    def bisect(it, carry):
        prefix, cnt_ge = carry
        cand = prefix + lax.shift_left(jnp.int32(1), 16 - it)
        cand_f = _key_to_f32(cand)
        cnt = count(lambda sc: sc >= cand_f[None])
        take = cnt >= ksel
        return jnp.where(take, cand, prefix), jnp.where(take, cnt, cnt_ge)

    vkey, cnt_ge = lax.fori_loop(0, 17, bisect, (key_lo, jnp.zeros((1, tq), I32)))
    vstar = _key_to_f32(vkey)
    cnt_gt = count(lambda sc: sc > vstar[None])
    need = ksel - cnt_gt
    n_eq = cnt_ge - cnt_gt

    def bias_chunk(c, carry):
        bias_scr[c] = jnp.where(score_scr[c] > vstar[0:1], 0.0, NEG)
        return carry

    lax.fori_loop(0, nck, bias_chunk, 0)

    @pl.when(jnp.max(n_eq - need) > 0)
    def _():
        tri = jnp.where(lax.broadcasted_iota(I32, (tk, tk), 1) <= lax.broadcasted_iota(I32, (tk, tk), 0),
                        1.0, 0.0).astype(BF16)
        need_f = need.astype(F32)

        def tie_chunk(c, run):
            eq = score_scr[c] == vstar[0:1]
            rank = jnp.dot(tri, jnp.where(eq, 1.0, 0.0).astype(BF16), preferred_element_type=F32) + run
            take = eq & (rank <= need_f) & (c * tk + key_iota <= q_pos)
            bias_scr[c] = jnp.where(take, 0.0, bias_scr[c])
            return rank[tk - 1:tk, :]

        lax.fori_loop(0, nck, tie_chunk, jnp.zeros((1, tq), F32))

    heads_per_half = MXU_DIM // ATT_HD
    n_half = ATT_W // MXU_DIM
    row_head = lax.broadcasted_iota(I32, (MXU_DIM, tq), 0) // ATT_HD
    for h in range(ATT_HEADS):
        half, hh = divmod(h, heads_per_half)
        qh = aqt_ref[0, half * MXU_DIM:(half + 1) * MXU_DIM, :]
        qbd_scr[half, :, hh * tq:(hh + 1) * tq] = jnp.where(row_head == hh, qh, jnp.zeros_like(qh))
    m_scr[...] = jnp.full_like(m_scr, NEG)
    acc_scr[...] = jnp.zeros_like(acc_scr)

    @pl.when(nck % 2 == 1)
    def _():
        bias_scr[nck] = jnp.full((tk, tq), NEG, F32)

    n_pair = (nck + 1) // 2

    def att_pair(c2, carry):
        s_all = {}
        for half in range(n_half):
            for sub in range(2):
                rows = pl.ds(pl.multiple_of((2 * c2 + sub) * tk, tk), tk)
                s_all[sub, half] = jnp.dot(ak_ref[0, rows, half * MXU_DIM:(half + 1) * MXU_DIM], qbd_scr[half],
                                           preferred_element_type=F32)
        for half in range(n_half):
            for hh in range(heads_per_half):
                h = half * heads_per_half + hh
                sl = slice(hh * tq, (hh + 1) * tq)
                s = [s_all[sub, half][:, sl] + bias_scr[2 * c2 + sub] for sub in range(2)]
                m_old = m_scr[h]
                m_new = jnp.maximum(m_old, jnp.max(jnp.maximum(s[0], s[1]), axis=0, keepdims=True))
                alpha = jnp.exp2(m_old - m_new)
                pv = alpha * acc_scr[h]
                for sub in range(2):
                    p = jnp.exp2(s[sub] - m_new).astype(BF16)
                    vt = avt_ref[0, 2 * c2 + sub, h * ATT_VROWS:(h + 1) * ATT_VROWS, :]
                    pv = pv + jnp.dot(vt, p, preferred_element_type=F32)
                acc_scr[h] = pv
                m_scr[h] = m_new
        return carry

    lax.fori_loop(0, n_pair, att_pair, 0)

    att_t = jnp.concatenate([acc_scr[h, :ATT_HD] * (1.0 / acc_scr[h, ATT_HD:ATT_HD + 1])
                             for h in range(ATT_HEADS)], axis=0)
    out_ref[0] = (att_t.T * ag_ref[0]).astype(BF16)


def _dsa_attention(aqt, ak, avt, iqt, ik, iwt, ag):
    b, s, _ = ak.shape
    tq, tk = DSA_TQ, DSA_TK
    nck = s // tk
    ksel = min(TOPK_MAX, s // 4)
    qt = lambda rows: pl.BlockSpec((1, rows, tq), lambda bi, i: (bi, 0, i))
    seq = lambda width: pl.BlockSpec((1, s, width), lambda bi, i: (bi, 0, 0))
    qn = pl.BlockSpec((1, tq, ATT_W), lambda bi, i: (bi, i, 0))
    return pl.pallas_call(
        functools.partial(_dsa_kernel, ksel=ksel),
        grid=(b, s // tq),
        in_specs=[qt(ATT_W), seq(ATT_W),
                  pl.BlockSpec((1, nck, ATT_HEADS * ATT_VROWS, tk), lambda bi, i: (bi, 0, 0, 0)),
                  qt(IDX_QW), seq(LANES), qt(SUBLANES), qn],
        out_specs=qn,
        out_shape=jax.ShapeDtypeStruct((b, s, ATT_W), BF16),
        scratch_shapes=[pltpu.VMEM((nck, tk, tq), F32), pltpu.VMEM((nck, tk, tq), BF16),
                        pltpu.VMEM((nck, tk, tq), F32),
                        pltpu.VMEM((ATT_W // MXU_DIM, MXU_DIM, (MXU_DIM // ATT_HD) * tq), BF16),
                        pltpu.VMEM((ATT_HEADS, 1, tq), F32),
                        pltpu.VMEM((ATT_HEADS, ATT_VROWS, tq), F32)],
        compiler_params=pltpu.CompilerParams(dimension_semantics=("arbitrary", "arbitrary"),
                                             vmem_limit_bytes=VMEM_LIMIT),
        name="dsa_attention",
    )(aqt, ak, avt, iqt, ik, iwt, ag)


def _gla_kernel(gqk_ref, gv_ref, glog_ref, gg_ref, nw_ref, out_ref, state_scr):
    c_len, sub = GLA_CHUNK, GLA_SUB
    nb = gqk_ref.shape[0]
    n_z = c_len * sub

    @pl.when(pl.program_id(0) == 0)
    def _():
        state_scr[...] = jnp.zeros_like(state_scr)

    t_i = lax.broadcasted_iota(I32, (c_len, GLA_QKW), 0)
    s_i = lax.broadcasted_iota(I32, (sub, GLA_QKW), 0)
    j_i = lax.broadcasted_iota(I32, (c_len, GLA_QKW), 1) % c_len
    tril = jnp.where(lax.broadcasted_iota(I32, (c_len, c_len), 0)
                     >= lax.broadcasted_iota(I32, (c_len, c_len), 1), 1.0, 0.0).astype(BF16)
    lane_head = lax.broadcasted_iota(I32, (c_len, GLA_QKW), 1) // GLA_DK

    def stack_heads(a):
        return jnp.concatenate([jnp.where(lane_head == h, a, 0.0) for h in range(GLA_HEADS)],
                               axis=0).astype(BF16)

    zr = lax.broadcasted_iota(I32, (n_z, GLA_QKW), 0)
    zc = lax.broadcasted_iota(I32, (n_z, GLA_QKW), 1)
    pick = (zc % c_len) == (zr // (sub * sub)) * sub + zr % sub
    hr = lax.broadcasted_iota(I32, (GLA_QKW, GLA_QKW), 0) // GLA_DK
    hc = lax.broadcasted_iota(I32, (GLA_QKW, GLA_QKW), 1) // c_len
    head_rep = jnp.where(hr == hc, 1.0, 0.0).astype(BF16)
    sr = lax.broadcasted_iota(I32, (c_len, n_z), 0)
    sc_ = lax.broadcasted_iota(I32, (c_len, n_z), 1)
    row_sel = jnp.where(sr == sc_ // sub, 1.0, 0.0).astype(BF16)
    vr = lax.broadcasted_iota(I32, (GLA_HEADS * c_len, GLA_W), 0) // c_len
    vc = lax.broadcasted_iota(I32, (GLA_HEADS * c_len, GLA_W), 1) // GLA_DV
    v_diag = vr == vc
    spans = [c_len >> (lv + 1) for lv in range((c_len // sub).bit_length() - 1)]

    units = range(nb * (gqk_ref.shape[1] // c_len))
    bat = lambda u: u % nb
    tsl = lambda u: slice((u // nb) * c_len, (u // nb + 1) * c_len)
    cur_state = {bi: state_scr[bi] for bi in range(nb)}
    q, k, vb, b, b_last, state_t, o, a, rep = ({} for _ in range(9))

    for u in units:
        gqk = gqk_ref[bat(u), tsl(u), :]
        q[u], k[u] = gqk[:, :GLA_QKW], gqk[:, GLA_QKW:]
        vb[u] = gv_ref[bat(u), tsl(u), :].astype(BF16)
        g = glog_ref[bat(u), tsl(u), :]
        g1 = g.astype(BF16)
        r1 = g - g1.astype(F32)
        g2 = r1.astype(BF16)
        g3 = (r1 - g2.astype(F32)).astype(BF16)
        b[u] = (jnp.dot(tril, g1, preferred_element_type=F32) + jnp.dot(tril, g2, preferred_element_type=F32)
                 + jnp.dot(tril, g3, preferred_element_type=F32))
        b_last[u] = b[u][c_len - 1:c_len, :]

    for u in units:
        state_t[u] = cur_state[bat(u)]
        o_stack = _nt_dot(stack_heads(q[u] * jnp.exp(b[u])), state_t[u].astype(BF16))
        o[u] = jnp.concatenate([o_stack[h * c_len:(h + 1) * c_len] for h in range(GLA_HEADS)], axis=1)

        k_bd = stack_heads(k[u] * jnp.exp(b_last[u] - b[u]))
        v_st = jnp.concatenate([vb[u][:, h * GLA_DV:(h + 1) * GLA_DV] for h in range(GLA_HEADS)], axis=0)
        cur_state[bat(u)] = jnp.exp(b_last[u]) * state_t[u] + _tn_dot(v_st, k_bd)

        for span in spans:
            blk = 2 * span
            ref = jnp.concatenate([jnp.broadcast_to(b[u][m * blk + span - 1:m * blk + span, :], (blk, GLA_QKW))
                                   for m in range(c_len // blk)], axis=0)
            upper = (t_i % blk) >= span
            q_l = q[u] * jnp.exp(jnp.where(upper, b[u] - ref, NEG))
            k_l = k[u] * jnp.exp(jnp.where(upper, NEG, ref - b[u]))
            a_l = _nt_dot(q_l.astype(BF16), stack_heads(k_l))
            if blk < c_len:
                a_l = jnp.where(t_i // blk == j_i // blk, a_l, 0.0)
            a[u] = a_l if u not in a else a[u] + a_l

    for bi in range(nb):
        state_scr[bi] = cur_state[bi]

    for u in units:
        zs = []
        for t in range(c_len):
            blk0 = (t // sub) * sub
            rs = slice(blk0, blk0 + sub)
            dec = jnp.exp(jnp.where(s_i <= t - blk0, b[u][t:t + 1, :] - b[u][rs], NEG))
            zs.append(q[u][t:t + 1, :] * k[u][rs] * dec)
        zmat = jnp.concatenate(zs, axis=0).astype(BF16)
        rep[u] = jnp.dot(zmat, head_rep, preferred_element_type=F32)

    for u in units:
        a_all = a[u] + jnp.dot(row_sel, jnp.where(pick, rep[u], 0.0).astype(BF16), preferred_element_type=F32)
        v_bd = jnp.where(v_diag, jnp.concatenate([vb[u]] * GLA_HEADS, axis=0), jnp.zeros((), BF16))
        o_u = o[u] + jnp.dot(a_all.astype(BF16), v_bd, preferred_element_type=F32)
        outs = []
        for h in range(GLA_HEADS):
            oh = o_u[:, h * GLA_DV:(h + 1) * GLA_DV]
            oh = oh * lax.rsqrt(jnp.mean(oh * oh, axis=-1, keepdims=True) + NORM_EPS)
            outs.append(oh * nw_ref[...])
        out_ref[bat(u), tsl(u), :] = (jnp.concatenate(outs, axis=1) * gg_ref[bat(u), tsl(u), :]).astype(BF16)


def _gla(gqk, gv, glog, gg, norm_w):
    b, s, _ = gqk.shape
    c_len = GLA_CHUNK
    step = min(GLA_STEP, s)
    blk = lambda width: pl.BlockSpec((b, step, width), lambda c: (0, c, 0))
    nw2 = norm_w.reshape(1, GLA_DV)
    return pl.pallas_call(
        _gla_kernel,
        grid=(s // step,),
        in_specs=[blk(2 * GLA_QKW), blk(GLA_W), blk(GLA_QKW), blk(GLA_W),
                  pl.BlockSpec(nw2.shape, lambda c: (0, 0))],
        out_specs=blk(GLA_W),
        out_shape=jax.ShapeDtypeStruct((b, s, GLA_W), BF16),
        scratch_shapes=[pltpu.VMEM((b, GLA_DV, GLA_QKW), F32)],
        compiler_params=pltpu.CompilerParams(dimension_semantics=("arbitrary",),
                                             vmem_limit_bytes=VMEM_LIMIT),
        name="gla_chunked",
    )(gqk, gv, glog, gg, nw2)


def _out_kernel(att_ref, gla_ref, wa_ref, wg_ref, x_ref, nw_ref, out_ref):
    mixed = jnp.dot(att_ref[...], wa_ref[...], preferred_element_type=F32) + \
        jnp.dot(gla_ref[...], wg_ref[...], preferred_element_type=F32)
    y = mixed * lax.rsqrt(jnp.mean(mixed * mixed, axis=-1, keepdims=True) + NORM_EPS)
    out_ref[...] = x_ref[...] + y * nw_ref[...]


def _out_project(att2, gla2, w_out, x2, post_norm_w):
    n, d = x2.shape
    tm = min(OUT_TM, n)
    wa, wg = w_out[:ATT_W].astype(BF16), w_out[ATT_W:].astype(BF16)
    row = lambda width: pl.BlockSpec((tm, width), lambda i: (i, 0))
    full = lambda a: pl.BlockSpec(a.shape, lambda i: (0, 0))
    return pl.pallas_call(
        _out_kernel,
        grid=(n // tm,),
        in_specs=[row(ATT_W), row(GLA_W), full(wa), full(wg), row(d), full(post_norm_w)],
        out_specs=row(d),
        out_shape=jax.ShapeDtypeStruct((n, d), F32),
        compiler_params=pltpu.CompilerParams(dimension_semantics=("arbitrary",),
                                             vmem_limit_bytes=VMEM_LIMIT),
        name="out_proj_postnorm",
    )(att2, gla2, wa, wg, x2, post_norm_w)


def kernel(x, positions, w_in, w_gla_gate_up, b_gla_gate, gla_norm_w, w_out, pre_norm_w, post_norm_w):
    b, s, d = x.shape
    n = b * s
    cos8, sin8 = _rope_tables(positions)
    aqt, ak, avt, iqt, ik, iwt, ag, gqk, gv, glog, gg = _project(x, pre_norm_w.reshape(1, d), cos8, sin8, w_in,
                                                                  w_gla_gate_up, b_gla_gate)
    att = _dsa_attention(aqt, ak, avt, iqt, ik, iwt, ag)
    gla = _gla(gqk, gv, glog, gg, gla_norm_w)
    out = _out_project(att.reshape(n, ATT_W), gla.reshape(n, GLA_W), w_out, x.reshape(n, d),
                       post_norm_w.reshape(1, d))
    return out.reshape(b, s, d)
```

```python
import functools

import jax
import jax.numpy as jnp
import numpy as np
from jax import lax
from jax.experimental import pallas as pl
from jax.experimental.pallas import tpu as pltpu

F32 = jnp.float32
BF16 = jnp.bfloat16
I32 = jnp.int32

ATT_HEADS = 8
ATT_HD = 64
ATT_W = ATT_HEADS * ATT_HD
IDX_HEADS = 4
IDX_HD = 64
IDX_QW = IDX_HEADS * IDX_HD
TOPK_MAX = 256
GLA_HEADS = 4
GLA_DK = 64
GLA_DV = 128
GLA_QKW = GLA_HEADS * GLA_DK
GLA_W = GLA_HEADS * GLA_DV
GLA_GATE_RANK = 16
GLA_TAU = 16.0
GLA_CHUNK = 64
GLA_SUB = 8
GLA_STEP = 256
ROPE_THETA = 500000.0
ROT_DIM = ATT_HD // 4
ROT_HALF = ROT_DIM // 2
NORM_EPS = 1e-6
NEG = float(np.float32(-1e30))
LOG2E = 1.4426950408889634

SUBLANES = 8
LANES = 128
MXU_DIM = 256
VMEM_LIMIT = 56 * 1024 * 1024

PROJ_TM = 512
OUT_TM = 1024
DSA_TQ = 256
DSA_TK = 256
COUNT_CHAINS = 4
ATT_VROWS = ATT_HD + 16
INT_MIN = -(2 ** 31)


def _nt_dot(a, b):
    return lax.dot_general(a, b, (((1,), (1,)), ((), ())), preferred_element_type=F32)


def _tn_dot(a, b):
    return lax.dot_general(a, b, (((0,), (0,)), ((), ())), preferred_element_type=F32)


def _rope_table_kernel(pos_ref, inv_ref, cos_ref, sin_ref):
    ang = pos_ref[...].astype(F32) * inv_ref[...]
    cos_ref[...] = jnp.cos(ang)
    sin_ref[...] = jnp.sin(ang)


def _rope_tables(positions):
    n = positions.size
    inv = ROPE_THETA ** (-jnp.arange(0, ROT_DIM, 2, dtype=F32) / ROT_DIM)
    return pl.pallas_call(
        _rope_table_kernel,
        out_shape=(jax.ShapeDtypeStruct((ROT_HALF, n), F32),) * 2,
        name="rope_tables",
    )(positions.reshape(1, n), inv.reshape(ROT_HALF, 1))


def _proj_kernel(x_ref, pnw_ref, cos_ref, sin_ref,
                 wt_aq, wt_ak, wt_av, wt_iq, wt_ik, wt_iw, w_ag, w_gqk, w_gv, w_glr, w_gg,
                 wup_hi, wup_lo, bup_ref,
                 aqt_o, ak_o, avt_o, iqt_o, ik_o, iwt_o, ag_o, gqk_o, gv_o, glog_o, gg_o):
    x = x_ref[0]
    ms = jnp.mean(x * x, axis=-1, keepdims=True)
    h = (x * lax.rsqrt(ms + NORM_EPS)) * pnw_ref[...]
    hb = h.astype(BF16)
    cos8, sin8 = cos_ref[...], sin_ref[...]

    def mm(w_ref):
        return jnp.dot(hb, w_ref[...], preferred_element_type=F32)

    def mm_t(wt_ref):
        return _nt_dot(wt_ref[...], hb)

    def rope_t(yt):
        pieces = []
        for hd in range(yt.shape[0] // ATT_HD):
            rows = yt[hd * ATT_HD:(hd + 1) * ATT_HD]
            x1, x2 = rows[:ROT_HALF], rows[ROT_HALF:ROT_DIM]
            pieces += [x1 * cos8 - x2 * sin8, x2 * cos8 + x1 * sin8, rows[ROT_DIM:]]
        return jnp.concatenate(pieces, axis=0)

    aqt_o[0] = (rope_t(mm_t(wt_aq)) * (ATT_HD ** -0.5 * LOG2E)).astype(BF16)
    ak_o[0] = rope_t(mm_t(wt_ak)).T.astype(BF16)
    avt = mm_t(wt_av)
    ones_rows = jnp.ones((ATT_VROWS - ATT_HD, avt.shape[1]), F32)
    avt = jnp.concatenate([piece for hd in range(ATT_HEADS)
                           for piece in (avt[hd * ATT_HD:(hd + 1) * ATT_HD], ones_rows)], axis=0).astype(BF16)
    for c in range(avt_o.shape[1]):
        avt_o[0, c] = avt[:, c * DSA_TK:(c + 1) * DSA_TK]
    iqt_o[0] = rope_t(mm_t(wt_iq)).astype(BF16)
    ikt = mm_t(wt_ik)
    ik_o[0] = jnp.concatenate([rope_t(ikt[:IDX_HD]), ikt[IDX_HD:]], axis=0).T.astype(BF16)
    iwt_o[0] = mm_t(wt_iw) * (IDX_HEADS ** -0.5 * IDX_HD ** -0.5)
    a_gate = mm(w_ag)
    ag_o[0] = a_gate * jax.nn.sigmoid(a_gate)
    gqk = mm(w_gqk)
    lane = lax.broadcasted_iota(I32, gqk.shape, 1)
    gqk_o[0] = jnp.where(lane < GLA_QKW, gqk * (GLA_DK ** -0.5), gqk)
    gv_o[0] = mm(w_gv)
    glr = mm(w_glr)
    glr_hi = glr.astype(BF16)
    glr_lo = (glr - glr_hi.astype(F32)).astype(BF16)
    z = (jnp.dot(glr_hi, wup_hi[...], preferred_element_type=F32)
         + jnp.dot(glr_hi, wup_lo[...], preferred_element_type=F32)
         + jnp.dot(glr_lo, wup_hi[...], preferred_element_type=F32)) + bup_ref[...]
    glog_o[0] = (jnp.minimum(z, 0.0) - jnp.log1p(jnp.exp(-jnp.abs(z)))) * (1.0 / GLA_TAU)
    g_gate = mm(w_gg)
    gg_o[0] = g_gate * jax.nn.sigmoid(g_gate)


def _split_w_in(w_in):
    sizes = (ATT_W, ATT_W, ATT_W, IDX_QW, IDX_HD, IDX_HEADS, ATT_W,
             GLA_QKW, GLA_QKW, GLA_W, GLA_GATE_RANK, GLA_W)
    offs = np.cumsum((0,) + sizes)
    seg = [w_in[:, offs[i]:offs[i + 1]] for i in range(len(sizes))]
    a_q, a_k, a_v, i_q, i_k, i_w, a_g, g_q, g_k, g_v, g_lr, g_g = seg
    d = w_in.shape[0]
    pad = lambda w, n: jnp.concatenate([w, jnp.zeros((d, n - w.shape[1]), w.dtype)], axis=1)
    transposed = [a_q.T, a_k.T, a_v.T, i_q.T, pad(i_k, LANES).T, pad(i_w, SUBLANES).T]
    natural = [a_g, jnp.concatenate([g_q, g_k], axis=1), g_v, pad(g_lr, LANES), g_g]
    return [w.astype(BF16) for w in transposed], [w.astype(BF16) for w in natural]


def _project(x, pre_norm_w, cos8, sin8, w_in, w_up, b_up):
    b, s, d = x.shape
    wup_pad = jnp.concatenate([w_up, jnp.zeros((LANES - GLA_GATE_RANK, GLA_QKW), F32)], axis=0)
    wup_hi = wup_pad.astype(BF16)
    wup_lo = (wup_pad - wup_hi.astype(F32)).astype(BF16)
    gate_ops = [wup_hi, wup_lo, b_up.reshape(1, GLA_QKW)]
    tm = min(PROJ_TM, s)
    nt = s // tm
    cpt = tm // DSA_TK
    wts, wns = _split_w_in(w_in)
    full = lambda a: pl.BlockSpec(a.shape, lambda bi, i: (0,) * a.ndim)
    nat = lambda width: pl.BlockSpec((1, tm, width), lambda bi, i: (bi, i, 0))
    tr = lambda rows: pl.BlockSpec((1, rows, tm), lambda bi, i: (bi, 0, i))
    tab = pl.BlockSpec((ROT_HALF, tm), lambda bi, i: (0, bi * nt + i))
    vrows = ATT_HEADS * ATT_VROWS
    out_specs = [tr(ATT_W), nat(ATT_W), pl.BlockSpec((1, cpt, vrows, DSA_TK), lambda bi, i: (bi, i, 0, 0)),
                 tr(IDX_QW), nat(LANES), tr(SUBLANES),
                 nat(ATT_W), nat(2 * GLA_QKW), nat(GLA_W), nat(GLA_QKW), nat(GLA_W)]
    sds = jax.ShapeDtypeStruct
    out_shape = [sds((b, ATT_W, s), BF16), sds((b, s, ATT_W), BF16), sds((b, s // DSA_TK, vrows, DSA_TK), BF16),
                 sds((b, IDX_QW, s), BF16), sds((b, s, LANES), BF16), sds((b, SUBLANES, s), F32),
                 sds((b, s, ATT_W), F32), sds((b, s, 2 * GLA_QKW), F32), sds((b, s, GLA_W), F32),
                 sds((b, s, GLA_QKW), F32), sds((b, s, GLA_W), F32)]
    return pl.pallas_call(
        _proj_kernel,
        grid=(b, nt),
        in_specs=[nat(d), full(pre_norm_w), tab, tab] + [full(w) for w in wts + wns + gate_ops],
        out_specs=out_specs,
        out_shape=out_shape,
        compiler_params=pltpu.CompilerParams(dimension_semantics=("arbitrary", "arbitrary"),
                                             vmem_limit_bytes=VMEM_LIMIT),
        name="prenorm_in_proj",
    )(x, pre_norm_w, cos8, sin8, *wts, *wns, *gate_ops)


def _key_to_f32(key):
    return pltpu.bitcast(key ^ ((key >> 31) & 0x7FFFFFFF), F32)


def _dsa_kernel(aqt_ref, ak_ref, avt_ref, iqt_ref, ik_ref, iwt_ref, ag_ref, out_ref,
                score_scr, coarse_scr, bias_scr, qbd_scr, m_scr, acc_scr, *, ksel):
    tq, tk = DSA_TQ, DSA_TK
    i = pl.program_id(1)
    nck = (i * tq + tq + tk - 1) // tk
    groups = tk // SUBLANES

    key_iota = lax.broadcasted_iota(I32, (tk, tq), 0)
    q_pos = i * tq + lax.broadcasted_iota(I32, (tk, tq), 1)

    iqt = iqt_ref[0]
    q_cat = jnp.concatenate([iqt[h * IDX_HD:(h + 1) * IDX_HD] for h in range(IDX_HEADS)], axis=1)
    q_cat = jnp.concatenate([q_cat, jnp.zeros((LANES - IDX_HD, IDX_HEADS * tq), BF16)], axis=0)
    iwt = iwt_ref[0]
    w_rows = [iwt[h:h + 1, :] for h in range(IDX_HEADS)]

    n_pair = (nck + 1) // 2

    def score_pair(c2, carry):
        chunks = (2 * c2, 2 * c2 + 1)
        rels = [jnp.dot(ik_ref[0, pl.ds(pl.multiple_of(c * tk, tk), tk), :], q_cat,
                        preferred_element_type=F32) for c in chunks]
        for c, rel in zip(chunks, rels):
            sc = w_rows[0] * jnp.maximum(rel[:, 0:tq], 0.0)
            for h in range(1, IDX_HEADS):
                sc = sc + w_rows[h] * jnp.maximum(rel[:, h * tq:(h + 1) * tq], 0.0)
            sc = jnp.where(c * tk + key_iota <= q_pos, sc, NEG)
            score_scr[c] = sc
            coarse_scr[c] = sc.astype(BF16)
        return carry

    lax.fori_loop(0, n_pair, score_pair, 0)

    def count(pred):
        def body(c2, acc):
            sc = score_scr[pl.ds(2 * c2, 2)].reshape(COUNT_CHAINS, 2 * groups // COUNT_CHAINS, SUBLANES, tq)
            return acc + jnp.sum(jnp.where(pred(sc), 1, 0), axis=1)
        acc = lax.fori_loop(0, n_pair, body, jnp.zeros((COUNT_CHAINS, SUBLANES, tq), I32))
        return jnp.sum(acc.reshape(COUNT_CHAINS * SUBLANES, tq), axis=0, keepdims=True)

    def count_coarse(cand_b):
        packed_rows = 2 * SUBLANES
        def body(c2, acc):
            hi = coarse_scr[pl.ds(2 * c2, 2)].reshape(COUNT_CHAINS, 2 * tk // (COUNT_CHAINS * packed_rows),
                                                      packed_rows, tq)
            hit = jnp.where(hi >= cand_b[None, None], jnp.ones((), BF16), jnp.zeros((), BF16))
            for g in range(hit.shape[1]):
                acc = acc + hit[:, g]
            return acc
        acc = lax.fori_loop(0, n_pair, body, jnp.zeros((COUNT_CHAINS, packed_rows, tq), BF16))
        return jnp.sum(acc.astype(F32).reshape(COUNT_CHAINS * packed_rows, tq), axis=0, keepdims=True)

    def bisect_coarse(it, prefix):
        cand = prefix + lax.shift_left(jnp.int32(1), 15 - it)
        pattern = cand ^ ((cand >> 31) & 0x7FFF)
        cand_b = pltpu.bitcast(lax.shift_left(pattern, 16), F32).astype(BF16)
        cnt = count_coarse(cand_b)
        return jnp.where(cnt >= ksel, cand, prefix)

    assert (tk // (COUNT_CHAINS * SUBLANES)) * (coarse_scr.shape[0] // 2) <= 2 ** 8, "bf16 counts must stay exact"
    key16 = lax.fori_loop(0, 16, bisect_coarse, jnp.full((2 * SUBLANES, tq), -(2 ** 15), I32))[:SUBLANES]
    key_p = lax.shift_left(key16, 16) | ((key16 >> 31) & 0xFFFF)
    key_lo = key_p - 2 ** 16

    def bisect(it, prefix):
        cand = prefix + lax.shift_left(jnp.int32(1), 16 - it)
        cand_f = _key_to_f32(cand)
        cnt = count(lambda sc: sc >= cand_f[None])
        return jnp.where(cnt >= ksel, cand, prefix)

    vstar = _key_to_f32(lax.fori_loop(0, 17, bisect, key_lo))
    need_f = (ksel - count(lambda sc: sc > vstar[None])).astype(F32)

    tri = jnp.where(lax.broadcasted_iota(I32, (tk, tk), 1) <= lax.broadcasted_iota(I32, (tk, tk), 0),
                    1.0, 0.0).astype(BF16)

    def select_pair(c2, run):
        chunks = (2 * c2, 2 * c2 + 1)
        scs = [score_scr[c] for c in chunks]
        eqs = [sc == vstar[0:1] for sc in scs]
        ranks = [jnp.dot(tri, jnp.where(eq, 1.0, 0.0).astype(BF16), preferred_element_type=F32) for eq in eqs]
        for c, sc, eq, rank in zip(chunks, scs, eqs, ranks):
            rank = rank + run
            tie_bias = jnp.where(rank <= need_f, jnp.where(c * tk + key_iota <= q_pos, 0.0, NEG), NEG)
            bias_scr[c] = jnp.where(sc > vstar[0:1], 0.0, jnp.where(eq, tie_bias, NEG))
            run = rank[tk - 1:tk, :]
        return run

    lax.fori_loop(0, n_pair, select_pair, jnp.zeros((1, tq), F32))

    heads_per_half = MXU_DIM // ATT_HD
    n_half = ATT_W // MXU_DIM
    row_head = lax.broadcasted_iota(I32, (MXU_DIM, tq), 0) // ATT_HD
    for h in range(ATT_HEADS):
        half, hh = divmod(h, heads_per_half)
        qh = aqt_ref[0, half * MXU_DIM:(half + 1) * MXU_DIM, :]
        qbd_scr[half, :, hh * tq:(hh + 1) * tq] = jnp.where(row_head == hh, qh, jnp.zeros_like(qh))
    m_scr[...] = jnp.full_like(m_scr, NEG)
    acc_scr[...] = jnp.zeros_like(acc_scr)

    def att_pair(c2, carry):
        s_all = {}
        for half in range(n_half):
            for sub in range(2):
                rows = pl.ds(pl.multiple_of((2 * c2 + sub) * tk, tk), tk)
                s_all[sub, half] = jnp.dot(ak_ref[0, rows, half * MXU_DIM:(half + 1) * MXU_DIM], qbd_scr[half],
                                           preferred_element_type=F32)
        for half in range(n_half):
            for hh in range(heads_per_half):
                h = half * heads_per_half + hh
                sl = slice(hh * tq, (hh + 1) * tq)
                s = [s_all[sub, half][:, sl] + bias_scr[2 * c2 + sub] for sub in range(2)]
                m_old = m_scr[h]
                m_new = jnp.maximum(m_old, jnp.max(jnp.maximum(s[0], s[1]), axis=0, keepdims=True))
                alpha = jnp.exp2(m_old - m_new)
                pv = alpha * acc_scr[h]
                for sub in range(2):
                    p = jnp.exp2(s[sub] - m_new).astype(BF16)
                    vt = avt_ref[0, 2 * c2 + sub, h * ATT_VROWS:(h + 1) * ATT_VROWS, :]
                    pv = pv + jnp.dot(vt, p, preferred_element_type=F32)
                acc_scr[h] = pv
                m_scr[h] = m_new
        return carry

    lax.fori_loop(0, n_pair, att_pair, 0)

    att_t = jnp.concatenate([acc_scr[h, :ATT_HD] * (1.0 / acc_scr[h, ATT_HD:ATT_HD + 1])
                             for h in range(ATT_HEADS)], axis=0)
    out_ref[0] = (att_t.T * ag_ref[0]).astype(BF16)


def _dsa_attention(aqt, ak, avt, iqt, ik, iwt, ag):
    b, s, _ = ak.shape
    tq, tk = DSA_TQ, DSA_TK
    nck = s // tk
    ksel = min(TOPK_MAX, s // 4)
    qt = lambda rows: pl.BlockSpec((1, rows, tq), lambda bi, i: (bi, 0, i))
    seq = lambda width: pl.BlockSpec((1, s, width), lambda bi, i: (bi, 0, 0))
    qn = pl.BlockSpec((1, tq, ATT_W), lambda bi, i: (bi, i, 0))
    return pl.pallas_call(
        functools.partial(_dsa_kernel, ksel=ksel),
        grid=(b, s // tq),
        in_specs=[qt(ATT_W), seq(ATT_W),
                  pl.BlockSpec((1, nck, ATT_HEADS * ATT_VROWS, tk), lambda bi, i: (bi, 0, 0, 0)),
                  qt(IDX_QW), seq(LANES), qt(SUBLANES), qn],
        out_specs=qn,
        out_shape=jax.ShapeDtypeStruct((b, s, ATT_W), BF16),
        scratch_shapes=[pltpu.VMEM((nck, tk, tq), F32), pltpu.VMEM((nck, tk, tq), BF16),
                        pltpu.VMEM((nck, tk, tq), F32),
                        pltpu.VMEM((ATT_W // MXU_DIM, MXU_DIM, (MXU_DIM // ATT_HD) * tq), BF16),
                        pltpu.VMEM((ATT_HEADS, 1, tq), F32),
                        pltpu.VMEM((ATT_HEADS, ATT_VROWS, tq), F32)],
        compiler_params=pltpu.CompilerParams(dimension_semantics=("arbitrary", "arbitrary"),
                                             vmem_limit_bytes=VMEM_LIMIT),
        name="dsa_attention",
    )(aqt, ak, avt, iqt, ik, iwt, ag)


def _gla_kernel(gqk_ref, gv_ref, glog_ref, gg_ref, nw_ref, out_ref, state_scr):
    c_len, sub = GLA_CHUNK, GLA_SUB
    nb = gqk_ref.shape[0]
    n_z = c_len * sub

    @pl.when(pl.program_id(0) == 0)
    def _():
        state_scr[...] = jnp.zeros_like(state_scr)

    t_i = lax.broadcasted_iota(I32, (c_len, GLA_QKW), 0)
    s_i = lax.broadcasted_iota(I32, (sub, GLA_QKW), 0)
    j_i = lax.broadcasted_iota(I32, (c_len, GLA_QKW), 1) % c_len
    tril = jnp.where(lax.broadcasted_iota(I32, (c_len, c_len), 0)
                     >= lax.broadcasted_iota(I32, (c_len, c_len), 1), 1.0, 0.0).astype(BF16)
    lane_head = lax.broadcasted_iota(I32, (c_len, GLA_QKW), 1) // GLA_DK

    def stack_heads(a):
        return jnp.concatenate([jnp.where(lane_head == h, a, 0.0) for h in range(GLA_HEADS)],
                               axis=0).astype(BF16)

    zr = lax.broadcasted_iota(I32, (n_z, GLA_QKW), 0)
    zc = lax.broadcasted_iota(I32, (n_z, GLA_QKW), 1)
    pick = (zc % c_len) == (zr // (sub * sub)) * sub + zr % sub
    hr = lax.broadcasted_iota(I32, (GLA_QKW, GLA_QKW), 0) // GLA_DK
    hc = lax.broadcasted_iota(I32, (GLA_QKW, GLA_QKW), 1) // c_len
    head_rep = jnp.where(hr == hc, 1.0, 0.0).astype(BF16)
    sr = lax.broadcasted_iota(I32, (c_len, n_z), 0)
    sc_ = lax.broadcasted_iota(I32, (c_len, n_z), 1)
    row_sel = jnp.where(sr == sc_ // sub, 1.0, 0.0).astype(BF16)
    vr = lax.broadcasted_iota(I32, (GLA_HEADS * c_len, GLA_W), 0) // c_len
    vc = lax.broadcasted_iota(I32, (GLA_HEADS * c_len, GLA_W), 1) // GLA_DV
    v_diag = vr == vc
    spans = [c_len >> (lv + 1) for lv in range((c_len // sub).bit_length() - 1)]

    units = range(nb * (gqk_ref.shape[1] // c_len))
    bat = lambda u: u % nb
    tsl = lambda u: slice((u // nb) * c_len, (u // nb + 1) * c_len)
    cur_state = {bi: state_scr[bi] for bi in range(nb)}
    q, k, vb, b, b_last, state_t, o, a, rep = ({} for _ in range(9))

    for u in units:
        gqk = gqk_ref[bat(u), tsl(u), :]
        q[u], k[u] = gqk[:, :GLA_QKW], gqk[:, GLA_QKW:]
        vb[u] = gv_ref[bat(u), tsl(u), :].astype(BF16)
        g = glog_ref[bat(u), tsl(u), :]
        g1 = g.astype(BF16)
        r1 = g - g1.astype(F32)
        g2 = r1.astype(BF16)
        g3 = (r1 - g2.astype(F32)).astype(BF16)
        b[u] = (jnp.dot(tril, g1, preferred_element_type=F32) + jnp.dot(tril, g2, preferred_element_type=F32)
                 + jnp.dot(tril, g3, preferred_element_type=F32))
        b_last[u] = b[u][c_len - 1:c_len, :]

    for u in units:
        state_t[u] = cur_state[bat(u)]
        o_stack = _nt_dot(stack_heads(q[u] * jnp.exp(b[u])), state_t[u].astype(BF16))
        o[u] = jnp.concatenate([o_stack[h * c_len:(h + 1) * c_len] for h in range(GLA_HEADS)], axis=1)

        k_bd = stack_heads(k[u] * jnp.exp(b_last[u] - b[u]))
        v_st = jnp.concatenate([vb[u][:, h * GLA_DV:(h + 1) * GLA_DV] for h in range(GLA_HEADS)], axis=0)
        cur_state[bat(u)] = jnp.exp(b_last[u]) * state_t[u] + _tn_dot(v_st, k_bd)

        for span in spans:
            blk = 2 * span
            ref = jnp.concatenate([jnp.broadcast_to(b[u][m * blk + span - 1:m * blk + span, :], (blk, GLA_QKW))
                                   for m in range(c_len // blk)], axis=0)
            upper = (t_i % blk) >= span
            q_l = q[u] * jnp.exp(jnp.where(upper, b[u] - ref, NEG))
            k_l = k[u] * jnp.exp(jnp.where(upper, NEG, ref - b[u]))
            a_l = _nt_dot(q_l.astype(BF16), stack_heads(k_l))
            if blk < c_len:
                a_l = jnp.where(t_i // blk == j_i // blk, a_l, 0.0)
            a[u] = a_l if u not in a else a[u] + a_l

    for bi in range(nb):
        state_scr[bi] = cur_state[bi]

    for u in units:
        zs = []
        for t in range(c_len):
            blk0 = (t // sub) * sub
            rs = slice(blk0, blk0 + sub)
            dec = jnp.exp(jnp.where(s_i <= t - blk0, b[u][t:t + 1, :] - b[u][rs], NEG))
            zs.append(q[u][t:t + 1, :] * k[u][rs] * dec)
        zmat = jnp.concatenate(zs, axis=0).astype(BF16)
        rep[u] = jnp.dot(zmat, head_rep, preferred_element_type=F32)

    for u in units:
        a_all = a[u] + jnp.dot(row_sel, jnp.where(pick, rep[u], 0.0).astype(BF16), preferred_element_type=F32)
        v_bd = jnp.where(v_diag, jnp.concatenate([vb[u]] * GLA_HEADS, axis=0), jnp.zeros((), BF16))
        o_u = o[u] + jnp.dot(a_all.astype(BF16), v_bd, preferred_element_type=F32)
        outs = []
        for h in range(GLA_HEADS):
            oh = o_u[:, h * GLA_DV:(h + 1) * GLA_DV]
            oh = oh * lax.rsqrt(jnp.mean(oh * oh, axis=-1, keepdims=True) + NORM_EPS)
            outs.append(oh * nw_ref[...])
        out_ref[bat(u), tsl(u), :] = (jnp.concatenate(outs, axis=1) * gg_ref[bat(u), tsl(u), :]).astype(BF16)


def _gla(gqk, gv, glog, gg, norm_w):
    b, s, _ = gqk.shape
    c_len = GLA_CHUNK
    step = min(GLA_STEP, s)
    blk = lambda width: pl.BlockSpec((b, step, width), lambda c: (0, c, 0))
    nw2 = norm_w.reshape(1, GLA_DV)
    return pl.pallas_call(
        _gla_kernel,
        grid=(s // step,),
        in_specs=[blk(2 * GLA_QKW), blk(GLA_W), blk(GLA_QKW), blk(GLA_W),
                  pl.BlockSpec(nw2.shape, lambda c: (0, 0))],
        out_specs=blk(GLA_W),
        out_shape=jax.ShapeDtypeStruct((b, s, GLA_W), BF16),
        scratch_shapes=[pltpu.VMEM((b, GLA_DV, GLA_QKW), F32)],
        compiler_params=pltpu.CompilerParams(dimension_semantics=("arbitrary",),
                                             vmem_limit_bytes=VMEM_LIMIT),
        name="gla_chunked",
    )(gqk, gv, glog, gg, nw2)


def _out_kernel(att_ref, gla_ref, wa_ref, wg_ref, x_ref, nw_ref, out_ref):
    mixed = jnp.dot(att_ref[...], wa_ref[...], preferred_element_type=F32) + \
        jnp.dot(gla_ref[...], wg_ref[...], preferred_element_type=F32)
    y = mixed * lax.rsqrt(jnp.mean(mixed * mixed, axis=-1, keepdims=True) + NORM_EPS)
    out_ref[...] = x_ref[...] + y * nw_ref[...]


def _out_project(att2, gla2, w_out, x2, post_norm_w):
    n, d = x2.shape
    tm = min(OUT_TM, n)
    wa, wg = w_out[:ATT_W].astype(BF16), w_out[ATT_W:].astype(BF16)
    row = lambda width: pl.BlockSpec((tm, width), lambda i: (i, 0))
    full = lambda a: pl.BlockSpec(a.shape, lambda i: (0, 0))
    return pl.pallas_call(
        _out_kernel,
        grid=(n // tm,),
        in_specs=[row(ATT_W), row(GLA_W), full(wa), full(wg), row(d), full(post_norm_w)],
        out_specs=row(d),
        out_shape=jax.ShapeDtypeStruct((n, d), F32),
        compiler_params=pltpu.CompilerParams(dimension_semantics=("arbitrary",),
                                             vmem_limit_bytes=VMEM_LIMIT),
        name="out_proj_postnorm",
    )(att2, gla2, wa, wg, x2, post_norm_w)


def kernel(x, positions, w_in, w_gla_gate_up, b_gla_gate, gla_norm_w, w_out, pre_norm_w, post_norm_w):
    b, s, d = x.shape
    n = b * s
    cos8, sin8 = _rope_tables(positions)
    aqt, ak, avt, iqt, ik, iwt, ag, gqk, gv, glog, gg = _project(x, pre_norm_w.reshape(1, d), cos8, sin8, w_in,
                                                                  w_gla_gate_up, b_gla_gate)
    att = _dsa_attention(aqt, ak, avt, iqt, ik, iwt, ag)
    gla = _gla(gqk, gv, glog, gg, gla_norm_w)
    out = _out_project(att.reshape(n, ATT_W), gla.reshape(n, GLA_W), w_out, x.reshape(n, d),
                       post_norm_w.reshape(1, d))
    return out.reshape(b, s, d)
```

```python
import functools

import jax
import jax.numpy as jnp
import numpy as np
from jax import lax
from jax.experimental import pallas as pl
from jax.experimental.pallas import tpu as pltpu

F32 = jnp.float32
BF16 = jnp.bfloat16
I32 = jnp.int32

ATT_HEADS = 8
ATT_HD = 64
ATT_W = ATT_HEADS * ATT_HD
IDX_HEADS = 4
IDX_HD = 64
IDX_QW = IDX_HEADS * IDX_HD
TOPK_MAX = 256
GLA_HEADS = 4
GLA_DK = 64
GLA_DV = 128
GLA_QKW = GLA_HEADS * GLA_DK
GLA_W = GLA_HEADS * GLA_DV
GLA_GATE_RANK = 16
GLA_TAU = 16.0
GLA_CHUNK = 64
GLA_SUB = 8
GLA_STEP = 256
ROPE_THETA = 500000.0
ROT_DIM = ATT_HD // 4
ROT_HALF = ROT_DIM // 2
NORM_EPS = 1e-6
NEG = float(np.float32(-1e30))
LOG2E = 1.4426950408889634

SUBLANES = 8
LANES = 128
MXU_DIM = 256
VMEM_LIMIT = 56 * 1024 * 1024

PROJ_TM = 512
OUT_TM = 1024
DSA_TQ = 512
DSA_TK = 256
COUNT_CHAINS = 4
ATT_VROWS = ATT_HD + 16
INT_MIN = -(2 ** 31)


def _nt_dot(a, b):
    return lax.dot_general(a, b, (((1,), (1,)), ((), ())), preferred_element_type=F32)


def _tn_dot(a, b):
    return lax.dot_general(a, b, (((0,), (0,)), ((), ())), preferred_element_type=F32)


def _rope_table_kernel(pos_ref, inv_ref, cos_ref, sin_ref):
    ang = pos_ref[...].astype(F32) * inv_ref[...]
    cos_ref[...] = jnp.cos(ang)
    sin_ref[...] = jnp.sin(ang)


def _rope_tables(positions):
    n = positions.size
    inv = ROPE_THETA ** (-jnp.arange(0, ROT_DIM, 2, dtype=F32) / ROT_DIM)
    return pl.pallas_call(
        _rope_table_kernel,
        out_shape=(jax.ShapeDtypeStruct((ROT_HALF, n), F32),) * 2,
        name="rope_tables",
    )(positions.reshape(1, n), inv.reshape(ROT_HALF, 1))


def _proj_kernel(x_ref, pnw_ref, cos_ref, sin_ref,
                 wt_aq, wt_ak, wt_av, wt_iq, wt_ik, wt_iw, w_ag, w_gqk, w_gv, w_glr, w_gg,
                 wup_hi, wup_lo, bup_ref,
                 aqt_o, ak_o, avt_o, iqt_o, ik_o, iwt_o, ag_o, gqk_o, gv_o, glog_o, gg_o):
    x = x_ref[0]
    ms = jnp.mean(x * x, axis=-1, keepdims=True)
    h = (x * lax.rsqrt(ms + NORM_EPS)) * pnw_ref[...]
    hb = h.astype(BF16)
    cos8, sin8 = cos_ref[...], sin_ref[...]

    def mm(w_ref):
        return jnp.dot(hb, w_ref[...], preferred_element_type=F32)

    def mm_t(wt_ref):
        return _nt_dot(wt_ref[...], hb)

    def rope_t(yt):
        pieces = []
        for hd in range(yt.shape[0] // ATT_HD):
            rows = yt[hd * ATT_HD:(hd + 1) * ATT_HD]
            x1, x2 = rows[:ROT_HALF], rows[ROT_HALF:ROT_DIM]
            pieces += [x1 * cos8 - x2 * sin8, x2 * cos8 + x1 * sin8, rows[ROT_DIM:]]
        return jnp.concatenate(pieces, axis=0)

    aqt_o[0] = (rope_t(mm_t(wt_aq)) * (ATT_HD ** -0.5 * LOG2E)).astype(BF16)
    ak_o[0] = rope_t(mm_t(wt_ak)).T.astype(BF16)
    avt = mm_t(wt_av)
    ones_rows = jnp.ones((ATT_VROWS - ATT_HD, avt.shape[1]), F32)
    avt = jnp.concatenate([piece for hd in range(ATT_HEADS)
                           for piece in (avt[hd * ATT_HD:(hd + 1) * ATT_HD], ones_rows)], axis=0).astype(BF16)
    for c in range(avt_o.shape[1]):
        avt_o[0, c] = avt[:, c * DSA_TK:(c + 1) * DSA_TK]
    iqt_o[0] = rope_t(mm_t(wt_iq)).astype(BF16)
    ikt = mm_t(wt_ik)
    ik_o[0] = jnp.concatenate([rope_t(ikt[:IDX_HD]), ikt[IDX_HD:]], axis=0).T.astype(BF16)
    iwt_o[0] = mm_t(wt_iw) * (IDX_HEADS ** -0.5 * IDX_HD ** -0.5)
    a_gate = mm(w_ag)
    ag_o[0] = a_gate * jax.nn.sigmoid(a_gate)
    gqk = mm(w_gqk)
    lane = lax.broadcasted_iota(I32, gqk.shape, 1)
    gqk_o[0] = jnp.where(lane < GLA_QKW, gqk * (GLA_DK ** -0.5), gqk)
    gv_o[0] = mm(w_gv)
    glr = mm(w_glr)
    glr_hi = glr.astype(BF16)
    glr_lo = (glr - glr_hi.astype(F32)).astype(BF16)
    z = (jnp.dot(glr_hi, wup_hi[...], preferred_element_type=F32)
         + jnp.dot(glr_hi, wup_lo[...], preferred_element_type=F32)
         + jnp.dot(glr_lo, wup_hi[...], preferred_element_type=F32)) + bup_ref[...]
    glog_o[0] = (jnp.minimum(z, 0.0) - jnp.log1p(jnp.exp(-jnp.abs(z)))) * (1.0 / GLA_TAU)
    g_gate = mm(w_gg)
    gg_o[0] = g_gate * jax.nn.sigmoid(g_gate)


def _split_w_in(w_in):
    sizes = (ATT_W, ATT_W, ATT_W, IDX_QW, IDX_HD, IDX_HEADS, ATT_W,
             GLA_QKW, GLA_QKW, GLA_W, GLA_GATE_RANK, GLA_W)
    offs = np.cumsum((0,) + sizes)
    seg = [w_in[:, offs[i]:offs[i + 1]] for i in range(len(sizes))]
    a_q, a_k, a_v, i_q, i_k, i_w, a_g, g_q, g_k, g_v, g_lr, g_g = seg
    d = w_in.shape[0]
    pad = lambda w, n: jnp.concatenate([w, jnp.zeros((d, n - w.shape[1]), w.dtype)], axis=1)
    transposed = [a_q.T, a_k.T, a_v.T, i_q.T, pad(i_k, LANES).T, pad(i_w, SUBLANES).T]
    natural = [a_g, jnp.concatenate([g_q, g_k], axis=1), g_v, pad(g_lr, LANES), g_g]
    return [w.astype(BF16) for w in transposed], [w.astype(BF16) for w in natural]


def _project(x, pre_norm_w, cos8, sin8, w_in, w_up, b_up):
    b, s, d = x.shape
    wup_pad = jnp.concatenate([w_up, jnp.zeros((LANES - GLA_GATE_RANK, GLA_QKW), F32)], axis=0)
    wup_hi = wup_pad.astype(BF16)
    wup_lo = (wup_pad - wup_hi.astype(F32)).astype(BF16)
    gate_ops = [wup_hi, wup_lo, b_up.reshape(1, GLA_QKW)]
    tm = min(PROJ_TM, s)
    nt = s // tm
    cpt = tm // DSA_TK
    wts, wns = _split_w_in(w_in)
    full = lambda a: pl.BlockSpec(a.shape, lambda bi, i: (0,) * a.ndim)
    nat = lambda width: pl.BlockSpec((1, tm, width), lambda bi, i: (bi, i, 0))
    tr = lambda rows: pl.BlockSpec((1, rows, tm), lambda bi, i: (bi, 0, i))
    tab = pl.BlockSpec((ROT_HALF, tm), lambda bi, i: (0, bi * nt + i))
    vrows = ATT_HEADS * ATT_VROWS
    out_specs = [tr(ATT_W), nat(ATT_W), pl.BlockSpec((1, cpt, vrows, DSA_TK), lambda bi, i: (bi, i, 0, 0)),
                 tr(IDX_QW), nat(LANES), tr(SUBLANES),
                 nat(ATT_W), nat(2 * GLA_QKW), nat(GLA_W), nat(GLA_QKW), nat(GLA_W)]
    sds = jax.ShapeDtypeStruct
    out_shape = [sds((b, ATT_W, s), BF16), sds((b, s, ATT_W), BF16), sds((b, s // DSA_TK, vrows, DSA_TK), BF16),
                 sds((b, IDX_QW, s), BF16), sds((b, s, LANES), BF16), sds((b, SUBLANES, s), F32),
                 sds((b, s, ATT_W), F32), sds((b, s, 2 * GLA_QKW), F32), sds((b, s, GLA_W), F32),
                 sds((b, s, GLA_QKW), F32), sds((b, s, GLA_W), F32)]
    return pl.pallas_call(
        _proj_kernel,
        grid=(b, nt),
        in_specs=[nat(d), full(pre_norm_w), tab, tab] + [full(w) for w in wts + wns + gate_ops],
        out_specs=out_specs,
        out_shape=out_shape,
        compiler_params=pltpu.CompilerParams(dimension_semantics=("arbitrary", "arbitrary"),
                                             vmem_limit_bytes=VMEM_LIMIT),
        name="prenorm_in_proj",
    )(x, pre_norm_w, cos8, sin8, *wts, *wns, *gate_ops)


def _key_to_f32(key):
    return pltpu.bitcast(key ^ ((key >> 31) & 0x7FFFFFFF), F32)


def _dsa_kernel(aqt_ref, ak_ref, avt_ref, iqt_ref, ik_ref, iwt_ref, ag_ref, out_ref,
                score_scr, coarse_scr, bias_scr, qbd_scr, m_scr, acc_scr, *, ksel):
    tq, tk = DSA_TQ, DSA_TK
    i = pl.program_id(1)
    nck = (i * tq + tq + tk - 1) // tk
    groups = tk // SUBLANES

    key_iota = lax.broadcasted_iota(I32, (tk, tq), 0)
    q_pos = i * tq + lax.broadcasted_iota(I32, (tk, tq), 1)

    iqt = iqt_ref[0]
    q_cat = jnp.concatenate([iqt[h * IDX_HD:(h + 1) * IDX_HD] for h in range(IDX_HEADS)], axis=1)
    q_cat = jnp.concatenate([q_cat, jnp.zeros((LANES - IDX_HD, IDX_HEADS * tq), BF16)], axis=0)
    iwt = iwt_ref[0]
    w_rows = [iwt[h:h + 1, :] for h in range(IDX_HEADS)]

    n_pair = (nck + 1) // 2

    def score_pair(c2, carry):
        chunks = (2 * c2, 2 * c2 + 1)
        rels = [jnp.dot(ik_ref[0, pl.ds(pl.multiple_of(c * tk, tk), tk), :], q_cat,
                        preferred_element_type=F32) for c in chunks]
        for c, rel in zip(chunks, rels):
            sc = w_rows[0] * jnp.maximum(rel[:, 0:tq], 0.0)
            for h in range(1, IDX_HEADS):
                sc = sc + w_rows[h] * jnp.maximum(rel[:, h * tq:(h + 1) * tq], 0.0)
            sc = jnp.where(c * tk + key_iota <= q_pos, sc, NEG)
            score_scr[c] = sc
            coarse_scr[c] = sc.astype(BF16)
        return carry

    lax.fori_loop(0, n_pair, score_pair, 0)

    def count(pred):
        def body(c2, acc):
            sc = score_scr[pl.ds(2 * c2, 2)].reshape(COUNT_CHAINS, 2 * groups // COUNT_CHAINS, SUBLANES, tq)
            return acc + jnp.sum(jnp.where(pred(sc), 1, 0), axis=1)
        acc = lax.fori_loop(0, n_pair, body, jnp.zeros((COUNT_CHAINS, SUBLANES, tq), I32))
        return jnp.sum(acc.reshape(COUNT_CHAINS * SUBLANES, tq), axis=0, keepdims=True)

    def count_coarse(cand_b):
        packed_rows = 2 * SUBLANES
        def body(c2, acc):
            hi = coarse_scr[pl.ds(2 * c2, 2)].reshape(COUNT_CHAINS, 2 * tk // (COUNT_CHAINS * packed_rows),
                                                      packed_rows, tq)
            hit = jnp.where(hi >= cand_b[None, None], jnp.ones((), BF16), jnp.zeros((), BF16))
            for g in range(hit.shape[1]):
                acc = acc + hit[:, g]
            return acc
        acc = lax.fori_loop(0, n_pair, body, jnp.zeros((COUNT_CHAINS, packed_rows, tq), BF16))
        return jnp.sum(acc.astype(F32).reshape(COUNT_CHAINS * packed_rows, tq), axis=0, keepdims=True)

    def bisect_coarse(it, prefix):
        cand = prefix + lax.shift_left(jnp.int32(1), 15 - it)
        pattern = cand ^ ((cand >> 31) & 0x7FFF)
        cand_b = pltpu.bitcast(lax.shift_left(pattern, 16), F32).astype(BF16)
        cnt = count_coarse(cand_b)
        return jnp.where(cnt >= ksel, cand, prefix)

    assert (tk // (COUNT_CHAINS * SUBLANES)) * (coarse_scr.shape[0] // 2) <= 2 ** 8, "bf16 counts must stay exact"
    key16 = lax.fori_loop(0, 16, bisect_coarse, jnp.full((2 * SUBLANES, tq), -(2 ** 15), I32))[:SUBLANES]
    key_p = lax.shift_left(key16, 16) | ((key16 >> 31) & 0xFFFF)
    key_lo = key_p - 2 ** 16

    def bisect(it, prefix):
        cand = prefix + lax.shift_left(jnp.int32(1), 16 - it)
        cand_f = _key_to_f32(cand)
        cnt = count(lambda sc: sc >= cand_f[None])
        return jnp.where(cnt >= ksel, cand, prefix)

    vstar = _key_to_f32(lax.fori_loop(0, 17, bisect, key_lo))
    need_f = (ksel - count(lambda sc: sc > vstar[None])).astype(F32)

    tri = jnp.where(lax.broadcasted_iota(I32, (tk, tk), 1) <= lax.broadcasted_iota(I32, (tk, tk), 0),
                    1.0, 0.0).astype(BF16)

    def select_pair(c2, run):
        chunks = (2 * c2, 2 * c2 + 1)
        scs = [score_scr[c] for c in chunks]
        eqs = [sc == vstar[0:1] for sc in scs]
        ranks = [jnp.dot(tri, jnp.where(eq, 1.0, 0.0).astype(BF16), preferred_element_type=F32) for eq in eqs]
        for c, sc, eq, rank in zip(chunks, scs, eqs, ranks):
            rank = rank + run
            tie_bias = jnp.where(rank <= need_f, jnp.where(c * tk + key_iota <= q_pos, 0.0, NEG), NEG)
            bias_scr[c] = jnp.where(sc > vstar[0:1], 0.0, jnp.where(eq, tie_bias, NEG))
            run = rank[tk - 1:tk, :]
        return run

    lax.fori_loop(0, n_pair, select_pair, jnp.zeros((1, tq), F32))

    heads_per_half = MXU_DIM // ATT_HD
    n_half = ATT_W // MXU_DIM
    row_head = lax.broadcasted_iota(I32, (MXU_DIM, tq), 0) // ATT_HD
    for h in range(ATT_HEADS):
        half, hh = divmod(h, heads_per_half)
        qh = aqt_ref[0, half * MXU_DIM:(half + 1) * MXU_DIM, :]
        qbd_scr[half, :, hh * tq:(hh + 1) * tq] = jnp.where(row_head == hh, qh, jnp.zeros_like(qh))
    m_scr[...] = jnp.full_like(m_scr, NEG)
    acc_scr[...] = jnp.zeros_like(acc_scr)

    def att_pair(c2, carry):
        s_all = {}
        for half in range(n_half):
            for sub in range(2):
                rows = pl.ds(pl.multiple_of((2 * c2 + sub) * tk, tk), tk)
                s_all[sub, half] = jnp.dot(ak_ref[0, rows, half * MXU_DIM:(half + 1) * MXU_DIM], qbd_scr[half],
                                           preferred_element_type=F32)
        for half in range(n_half):
            for hh in range(heads_per_half):
                h = half * heads_per_half + hh
                sl = slice(hh * tq, (hh + 1) * tq)
                s = [s_all[sub, half][:, sl] + bias_scr[2 * c2 + sub] for sub in range(2)]
                m_old = m_scr[h]
                m_new = jnp.maximum(m_old, jnp.max(jnp.maximum(s[0], s[1]), axis=0, keepdims=True))
                alpha = jnp.exp2(m_old - m_new)
                pv = alpha * acc_scr[h]
                for sub in range(2):
                    p = jnp.exp2(s[sub] - m_new).astype(BF16)
                    vt = avt_ref[0, 2 * c2 + sub, h * ATT_VROWS:(h + 1) * ATT_VROWS, :]
                    pv = pv + jnp.dot(vt, p, preferred_element_type=F32)
                acc_scr[h] = pv
                m_scr[h] = m_new
        return carry

    lax.fori_loop(0, n_pair, att_pair, 0)

    att_t = jnp.concatenate([acc_scr[h, :ATT_HD] * (1.0 / acc_scr[h, ATT_HD:ATT_HD + 1])
                             for h in range(ATT_HEADS)], axis=0)
    out_ref[0] = (att_t.T * ag_ref[0]).astype(BF16)


def _dsa_attention(aqt, ak, avt, iqt, ik, iwt, ag):
    b, s, _ = ak.shape
    tq, tk = DSA_TQ, DSA_TK
    nck = s // tk
    ksel = min(TOPK_MAX, s // 4)
    qt = lambda rows: pl.BlockSpec((1, rows, tq), lambda bi, i: (bi, 0, i))
    seq = lambda width: pl.BlockSpec((1, s, width), lambda bi, i: (bi, 0, 0))
    qn = pl.BlockSpec((1, tq, ATT_W), lambda bi, i: (bi, i, 0))
    return pl.pallas_call(
        functools.partial(_dsa_kernel, ksel=ksel),
        grid=(b, s // tq),
        in_specs=[qt(ATT_W), seq(ATT_W),
                  pl.BlockSpec((1, nck, ATT_HEADS * ATT_VROWS, tk), lambda bi, i: (bi, 0, 0, 0)),
                  qt(IDX_QW), seq(LANES), qt(SUBLANES), qn],
        out_specs=qn,
        out_shape=jax.ShapeDtypeStruct((b, s, ATT_W), BF16),
        scratch_shapes=[pltpu.VMEM((nck, tk, tq), F32), pltpu.VMEM((nck, tk, tq), BF16),
                        pltpu.VMEM((nck, tk, tq), F32),
                        pltpu.VMEM((ATT_W // MXU_DIM, MXU_DIM, (MXU_DIM // ATT_HD) * tq), BF16),
                        pltpu.VMEM((ATT_HEADS, 1, tq), F32),
                        pltpu.VMEM((ATT_HEADS, ATT_VROWS, tq), F32)],
        compiler_params=pltpu.CompilerParams(dimension_semantics=("arbitrary", "arbitrary"),
                                             vmem_limit_bytes=VMEM_LIMIT),
        name="dsa_attention",
    )(aqt, ak, avt, iqt, ik, iwt, ag)


def _gla_kernel(gqk_ref, gv_ref, glog_ref, gg_ref, nw_ref, out_ref, state_scr):
    c_len, sub = GLA_CHUNK, GLA_SUB
    nb = gqk_ref.shape[0]
    n_z = c_len * sub

    @pl.when(pl.program_id(0) == 0)
    def _():
        state_scr[...] = jnp.zeros_like(state_scr)

    t_i = lax.broadcasted_iota(I32, (c_len, GLA_QKW), 0)
    s_i = lax.broadcasted_iota(I32, (sub, GLA_QKW), 0)
    j_i = lax.broadcasted_iota(I32, (c_len, GLA_QKW), 1) % c_len
    tril = jnp.where(lax.broadcasted_iota(I32, (c_len, c_len), 0)
                     >= lax.broadcasted_iota(I32, (c_len, c_len), 1), 1.0, 0.0).astype(BF16)
    lane_head = lax.broadcasted_iota(I32, (c_len, GLA_QKW), 1) // GLA_DK

    def stack_heads(a):
        return jnp.concatenate([jnp.where(lane_head == h, a, 0.0) for h in range(GLA_HEADS)],
                               axis=0).astype(BF16)

    zr = lax.broadcasted_iota(I32, (n_z, GLA_QKW), 0)
    zc = lax.broadcasted_iota(I32, (n_z, GLA_QKW), 1)
    pick = (zc % c_len) == (zr // (sub * sub)) * sub + zr % sub
    hr = lax.broadcasted_iota(I32, (GLA_QKW, GLA_QKW), 0) // GLA_DK
    hc = lax.broadcasted_iota(I32, (GLA_QKW, GLA_QKW), 1) // c_len
    head_rep = jnp.where(hr == hc, 1.0, 0.0).astype(BF16)
    sr = lax.broadcasted_iota(I32, (c_len, n_z), 0)
    sc_ = lax.broadcasted_iota(I32, (c_len, n_z), 1)
    row_sel = jnp.where(sr == sc_ // sub, 1.0, 0.0).astype(BF16)
    vr = lax.broadcasted_iota(I32, (GLA_HEADS * c_len, GLA_W), 0) // c_len
    vc = lax.broadcasted_iota(I32, (GLA_HEADS * c_len, GLA_W), 1) // GLA_DV
    v_diag = vr == vc
    spans = [c_len >> (lv + 1) for lv in range((c_len // sub).bit_length() - 1)]

    units = range(nb * (gqk_ref.shape[1] // c_len))
    bat = lambda u: u % nb
    tsl = lambda u: slice((u // nb) * c_len, (u // nb + 1) * c_len)
    cur_state = {bi: state_scr[bi] for bi in range(nb)}
    q, k, vb, b, b_last, state_t, o, a, rep = ({} for _ in range(9))

    for u in units:
        gqk = gqk_ref[bat(u), tsl(u), :]
        q[u], k[u] = gqk[:, :GLA_QKW], gqk[:, GLA_QKW:]
        vb[u] = gv_ref[bat(u), tsl(u), :].astype(BF16)
        g = glog_ref[bat(u), tsl(u), :]
        g1 = g.astype(BF16)
        r1 = g - g1.astype(F32)
        g2 = r1.astype(BF16)
        g3 = (r1 - g2.astype(F32)).astype(BF16)
        b[u] = (jnp.dot(tril, g1, preferred_element_type=F32) + jnp.dot(tril, g2, preferred_element_type=F32)
                 + jnp.dot(tril, g3, preferred_element_type=F32))
        b_last[u] = b[u][c_len - 1:c_len, :]

    for u in units:
        state_t[u] = cur_state[bat(u)]
        o_stack = _nt_dot(stack_heads(q[u] * jnp.exp(b[u])), state_t[u].astype(BF16))
        o[u] = jnp.concatenate([o_stack[h * c_len:(h + 1) * c_len] for h in range(GLA_HEADS)], axis=1)

        k_bd = stack_heads(k[u] * jnp.exp(b_last[u] - b[u]))
        v_st = jnp.concatenate([vb[u][:, h * GLA_DV:(h + 1) * GLA_DV] for h in range(GLA_HEADS)], axis=0)
        cur_state[bat(u)] = jnp.exp(b_last[u]) * state_t[u] + _tn_dot(v_st, k_bd)

        for span in spans:
            blk = 2 * span
            ref = jnp.concatenate([jnp.broadcast_to(b[u][m * blk + span - 1:m * blk + span, :], (blk, GLA_QKW))
                                   for m in range(c_len // blk)], axis=0)
            upper = (t_i % blk) >= span
            q_l = q[u] * jnp.exp(jnp.where(upper, b[u] - ref, NEG))
            k_l = k[u] * jnp.exp(jnp.where(upper, NEG, ref - b[u]))
            a_l = _nt_dot(q_l.astype(BF16), stack_heads(k_l))
            if blk < c_len:
                a_l = jnp.where(t_i // blk == j_i // blk, a_l, 0.0)
            a[u] = a_l if u not in a else a[u] + a_l

    for bi in range(nb):
        state_scr[bi] = cur_state[bi]

    for u in units:
        zs = []
        for t in range(c_len):
            blk0 = (t // sub) * sub
            rs = slice(blk0, blk0 + sub)
            dec = jnp.exp(jnp.where(s_i <= t - blk0, b[u][t:t + 1, :] - b[u][rs], NEG))
            zs.append(q[u][t:t + 1, :] * k[u][rs] * dec)
        zmat = jnp.concatenate(zs, axis=0).astype(BF16)
        rep[u] = jnp.dot(zmat, head_rep, preferred_element_type=F32)

    for u in units:
        a_all = a[u] + jnp.dot(row_sel, jnp.where(pick, rep[u], 0.0).astype(BF16), preferred_element_type=F32)
        v_bd = jnp.where(v_diag, jnp.concatenate([vb[u]] * GLA_HEADS, axis=0), jnp.zeros((), BF16))
        o_u = o[u] + jnp.dot(a_all.astype(BF16), v_bd, preferred_element_type=F32)
        outs = []
        for h in range(GLA_HEADS):
            oh = o_u[:, h * GLA_DV:(h + 1) * GLA_DV]
            oh = oh * lax.rsqrt(jnp.mean(oh * oh, axis=-1, keepdims=True) + NORM_EPS)
            outs.append(oh * nw_ref[...])
        out_ref[bat(u), tsl(u), :] = (jnp.concatenate(outs, axis=1) * gg_ref[bat(u), tsl(u), :]).astype(BF16)


def _gla(gqk, gv, glog, gg, norm_w):
    b, s, _ = gqk.shape
    c_len = GLA_CHUNK
    step = min(GLA_STEP, s)
    blk = lambda width: pl.BlockSpec((b, step, width), lambda c: (0, c, 0))
    nw2 = norm_w.reshape(1, GLA_DV)
    return pl.pallas_call(
        _gla_kernel,
        grid=(s // step,),
        in_specs=[blk(2 * GLA_QKW), blk(GLA_W), blk(GLA_QKW), blk(GLA_W),
                  pl.BlockSpec(nw2.shape, lambda c: (0, 0))],
        out_specs=blk(GLA_W),
        out_shape=jax.ShapeDtypeStruct((b, s, GLA_W), BF16),
        scratch_shapes=[pltpu.VMEM((b, GLA_DV, GLA_QKW), F32)],
        compiler_params=pltpu.CompilerParams(dimension_semantics=("arbitrary",),
                                             vmem_limit_bytes=VMEM_LIMIT),
        name="gla_chunked",
    )(gqk, gv, glog, gg, nw2)


def _out_kernel(att_ref, gla_ref, wa_ref, wg_ref, x_ref, nw_ref, out_ref):
    mixed = jnp.dot(att_ref[...], wa_ref[...], preferred_element_type=F32) + \
        jnp.dot(gla_ref[...], wg_ref[...], preferred_element_type=F32)
    y = mixed * lax.rsqrt(jnp.mean(mixed * mixed, axis=-1, keepdims=True) + NORM_EPS)
    out_ref[...] = x_ref[...] + y * nw_ref[...]


def _out_project(att2, gla2, w_out, x2, post_norm_w):
    n, d = x2.shape
    tm = min(OUT_TM, n)
    wa, wg = w_out[:ATT_W].astype(BF16), w_out[ATT_W:].astype(BF16)
    row = lambda width: pl.BlockSpec((tm, width), lambda i: (i, 0))
    full = lambda a: pl.BlockSpec(a.shape, lambda i: (0, 0))
    return pl.pallas_call(
        _out_kernel,
        grid=(n // tm,),
        in_specs=[row(ATT_W), row(GLA_W), full(wa), full(wg), row(d), full(post_norm_w)],
        out_specs=row(d),
        out_shape=jax.ShapeDtypeStruct((n, d), F32),
        compiler_params=pltpu.CompilerParams(dimension_semantics=("arbitrary",),
                                             vmem_limit_bytes=VMEM_LIMIT),
        name="out_proj_postnorm",
    )(att2, gla2, wa, wg, x2, post_norm_w)


def kernel(x, positions, w_in, w_gla_gate_up, b_gla_gate, gla_norm_w, w_out, pre_norm_w, post_norm_w):
    b, s, d = x.shape
    n = b * s
    cos8, sin8 = _rope_tables(positions)
    aqt, ak, avt, iqt, ik, iwt, ag, gqk, gv, glog, gg = _project(x, pre_norm_w.reshape(1, d), cos8, sin8, w_in,
                                                                  w_gla_gate_up, b_gla_gate)
    att = _dsa_attention(aqt, ak, avt, iqt, ik, iwt, ag)
    gla = _gla(gqk, gv, glog, gg, gla_norm_w)
    out = _out_project(att.reshape(n, ATT_W), gla.reshape(n, GLA_W), w_out, x.reshape(n, d),
                       post_norm_w.reshape(1, d))
    return out.reshape(b, s, d)
```

```python
import functools

import jax
import jax.numpy as jnp
import numpy as np
from jax import lax
from jax.experimental import pallas as pl
from jax.experimental.pallas import tpu as pltpu

F32 = jnp.float32
BF16 = jnp.bfloat16
I32 = jnp.int32

ATT_HEADS = 8
ATT_HD = 64
ATT_W = ATT_HEADS * ATT_HD
IDX_HEADS = 4
IDX_HD = 64
IDX_QW = IDX_HEADS * IDX_HD
TOPK_MAX = 256
GLA_HEADS = 4
GLA_DK = 64
GLA_DV = 128
GLA_QKW = GLA_HEADS * GLA_DK
GLA_W = GLA_HEADS * GLA_DV
GLA_GATE_RANK = 16
GLA_TAU = 16.0
GLA_CHUNK = 64
GLA_SUB = 8
GLA_STEP = 256
ROPE_THETA = 500000.0
ROT_DIM = ATT_HD // 4
ROT_HALF = ROT_DIM // 2
NORM_EPS = 1e-6
NEG = float(np.float32(-1e30))
LOG2E = 1.4426950408889634

SUBLANES = 8
LANES = 128
MXU_DIM = 256
VMEM_LIMIT = 56 * 1024 * 1024

PROJ_TM = 512
OUT_TM = 1024
DSA_TQ = 512
DSA_TK = 256
COUNT_CHAINS = 4
ATT_VROWS = ATT_HD + 16
INT_MIN = -(2 ** 31)


def _nt_dot(a, b):
    return lax.dot_general(a, b, (((1,), (1,)), ((), ())), preferred_element_type=F32)


def _tn_dot(a, b):
    return lax.dot_general(a, b, (((0,), (0,)), ((), ())), preferred_element_type=F32)


def _rope_table_kernel(pos_ref, inv_ref, cos_ref, sin_ref):
    ang = pos_ref[...].astype(F32) * inv_ref[...]
    cos_ref[...] = jnp.cos(ang)
    sin_ref[...] = jnp.sin(ang)


def _rope_tables(positions):
    n = positions.size
    inv = ROPE_THETA ** (-jnp.arange(0, ROT_DIM, 2, dtype=F32) / ROT_DIM)
    return pl.pallas_call(
        _rope_table_kernel,
        out_shape=(jax.ShapeDtypeStruct((ROT_HALF, n), F32),) * 2,
        name="rope_tables",
    )(positions.reshape(1, n), inv.reshape(ROT_HALF, 1))


def _proj_kernel(x_ref, pnw_ref, cos_ref, sin_ref,
                 wt_aq, wt_ak, wt_av, wt_iq, wt_ik, wt_iw, w_ag, w_gqk, w_gv, w_gg,
                 wupt_hi, wupt_lo, bup_ref,
                 aqt_o, ak_o, avt_o, iqt_o, ik_o, iwt_o, ag_o, gqk_o, gv_o, glog_o, gg_o):
    x = x_ref[0]
    ms = jnp.mean(x * x, axis=-1, keepdims=True)
    h = (x * lax.rsqrt(ms + NORM_EPS)) * pnw_ref[...]
    hb = h.astype(BF16)
    cos8, sin8 = cos_ref[...], sin_ref[...]

    def mm(w_ref):
        return jnp.dot(hb, w_ref[...], preferred_element_type=F32)

    def mm_t(wt_ref):
        return _nt_dot(wt_ref[...], hb)

    def rope_t(yt):
        pieces = []
        for hd in range(yt.shape[0] // ATT_HD):
            rows = yt[hd * ATT_HD:(hd + 1) * ATT_HD]
            x1, x2 = rows[:ROT_HALF], rows[ROT_HALF:ROT_DIM]
            pieces += [x1 * cos8 - x2 * sin8, x2 * cos8 + x1 * sin8, rows[ROT_DIM:]]
        return jnp.concatenate(pieces, axis=0)

    aqt_o[0] = (rope_t(mm_t(wt_aq)) * (ATT_HD ** -0.5 * LOG2E)).astype(BF16)
    ak_o[0] = rope_t(mm_t(wt_ak)).T.astype(BF16)
    avt = mm_t(wt_av)
    ones_rows = jnp.ones((ATT_VROWS - ATT_HD, avt.shape[1]), F32)
    avt = jnp.concatenate([piece for hd in range(ATT_HEADS)
                           for piece in (avt[hd * ATT_HD:(hd + 1) * ATT_HD], ones_rows)], axis=0).astype(BF16)
    for c in range(avt_o.shape[1]):
        avt_o[0, c] = avt[:, c * DSA_TK:(c + 1) * DSA_TK]
    iqt_o[0] = rope_t(mm_t(wt_iq)).astype(BF16)
    ikt = mm_t(wt_ik)
    ik_o[0] = jnp.concatenate([rope_t(ikt[:IDX_HD]), jnp.zeros_like(ikt[IDX_HD:])], axis=0).T.astype(BF16)
    iwt_o[0] = mm_t(wt_iw) * (IDX_HEADS ** -0.5 * IDX_HD ** -0.5)
    a_gate = mm(w_ag)
    ag_o[0] = a_gate * jax.nn.sigmoid(a_gate)
    gqk = mm(w_gqk)
    lane = lax.broadcasted_iota(I32, gqk.shape, 1)
    gqk_o[0] = jnp.where(lane < GLA_QKW, gqk * (GLA_DK ** -0.5), gqk)
    gv_o[0] = mm(w_gv)
    ikt_hi = ikt.astype(BF16)
    ikt_lo = (ikt - ikt_hi.astype(F32)).astype(BF16)
    zt = (jnp.dot(wupt_hi[...], ikt_hi, preferred_element_type=F32)
          + jnp.dot(wupt_lo[...], ikt_hi, preferred_element_type=F32)
          + jnp.dot(wupt_hi[...], ikt_lo, preferred_element_type=F32))
    z = zt.T + bup_ref[...]
    glog_o[0] = (jnp.minimum(z, 0.0) - jnp.log1p(jnp.exp(-jnp.abs(z)))) * (1.0 / GLA_TAU)
    g_gate = mm(w_gg)
    gg_o[0] = g_gate * jax.nn.sigmoid(g_gate)


def _split_w_in(w_in):
    sizes = (ATT_W, ATT_W, ATT_W, IDX_QW, IDX_HD, IDX_HEADS, ATT_W,
             GLA_QKW, GLA_QKW, GLA_W, GLA_GATE_RANK, GLA_W)
    offs = np.cumsum((0,) + sizes)
    seg = [w_in[:, offs[i]:offs[i + 1]] for i in range(len(sizes))]
    a_q, a_k, a_v, i_q, i_k, i_w, a_g, g_q, g_k, g_v, g_lr, g_g = seg
    d = w_in.shape[0]
    pad = lambda w, n: jnp.concatenate([w, jnp.zeros((d, n - w.shape[1]), w.dtype)], axis=1)
    ik_glr = pad(jnp.concatenate([i_k, g_lr], axis=1), LANES)
    transposed = [a_q.T, a_k.T, a_v.T, i_q.T, ik_glr.T, pad(i_w, SUBLANES).T]
    natural = [a_g, jnp.concatenate([g_q, g_k], axis=1), g_v, g_g]
    return [w.astype(BF16) for w in transposed], [w.astype(BF16) for w in natural]


def _project(x, pre_norm_w, cos8, sin8, w_in, w_up, b_up):
    b, s, d = x.shape
    wupt = jnp.concatenate([jnp.zeros((GLA_QKW, IDX_HD), F32), w_up.T,
                            jnp.zeros((GLA_QKW, LANES - IDX_HD - GLA_GATE_RANK), F32)], axis=1)
    wupt_hi = wupt.astype(BF16)
    wupt_lo = (wupt - wupt_hi.astype(F32)).astype(BF16)
    gate_ops = [wupt_hi, wupt_lo, b_up.reshape(1, GLA_QKW)]
    tm = min(PROJ_TM, s)
    nt = s // tm
    cpt = tm // DSA_TK
    wts, wns = _split_w_in(w_in)
    full = lambda a: pl.BlockSpec(a.shape, lambda bi, i: (0,) * a.ndim)
    nat = lambda width: pl.BlockSpec((1, tm, width), lambda bi, i: (bi, i, 0))
    tr = lambda rows: pl.BlockSpec((1, rows, tm), lambda bi, i: (bi, 0, i))
    tab = pl.BlockSpec((ROT_HALF, tm), lambda bi, i: (0, bi * nt + i))
    vrows = ATT_HEADS * ATT_VROWS
    out_specs = [tr(ATT_W), nat(ATT_W), pl.BlockSpec((1, cpt, vrows, DSA_TK), lambda bi, i: (bi, i, 0, 0)),
                 tr(IDX_QW), nat(LANES), tr(SUBLANES),
                 nat(ATT_W), nat(2 * GLA_QKW), nat(GLA_W), nat(GLA_QKW), nat(GLA_W)]
    sds = jax.ShapeDtypeStruct
    out_shape = [sds((b, ATT_W, s), BF16), sds((b, s, ATT_W), BF16), sds((b, s // DSA_TK, vrows, DSA_TK), BF16),
                 sds((b, IDX_QW, s), BF16), sds((b, s, LANES), BF16), sds((b, SUBLANES, s), F32),
                 sds((b, s, ATT_W), F32), sds((b, s, 2 * GLA_QKW), F32), sds((b, s, GLA_W), F32),
                 sds((b, s, GLA_QKW), F32), sds((b, s, GLA_W), F32)]
    return pl.pallas_call(
        _proj_kernel,
        grid=(b, nt),
        in_specs=[nat(d), full(pre_norm_w), tab, tab] + [full(w) for w in wts + wns + gate_ops],
        out_specs=out_specs,
        out_shape=out_shape,
        compiler_params=pltpu.CompilerParams(dimension_semantics=("arbitrary", "arbitrary"),
                                             vmem_limit_bytes=VMEM_LIMIT),
        name="prenorm_in_proj",
    )(x, pre_norm_w, cos8, sin8, *wts, *wns, *gate_ops)


def _key_to_f32(key):
    return pltpu.bitcast(key ^ ((key >> 31) & 0x7FFFFFFF), F32)


def _dsa_kernel(aqt_ref, ak_ref, avt_ref, iqt_ref, ik_ref, iwt_ref, ag_ref, out_ref,
                score_scr, coarse_scr, bias_scr, qbd_scr, m_scr, acc_scr, *, ksel):
    tq, tk = DSA_TQ, DSA_TK
    i = pl.program_id(1)
    nck = (i * tq + tq + tk - 1) // tk
    groups = tk // SUBLANES

    key_iota = lax.broadcasted_iota(I32, (tk, tq), 0)
    q_pos = i * tq + lax.broadcasted_iota(I32, (tk, tq), 1)

    iqt = iqt_ref[0]
    q_cat = jnp.concatenate([iqt[h * IDX_HD:(h + 1) * IDX_HD] for h in range(IDX_HEADS)], axis=1)
    q_cat = jnp.concatenate([q_cat, jnp.zeros((LANES - IDX_HD, IDX_HEADS * tq), BF16)], axis=0)
    iwt = iwt_ref[0]
    w_rows = [iwt[h:h + 1, :] for h in range(IDX_HEADS)]

    n_pair = (nck + 1) // 2

    def score_pair(c2, carry):
        chunks = (2 * c2, 2 * c2 + 1)
        rels = [jnp.dot(ik_ref[0, pl.ds(pl.multiple_of(c * tk, tk), tk), :], q_cat,
                        preferred_element_type=F32) for c in chunks]
        for c, rel in zip(chunks, rels):
            sc = w_rows[0] * jnp.maximum(rel[:, 0:tq], 0.0)
            for h in range(1, IDX_HEADS):
                sc = sc + w_rows[h] * jnp.maximum(rel[:, h * tq:(h + 1) * tq], 0.0)
            sc = jnp.where(c * tk + key_iota <= q_pos, sc, NEG)
            score_scr[c] = sc
            coarse_scr[c] = sc.astype(BF16)
        return carry

    lax.fori_loop(0, n_pair, score_pair, 0)

    def count(pred):
        def body(c2, acc):
            sc = score_scr[pl.ds(2 * c2, 2)].reshape(COUNT_CHAINS, 2 * groups // COUNT_CHAINS, SUBLANES, tq)
            return acc + jnp.sum(jnp.where(pred(sc), 1, 0), axis=1)
        acc = lax.fori_loop(0, n_pair, body, jnp.zeros((COUNT_CHAINS, SUBLANES, tq), I32))
        return jnp.sum(acc.reshape(COUNT_CHAINS * SUBLANES, tq), axis=0, keepdims=True)

    def count_coarse(cand_b):
        packed_rows = 2 * SUBLANES
        def body(c2, acc):
            hi = coarse_scr[pl.ds(2 * c2, 2)].reshape(COUNT_CHAINS, 2 * tk // (COUNT_CHAINS * packed_rows),
                                                      packed_rows, tq)
            for g in range(hi.shape[1]):
                acc = acc + jnp.where(hi[:, g] >= cand_b[None], jnp.ones((), BF16), jnp.zeros((), BF16))
            return acc
        acc = lax.fori_loop(0, n_pair, body, jnp.zeros((COUNT_CHAINS, packed_rows, tq), BF16))
        return jnp.sum(acc.astype(F32).reshape(COUNT_CHAINS * packed_rows, tq), axis=0, keepdims=True)

    def bisect_coarse(it, prefix):
        cand = prefix + lax.shift_left(jnp.int32(1), 15 - it)
        pattern = cand ^ ((cand >> 31) & 0x7FFF)
        cand_b = pltpu.bitcast(lax.shift_left(pattern, 16), F32).astype(BF16)
        cnt = count_coarse(cand_b)
        return jnp.where(cnt >= ksel, cand, prefix)

    assert (tk // (COUNT_CHAINS * SUBLANES)) * (coarse_scr.shape[0] // 2) <= 2 ** 8, "bf16 counts must stay exact"
    key16 = lax.fori_loop(0, 16, bisect_coarse, jnp.full((2 * SUBLANES, tq), -(2 ** 15), I32))[:SUBLANES]
    key_p = lax.shift_left(key16, 16) | ((key16 >> 31) & 0xFFFF)
    key_lo = key_p - 2 ** 16

    def bisect(it, prefix):
        cand = prefix + lax.shift_left(jnp.int32(1), 16 - it)
        cand_f = _key_to_f32(cand)
        cnt = count(lambda sc: sc >= cand_f[None])
        return jnp.where(cnt >= ksel, cand, prefix)

    vstar = _key_to_f32(lax.fori_loop(0, 17, bisect, key_lo))
    need_f = (ksel - count(lambda sc: sc > vstar[None])).astype(F32)

    tri = jnp.where(lax.broadcasted_iota(I32, (tk, tk), 1) <= lax.broadcasted_iota(I32, (tk, tk), 0),
                    1.0, 0.0).astype(BF16)

    def select_pair(c2, run):
        chunks = (2 * c2, 2 * c2 + 1)
        scs = [score_scr[c] for c in chunks]
        eqs = [sc == vstar[0:1] for sc in scs]
        ranks = [jnp.dot(tri, jnp.where(eq, 1.0, 0.0).astype(BF16), preferred_element_type=F32) for eq in eqs]
        for c, sc, eq, rank in zip(chunks, scs, eqs, ranks):
            rank = rank + run
            tie_bias = jnp.where(rank <= need_f, jnp.where(c * tk + key_iota <= q_pos, 0.0, NEG), NEG)
            bias_scr[c] = jnp.where(sc > vstar[0:1], 0.0, jnp.where(eq, tie_bias, NEG))
            run = rank[tk - 1:tk, :]
        return run

    lax.fori_loop(0, n_pair, select_pair, jnp.zeros((1, tq), F32))

    heads_per_half = MXU_DIM // ATT_HD
    n_half = ATT_W // MXU_DIM
    row_head = lax.broadcasted_iota(I32, (MXU_DIM, tq), 0) // ATT_HD
    for h in range(ATT_HEADS):
        half, hh = divmod(h, heads_per_half)
        qh = aqt_ref[0, half * MXU_DIM:(half + 1) * MXU_DIM, :]
        qbd_scr[half, :, hh * tq:(hh + 1) * tq] = jnp.where(row_head == hh, qh, jnp.zeros_like(qh))
    m_scr[...] = jnp.full_like(m_scr, NEG)
    acc_scr[...] = jnp.zeros_like(acc_scr)

    def att_pair(c2, carry):
        s_all = {}
        for half in range(n_half):
            for sub in range(2):
                rows = pl.ds(pl.multiple_of((2 * c2 + sub) * tk, tk), tk)
                s_all[sub, half] = jnp.dot(ak_ref[0, rows, half * MXU_DIM:(half + 1) * MXU_DIM], qbd_scr[half],
                                           preferred_element_type=F32)
        for half in range(n_half):
            for hh in range(heads_per_half):
                h = half * heads_per_half + hh
                sl = slice(hh * tq, (hh + 1) * tq)
                s = [s_all[sub, half][:, sl] + bias_scr[2 * c2 + sub] for sub in range(2)]
                m_old = m_scr[h]
                m_new = jnp.maximum(m_old, jnp.max(jnp.maximum(s[0], s[1]), axis=0, keepdims=True))
                alpha = jnp.exp2(m_old - m_new)
                pv = alpha * acc_scr[h]
                for sub in range(2):
                    p = jnp.exp2(s[sub] - m_new).astype(BF16)
                    vt = avt_ref[0, 2 * c2 + sub, h * ATT_VROWS:(h + 1) * ATT_VROWS, :]
                    pv = pv + jnp.dot(vt, p, preferred_element_type=F32)
                acc_scr[h] = pv
                m_scr[h] = m_new
        return carry

    lax.fori_loop(0, n_pair, att_pair, 0)

    att_t = jnp.concatenate([acc_scr[h, :ATT_HD] * (1.0 / acc_scr[h, ATT_HD:ATT_HD + 1])
                             for h in range(ATT_HEADS)], axis=0)
    out_ref[0] = (att_t.T * ag_ref[0]).astype(BF16)


def _dsa_attention(aqt, ak, avt, iqt, ik, iwt, ag):
    b, s, _ = ak.shape
    tq, tk = DSA_TQ, DSA_TK
    nck = s // tk
    ksel = min(TOPK_MAX, s // 4)
    qt = lambda rows: pl.BlockSpec((1, rows, tq), lambda bi, i: (bi, 0, i))
    seq = lambda width: pl.BlockSpec((1, s, width), lambda bi, i: (bi, 0, 0))
    qn = pl.BlockSpec((1, tq, ATT_W), lambda bi, i: (bi, i, 0))
    return pl.pallas_call(
        functools.partial(_dsa_kernel, ksel=ksel),
        grid=(b, s // tq),
        in_specs=[qt(ATT_W), seq(ATT_W),
                  pl.BlockSpec((1, nck, ATT_HEADS * ATT_VROWS, tk), lambda bi, i: (bi, 0, 0, 0)),
                  qt(IDX_QW), seq(LANES), qt(SUBLANES), qn],
        out_specs=qn,
        out_shape=jax.ShapeDtypeStruct((b, s, ATT_W), BF16),
        scratch_shapes=[pltpu.VMEM((nck, tk, tq), F32), pltpu.VMEM((nck, tk, tq), BF16),
                        pltpu.VMEM((nck, tk, tq), F32),
                        pltpu.VMEM((ATT_W // MXU_DIM, MXU_DIM, (MXU_DIM // ATT_HD) * tq), BF16),
                        pltpu.VMEM((ATT_HEADS, 1, tq), F32),
                        pltpu.VMEM((ATT_HEADS, ATT_VROWS, tq), F32)],
        compiler_params=pltpu.CompilerParams(dimension_semantics=("arbitrary", "arbitrary"),
                                             vmem_limit_bytes=VMEM_LIMIT),
        name="dsa_attention",
    )(aqt, ak, avt, iqt, ik, iwt, ag)


def _gla_kernel(gqk_ref, gv_ref, glog_ref, gg_ref, nw_ref, out_ref, state_scr):
    c_len, sub = GLA_CHUNK, GLA_SUB
    nb = gqk_ref.shape[0]
    n_z = c_len * sub

    @pl.when(pl.program_id(0) == 0)
    def _():
        state_scr[...] = jnp.zeros_like(state_scr)

    t_i = lax.broadcasted_iota(I32, (c_len, GLA_QKW), 0)
    s_i = lax.broadcasted_iota(I32, (sub, GLA_QKW), 0)
    j_i = lax.broadcasted_iota(I32, (c_len, GLA_QKW), 1) % c_len
    tril = jnp.where(lax.broadcasted_iota(I32, (c_len, c_len), 0)
                     >= lax.broadcasted_iota(I32, (c_len, c_len), 1), 1.0, 0.0).astype(BF16)
    lane_head = lax.broadcasted_iota(I32, (c_len, GLA_QKW), 1) // GLA_DK

    def stack_heads(a):
        return jnp.concatenate([jnp.where(lane_head == h, a, 0.0) for h in range(GLA_HEADS)],
                               axis=0).astype(BF16)

    zr = lax.broadcasted_iota(I32, (n_z, GLA_QKW), 0)
    zc = lax.broadcasted_iota(I32, (n_z, GLA_QKW), 1)
    pick = (zc % c_len) == (zr // (sub * sub)) * sub + zr % sub
    hr = lax.broadcasted_iota(I32, (GLA_QKW, GLA_QKW), 0) // GLA_DK
    hc = lax.broadcasted_iota(I32, (GLA_QKW, GLA_QKW), 1) // c_len
    head_rep = jnp.where(hr == hc, 1.0, 0.0).astype(BF16)
    sr = lax.broadcasted_iota(I32, (c_len, n_z), 0)
    sc_ = lax.broadcasted_iota(I32, (c_len, n_z), 1)
    row_sel = jnp.where(sr == sc_ // sub, 1.0, 0.0).astype(BF16)
    vr = lax.broadcasted_iota(I32, (GLA_HEADS * c_len, GLA_W), 0) // c_len
    vc = lax.broadcasted_iota(I32, (GLA_HEADS * c_len, GLA_W), 1) // GLA_DV
    v_diag = vr == vc
    spans = [c_len >> (lv + 1) for lv in range((c_len // sub).bit_length() - 1)]

    units = range(nb * (gqk_ref.shape[1] // c_len))
    bat = lambda u: u % nb
    tsl = lambda u: slice((u // nb) * c_len, (u // nb + 1) * c_len)
    cur_state = {bi: state_scr[bi] for bi in range(nb)}
    q, k, vb, b, b_last, state_t, o, a, rep = ({} for _ in range(9))

    for u in units:
        gqk = gqk_ref[bat(u), tsl(u), :]
        q[u], k[u] = gqk[:, :GLA_QKW], gqk[:, GLA_QKW:]
        vb[u] = gv_ref[bat(u), tsl(u), :].astype(BF16)
        g = glog_ref[bat(u), tsl(u), :]
        g1 = g.astype(BF16)
        r1 = g - g1.astype(F32)
        g2 = r1.astype(BF16)
        g3 = (r1 - g2.astype(F32)).astype(BF16)
        b[u] = (jnp.dot(tril, g1, preferred_element_type=F32) + jnp.dot(tril, g2, preferred_element_type=F32)
                 + jnp.dot(tril, g3, preferred_element_type=F32))
        b_last[u] = b[u][c_len - 1:c_len, :]

    for u in units:
        state_t[u] = cur_state[bat(u)]
        o_stack = _nt_dot(stack_heads(q[u] * jnp.exp(b[u])), state_t[u].astype(BF16))
        o[u] = jnp.concatenate([o_stack[h * c_len:(h + 1) * c_len] for h in range(GLA_HEADS)], axis=1)

        k_bd = stack_heads(k[u] * jnp.exp(b_last[u] - b[u]))
        v_st = jnp.concatenate([vb[u][:, h * GLA_DV:(h + 1) * GLA_DV] for h in range(GLA_HEADS)], axis=0)
        cur_state[bat(u)] = jnp.exp(b_last[u]) * state_t[u] + _tn_dot(v_st, k_bd)

        for span in spans:
            blk = 2 * span
            ref = jnp.concatenate([jnp.broadcast_to(b[u][m * blk + span - 1:m * blk + span, :], (blk, GLA_QKW))
                                   for m in range(c_len // blk)], axis=0)
            upper = (t_i % blk) >= span
            q_l = q[u] * jnp.exp(jnp.where(upper, b[u] - ref, NEG))
            k_l = k[u] * jnp.exp(jnp.where(upper, NEG, ref - b[u]))
            a_l = _nt_dot(q_l.astype(BF16), stack_heads(k_l))
            if blk < c_len:
                a_l = jnp.where(t_i // blk == j_i // blk, a_l, 0.0)
            a[u] = a_l if u not in a else a[u] + a_l

    for bi in range(nb):
        state_scr[bi] = cur_state[bi]

    for u in units:
        zs = []
        for t in range(c_len):
            blk0 = (t // sub) * sub
            rs = slice(blk0, blk0 + sub)
            dec = jnp.exp(jnp.where(s_i <= t - blk0, b[u][t:t + 1, :] - b[u][rs], NEG))
            zs.append(q[u][t:t + 1, :] * k[u][rs] * dec)
        zmat = jnp.concatenate(zs, axis=0).astype(BF16)
        rep[u] = jnp.dot(zmat, head_rep, preferred_element_type=F32)

    for u in units:
        a_all = a[u] + jnp.dot(row_sel, jnp.where(pick, rep[u], 0.0).astype(BF16), preferred_element_type=F32)
        v_bd = jnp.where(v_diag, jnp.concatenate([vb[u]] * GLA_HEADS, axis=0), jnp.zeros((), BF16))
        o_u = o[u] + jnp.dot(a_all.astype(BF16), v_bd, preferred_element_type=F32)
        outs = []
        for h in range(GLA_HEADS):
            oh = o_u[:, h * GLA_DV:(h + 1) * GLA_DV]
            oh = oh * lax.rsqrt(jnp.mean(oh * oh, axis=-1, keepdims=True) + NORM_EPS)
            outs.append(oh * nw_ref[...])
        out_ref[bat(u), tsl(u), :] = (jnp.concatenate(outs, axis=1) * gg_ref[bat(u), tsl(u), :]).astype(BF16)


def _gla(gqk, gv, glog, gg, norm_w):
    b, s, _ = gqk.shape
    c_len = GLA_CHUNK
    step = min(GLA_STEP, s)
    blk = lambda width: pl.BlockSpec((b, step, width), lambda c: (0, c, 0))
    nw2 = norm_w.reshape(1, GLA_DV)
    return pl.pallas_call(
        _gla_kernel,
        grid=(s // step,),
        in_specs=[blk(2 * GLA_QKW), blk(GLA_W), blk(GLA_QKW), blk(GLA_W),
                  pl.BlockSpec(nw2.shape, lambda c: (0, 0))],
        out_specs=blk(GLA_W),
        out_shape=jax.ShapeDtypeStruct((b, s, GLA_W), BF16),
        scratch_shapes=[pltpu.VMEM((b, GLA_DV, GLA_QKW), F32)],
        compiler_params=pltpu.CompilerParams(dimension_semantics=("arbitrary",),
                                             vmem_limit_bytes=VMEM_LIMIT),
        name="gla_chunked",
    )(gqk, gv, glog, gg, nw2)


def _out_kernel(att_ref, gla_ref, wa_ref, wg_ref, x_ref, nw_ref, out_ref):
    mixed = jnp.dot(att_ref[...], wa_ref[...], preferred_element_type=F32) + \
        jnp.dot(gla_ref[...], wg_ref[...], preferred_element_type=F32)
    y = mixed * lax.rsqrt(jnp.mean(mixed * mixed, axis=-1, keepdims=True) + NORM_EPS)
    out_ref[...] = x_ref[...] + y * nw_ref[...]


def _out_project(att2, gla2, w_out, x2, post_norm_w):
    n, d = x2.shape
    tm = min(OUT_TM, n)
    wa, wg = w_out[:ATT_W].astype(BF16), w_out[ATT_W:].astype(BF16)
    row = lambda width: pl.BlockSpec((tm, width), lambda i: (i, 0))
    full = lambda a: pl.BlockSpec(a.shape, lambda i: (0, 0))
    return pl.pallas_call(
        _out_kernel,
        grid=(n // tm,),
        in_specs=[row(ATT_W), row(GLA_W), full(wa), full(wg), row(d), full(post_norm_w)],
        out_specs=row(d),
        out_shape=jax.ShapeDtypeStruct((n, d), F32),
        compiler_params=pltpu.CompilerParams(dimension_semantics=("arbitrary",),
                                             vmem_limit_bytes=VMEM_LIMIT),
        name="out_proj_postnorm",
    )(att2, gla2, wa, wg, x2, post_norm_w)


def kernel(x, positions, w_in, w_gla_gate_up, b_gla_gate, gla_norm_w, w_out, pre_norm_w, post_norm_w):
    b, s, d = x.shape
    n = b * s
    cos8, sin8 = _rope_tables(positions)
    aqt, ak, avt, iqt, ik, iwt, ag, gqk, gv, glog, gg = _project(x, pre_norm_w.reshape(1, d), cos8, sin8, w_in,
                                                                  w_gla_gate_up, b_gla_gate)
    att = _dsa_attention(aqt, ak, avt, iqt, ik, iwt, ag)
    gla = _gla(gqk, gv, glog, gg, gla_norm_w)
    out = _out_project(att.reshape(n, ATT_W), gla.reshape(n, GLA_W), w_out, x.reshape(n, d),
                       post_norm_w.reshape(1, d))
    return out.reshape(b, s, d)
```

```python
import functools

import jax
import jax.numpy as jnp
import numpy as np
from jax import lax
from jax.experimental import pallas as pl
from jax.experimental.pallas import tpu as pltpu

F32 = jnp.float32
BF16 = jnp.bfloat16
I32 = jnp.int32

ATT_HEADS = 8
ATT_HD = 64
ATT_W = ATT_HEADS * ATT_HD
IDX_HEADS = 4
IDX_HD = 64
IDX_QW = IDX_HEADS * IDX_HD
TOPK_MAX = 256
GLA_HEADS = 4
GLA_DK = 64
GLA_DV = 128
GLA_QKW = GLA_HEADS * GLA_DK
GLA_W = GLA_HEADS * GLA_DV
GLA_GATE_RANK = 16
GLA_TAU = 16.0
GLA_CHUNK = 64
GLA_SUB = 8
GLA_STEP = 256
ROPE_THETA = 500000.0
ROT_DIM = ATT_HD // 4
ROT_HALF = ROT_DIM // 2
NORM_EPS = 1e-6
NEG = float(np.float32(-1e30))
LOG2E = 1.4426950408889634

SUBLANES = 8
LANES = 128
MXU_DIM = 256
VMEM_LIMIT = 56 * 1024 * 1024

PROJ_TM = 512
OUT_TM = 1024
DSA_TQ = 512
DSA_TK = 256
COUNT_CHAINS = 4
ATT_VROWS = ATT_HD + 16
INT_MIN = -(2 ** 31)


def _nt_dot(a, b):
    return lax.dot_general(a, b, (((1,), (1,)), ((), ())), preferred_element_type=F32)


def _tn_dot(a, b):
    return lax.dot_general(a, b, (((0,), (0,)), ((), ())), preferred_element_type=F32)


def _rope_table_kernel(pos_ref, inv_ref, cos_ref, sin_ref):
    ang = pos_ref[...].astype(F32) * inv_ref[...]
    cos_ref[...] = jnp.cos(ang)
    sin_ref[...] = jnp.sin(ang)


def _rope_tables(positions):
    n = positions.size
    inv = ROPE_THETA ** (-jnp.arange(0, ROT_DIM, 2, dtype=F32) / ROT_DIM)
    return pl.pallas_call(
        _rope_table_kernel,
        out_shape=(jax.ShapeDtypeStruct((ROT_HALF, n), F32),) * 2,
        name="rope_tables",
    )(positions.reshape(1, n), inv.reshape(ROT_HALF, 1))


def _proj_kernel(x_ref, pnw_ref, cos_ref, sin_ref,
                 wt_all, w_ag, w_gqk, w_gv, w_gg,
                 wupt_hi, wupt_lo, bup_ref,
                 aqt_o, ak_o, avt_o, iqt_o, ik_o, iwt_o, ag_o, gqk_o, gv_o, glog_o, gg_o):
    x = x_ref[0]
    ms = jnp.mean(x * x, axis=-1, keepdims=True)
    h = (x * lax.rsqrt(ms + NORM_EPS)) * pnw_ref[...]
    hb = h.astype(BF16)
    cos8, sin8 = cos_ref[...], sin_ref[...]

    def mm(w_ref):
        return jnp.dot(hb, w_ref[...], preferred_element_type=F32)

    def rope_t(yt):
        pieces = []
        for hd in range(yt.shape[0] // ATT_HD):
            rows = yt[hd * ATT_HD:(hd + 1) * ATT_HD]
            x1, x2 = rows[:ROT_HALF], rows[ROT_HALF:ROT_DIM]
            pieces += [x1 * cos8 - x2 * sin8, x2 * cos8 + x1 * sin8, rows[ROT_DIM:]]
        return jnp.concatenate(pieces, axis=0)

    yt = _nt_dot(wt_all[...], hb)
    ikt, o = yt[:LANES], LANES
    aq_t, o = yt[o:o + ATT_W], o + ATT_W
    ak_t, o = yt[o:o + ATT_W], o + ATT_W
    iq_t, o = yt[o:o + IDX_QW], o + IDX_QW
    av_t = yt[o:o + ATT_W]
    iw_row = IDX_HD + GLA_GATE_RANK
    iwt_o[0] = ikt[iw_row:iw_row + SUBLANES] * (IDX_HEADS ** -0.5 * IDX_HD ** -0.5)
    aqt_o[0] = (rope_t(aq_t) * (ATT_HD ** -0.5 * LOG2E)).astype(BF16)
    ik_o[0] = jnp.concatenate([rope_t(ikt[:IDX_HD]), jnp.zeros_like(ikt[IDX_HD:])], axis=0).T.astype(BF16)
    ak_o[0] = rope_t(ak_t).T.astype(BF16)
    ikt_hi = ikt.astype(BF16)
    ikt_lo = (ikt - ikt_hi.astype(F32)).astype(BF16)
    zt = (jnp.dot(wupt_hi[...], ikt_hi, preferred_element_type=F32)
          + jnp.dot(wupt_lo[...], ikt_hi, preferred_element_type=F32)
          + jnp.dot(wupt_hi[...], ikt_lo, preferred_element_type=F32))
    a_gate = mm(w_ag)
    z = zt.T + bup_ref[...]
    glog_o[0] = (jnp.minimum(z, 0.0) - jnp.log1p(jnp.exp(-jnp.abs(z)))) * (1.0 / GLA_TAU)
    g_gate = mm(w_gg)
    ag_o[0] = a_gate * jax.nn.sigmoid(a_gate)
    iqt_o[0] = rope_t(iq_t).astype(BF16)
    gg_o[0] = g_gate * jax.nn.sigmoid(g_gate)
    gqk = mm(w_gqk)
    lane = lax.broadcasted_iota(I32, gqk.shape, 1)
    gqk_o[0] = jnp.where(lane < GLA_QKW, gqk * (GLA_DK ** -0.5), gqk)
    avt = av_t
    ones_rows = jnp.ones((ATT_VROWS - ATT_HD, avt.shape[1]), F32)
    avt = jnp.concatenate([piece for hd in range(ATT_HEADS)
                           for piece in (avt[hd * ATT_HD:(hd + 1) * ATT_HD], ones_rows)], axis=0).astype(BF16)
    for c in range(avt_o.shape[1]):
        avt_o[0, c] = avt[:, c * DSA_TK:(c + 1) * DSA_TK]
    gv_o[0] = mm(w_gv)


def _split_w_in(w_in):
    sizes = (ATT_W, ATT_W, ATT_W, IDX_QW, IDX_HD, IDX_HEADS, ATT_W,
             GLA_QKW, GLA_QKW, GLA_W, GLA_GATE_RANK, GLA_W)
    offs = np.cumsum((0,) + sizes)
    seg = [w_in[:, offs[i]:offs[i + 1]] for i in range(len(sizes))]
    a_q, a_k, a_v, i_q, i_k, i_w, a_g, g_q, g_k, g_v, g_lr, g_g = seg
    d = w_in.shape[0]
    pad = lambda w, n: jnp.concatenate([w, jnp.zeros((d, n - w.shape[1]), w.dtype)], axis=1)
    small = pad(jnp.concatenate([i_k, g_lr, i_w], axis=1), LANES)
    transposed = jnp.concatenate([small, a_q, a_k, i_q, a_v], axis=1).T
    natural = [a_g, jnp.concatenate([g_q, g_k], axis=1), g_v, g_g]
    return [transposed.astype(BF16)], [w.astype(BF16) for w in natural]


def _project(x, pre_norm_w, cos8, sin8, w_in, w_up, b_up):
    b, s, d = x.shape
    wupt = jnp.concatenate([jnp.zeros((GLA_QKW, IDX_HD), F32), w_up.T,
                            jnp.zeros((GLA_QKW, LANES - IDX_HD - GLA_GATE_RANK), F32)], axis=1)
    wupt_hi = wupt.astype(BF16)
    wupt_lo = (wupt - wupt_hi.astype(F32)).astype(BF16)
    gate_ops = [wupt_hi, wupt_lo, b_up.reshape(1, GLA_QKW)]
    tm = min(PROJ_TM, s)
    nt = s // tm
    cpt = tm // DSA_TK
    wts, wns = _split_w_in(w_in)
    full = lambda a: pl.BlockSpec(a.shape, lambda bi, i: (0,) * a.ndim)
    nat = lambda width: pl.BlockSpec((1, tm, width), lambda bi, i: (bi, i, 0))
    tr = lambda rows: pl.BlockSpec((1, rows, tm), lambda bi, i: (bi, 0, i))
    tab = pl.BlockSpec((ROT_HALF, tm), lambda bi, i: (0, bi * nt + i))
    vrows = ATT_HEADS * ATT_VROWS
    out_specs = [tr(ATT_W), nat(ATT_W), pl.BlockSpec((1, cpt, vrows, DSA_TK), lambda bi, i: (bi, i, 0, 0)),
                 tr(IDX_QW), nat(LANES), tr(SUBLANES),
                 nat(ATT_W), nat(2 * GLA_QKW), nat(GLA_W), nat(GLA_QKW), nat(GLA_W)]
    sds = jax.ShapeDtypeStruct
    out_shape = [sds((b, ATT_W, s), BF16), sds((b, s, ATT_W), BF16), sds((b, s // DSA_TK, vrows, DSA_TK), BF16),
                 sds((b, IDX_QW, s), BF16), sds((b, s, LANES), BF16), sds((b, SUBLANES, s), F32),
                 sds((b, s, ATT_W), F32), sds((b, s, 2 * GLA_QKW), F32), sds((b, s, GLA_W), F32),
                 sds((b, s, GLA_QKW), F32), sds((b, s, GLA_W), F32)]
    return pl.pallas_call(
        _proj_kernel,
        grid=(b, nt),
        in_specs=[nat(d), full(pre_norm_w), tab, tab] + [full(w) for w in wts + wns + gate_ops],
        out_specs=out_specs,
        out_shape=out_shape,
        compiler_params=pltpu.CompilerParams(dimension_semantics=("arbitrary", "arbitrary"),
                                             vmem_limit_bytes=VMEM_LIMIT),
        name="prenorm_in_proj",
    )(x, pre_norm_w, cos8, sin8, *wts, *wns, *gate_ops)


def _key_to_f32(key):
    return pltpu.bitcast(key ^ ((key >> 31) & 0x7FFFFFFF), F32)


def _dsa_kernel(aqt_ref, ak_ref, avt_ref, iqt_ref, ik_ref, iwt_ref, ag_ref, out_ref,
                score_scr, coarse_scr, bias_scr, qbd_scr, m_scr, acc_scr, *, ksel):
    tq, tk = DSA_TQ, DSA_TK
    i = pl.program_id(1)
    nck = (i * tq + tq + tk - 1) // tk
    groups = tk // SUBLANES

    key_iota = lax.broadcasted_iota(I32, (tk, tq), 0)
    q_pos = i * tq + lax.broadcasted_iota(I32, (tk, tq), 1)

    iqt = iqt_ref[0]
    q_cat = jnp.concatenate([iqt[h * IDX_HD:(h + 1) * IDX_HD] for h in range(IDX_HEADS)], axis=1)
    q_cat = jnp.concatenate([q_cat, jnp.zeros((LANES - IDX_HD, IDX_HEADS * tq), BF16)], axis=0)
    iwt = iwt_ref[0]
    w_rows = [iwt[h:h + 1, :] for h in range(IDX_HEADS)]

    n_pair = (nck + 1) // 2

    def score_pair(c2, carry):
        chunks = (2 * c2, 2 * c2 + 1)
        rels = [jnp.dot(ik_ref[0, pl.ds(pl.multiple_of(c * tk, tk), tk), :], q_cat,
                        preferred_element_type=F32) for c in chunks]
        for c, rel in zip(chunks, rels):
            sc = w_rows[0] * jnp.maximum(rel[:, 0:tq], 0.0)
            for h in range(1, IDX_HEADS):
                sc = sc + w_rows[h] * jnp.maximum(rel[:, h * tq:(h + 1) * tq], 0.0)
            sc = jnp.where(c * tk + key_iota <= q_pos, sc, NEG)
            score_scr[c] = sc
            coarse_scr[c] = sc.astype(BF16)
        return carry

    lax.fori_loop(0, n_pair, score_pair, 0)

    def count(pred):
        def body(c2, acc):
            sc = score_scr[pl.ds(2 * c2, 2)].reshape(COUNT_CHAINS, 2 * groups // COUNT_CHAINS, SUBLANES, tq)
            return acc + jnp.sum(jnp.where(pred(sc), 1, 0), axis=1)
        acc = lax.fori_loop(0, n_pair, body, jnp.zeros((COUNT_CHAINS, SUBLANES, tq), I32))
        return jnp.sum(acc.reshape(COUNT_CHAINS * SUBLANES, tq), axis=0, keepdims=True)

    def count_coarse(cand_b):
        packed_rows = 2 * SUBLANES
        def body(c2, acc):
            hi = coarse_scr[pl.ds(2 * c2, 2)].reshape(COUNT_CHAINS, 2 * tk // (COUNT_CHAINS * packed_rows),
                                                      packed_rows, tq)
            for g in range(hi.shape[1]):
                acc = acc + jnp.where(hi[:, g] >= cand_b[None], jnp.ones((), BF16), jnp.zeros((), BF16))
            return acc
        acc = lax.fori_loop(0, n_pair, body, jnp.zeros((COUNT_CHAINS, packed_rows, tq), BF16))
        return jnp.sum(acc.astype(F32).reshape(COUNT_CHAINS * packed_rows, tq), axis=0, keepdims=True)

    def bisect_coarse(it, prefix):
        cand = prefix + lax.shift_left(jnp.int32(1), 15 - it)
        pattern = cand ^ ((cand >> 31) & 0x7FFF)
        cand_b = pltpu.bitcast(lax.shift_left(pattern, 16), F32).astype(BF16)
        cnt = count_coarse(cand_b)
        return jnp.where(cnt >= ksel, cand, prefix)

    assert (tk // (COUNT_CHAINS * SUBLANES)) * (coarse_scr.shape[0] // 2) <= 2 ** 8, "bf16 counts must stay exact"
    key16 = lax.fori_loop(0, 16, bisect_coarse, jnp.full((2 * SUBLANES, tq), -(2 ** 15), I32))[:SUBLANES]
    key_p = lax.shift_left(key16, 16) | ((key16 >> 31) & 0xFFFF)
    key_lo = key_p - 2 ** 16

    def bisect(it, prefix):
        cand = prefix + lax.shift_left(jnp.int32(1), 16 - it)
        cand_f = _key_to_f32(cand)
        cnt = count(lambda sc: sc >= cand_f[None])
        return jnp.where(cnt >= ksel, cand, prefix)

    vstar = _key_to_f32(lax.fori_loop(0, 17, bisect, key_lo))
    need_f = (ksel - count(lambda sc: sc > vstar[None])).astype(F32)

    tri = jnp.where(lax.broadcasted_iota(I32, (tk, tk), 1) <= lax.broadcasted_iota(I32, (tk, tk), 0),
                    1.0, 0.0).astype(BF16)

    def select_pair(c2, run):
        chunks = (2 * c2, 2 * c2 + 1)
        scs = [score_scr[c] for c in chunks]
        eqs = [sc == vstar[0:1] for sc in scs]
        ranks = [jnp.dot(tri, jnp.where(eq, 1.0, 0.0).astype(BF16), preferred_element_type=F32) for eq in eqs]
        for c, sc, eq, rank in zip(chunks, scs, eqs, ranks):
            rank = rank + run
            tie_bias = jnp.where(rank <= need_f, jnp.where(c * tk + key_iota <= q_pos, 0.0, NEG), NEG)
            bias_scr[c] = jnp.where(sc > vstar[0:1], 0.0, jnp.where(eq, tie_bias, NEG))
            run = rank[tk - 1:tk, :]
        return run

    lax.fori_loop(0, n_pair, select_pair, jnp.zeros((1, tq), F32))

    heads_per_half = MXU_DIM // ATT_HD
    n_half = ATT_W // MXU_DIM
    row_head = lax.broadcasted_iota(I32, (MXU_DIM, tq), 0) // ATT_HD
    for h in range(ATT_HEADS):
        half, hh = divmod(h, heads_per_half)
        qh = aqt_ref[0, half * MXU_DIM:(half + 1) * MXU_DIM, :]
        qbd_scr[half, :, hh * tq:(hh + 1) * tq] = jnp.where(row_head == hh, qh, jnp.zeros_like(qh))
    m_scr[...] = jnp.full_like(m_scr, NEG)
    acc_scr[...] = jnp.zeros_like(acc_scr)

    def att_pair(c2, carry):
        s_all = {}
        for half in range(n_half):
            for sub in range(2):
                rows = pl.ds(pl.multiple_of((2 * c2 + sub) * tk, tk), tk)
                s_all[sub, half] = jnp.dot(ak_ref[0, rows, half * MXU_DIM:(half + 1) * MXU_DIM], qbd_scr[half],
                                           preferred_element_type=F32)
        for half in range(n_half):
            for hh in range(heads_per_half):
                h = half * heads_per_half + hh
                sl = slice(hh * tq, (hh + 1) * tq)
                s = [s_all[sub, half][:, sl] + bias_scr[2 * c2 + sub] for sub in range(2)]
                m_old = m_scr[h]
                m_new = jnp.maximum(m_old, jnp.max(jnp.maximum(s[0], s[1]), axis=0, keepdims=True))
                alpha = jnp.exp2(m_old - m_new)
                pv = alpha * acc_scr[h]
                for sub in range(2):
                    p = jnp.exp2(s[sub] - m_new).astype(BF16)
                    vt = avt_ref[0, 2 * c2 + sub, h * ATT_VROWS:(h + 1) * ATT_VROWS, :]
                    pv = pv + jnp.dot(vt, p, preferred_element_type=F32)
                acc_scr[h] = pv
                m_scr[h] = m_new
        return carry

    lax.fori_loop(0, n_pair, att_pair, 0)

    att_t = jnp.concatenate([acc_scr[h, :ATT_HD] * (1.0 / acc_scr[h, ATT_HD:ATT_HD + 1])
                             for h in range(ATT_HEADS)], axis=0)
    out_ref[0] = (att_t.T * ag_ref[0]).astype(BF16)


def _dsa_attention(aqt, ak, avt, iqt, ik, iwt, ag):
    b, s, _ = ak.shape
    tq, tk = DSA_TQ, DSA_TK
    nck = s // tk
    ksel = min(TOPK_MAX, s // 4)
    qt = lambda rows: pl.BlockSpec((1, rows, tq), lambda bi, i: (bi, 0, i))
    seq = lambda width: pl.BlockSpec((1, s, width), lambda bi, i: (bi, 0, 0))
    qn = pl.BlockSpec((1, tq, ATT_W), lambda bi, i: (bi, i, 0))
    return pl.pallas_call(
        functools.partial(_dsa_kernel, ksel=ksel),
        grid=(b, s // tq),
        in_specs=[qt(ATT_W), seq(ATT_W),
                  pl.BlockSpec((1, nck, ATT_HEADS * ATT_VROWS, tk), lambda bi, i: (bi, 0, 0, 0)),
                  qt(IDX_QW), seq(LANES), qt(SUBLANES), qn],
        out_specs=qn,
        out_shape=jax.ShapeDtypeStruct((b, s, ATT_W), BF16),
        scratch_shapes=[pltpu.VMEM((nck, tk, tq), F32), pltpu.VMEM((nck, tk, tq), BF16),
                        pltpu.VMEM((nck, tk, tq), F32),
                        pltpu.VMEM((ATT_W // MXU_DIM, MXU_DIM, (MXU_DIM // ATT_HD) * tq), BF16),
                        pltpu.VMEM((ATT_HEADS, 1, tq), F32),
                        pltpu.VMEM((ATT_HEADS, ATT_VROWS, tq), F32)],
        compiler_params=pltpu.CompilerParams(dimension_semantics=("arbitrary", "arbitrary"),
                                             vmem_limit_bytes=VMEM_LIMIT),
        name="dsa_attention",
    )(aqt, ak, avt, iqt, ik, iwt, ag)


def _gla_kernel(gqk_ref, gv_ref, glog_ref, gg_ref, nw_ref, out_ref, state_scr):
    c_len, sub = GLA_CHUNK, GLA_SUB
    nb = gqk_ref.shape[0]
    n_z = c_len * sub

    @pl.when(pl.program_id(0) == 0)
    def _():
        state_scr[...] = jnp.zeros_like(state_scr)

    t_i = lax.broadcasted_iota(I32, (c_len, GLA_QKW), 0)
    s_i = lax.broadcasted_iota(I32, (sub, GLA_QKW), 0)
    j_i = lax.broadcasted_iota(I32, (c_len, GLA_QKW), 1) % c_len
    tril = jnp.where(lax.broadcasted_iota(I32, (c_len, c_len), 0)
                     >= lax.broadcasted_iota(I32, (c_len, c_len), 1), 1.0, 0.0).astype(BF16)
    lane_head = lax.broadcasted_iota(I32, (c_len, GLA_QKW), 1) // GLA_DK

    def stack_heads(a):
        return jnp.concatenate([jnp.where(lane_head == h, a, 0.0) for h in range(GLA_HEADS)],
                               axis=0).astype(BF16)

    zr = lax.broadcasted_iota(I32, (n_z, GLA_QKW), 0)
    zc = lax.broadcasted_iota(I32, (n_z, GLA_QKW), 1)
    pick = (zc % c_len) == (zr // (sub * sub)) * sub + zr % sub
    hr = lax.broadcasted_iota(I32, (GLA_QKW, GLA_QKW), 0) // GLA_DK
    hc = lax.broadcasted_iota(I32, (GLA_QKW, GLA_QKW), 1) // c_len
    head_rep = jnp.where(hr == hc, 1.0, 0.0).astype(BF16)
    sr = lax.broadcasted_iota(I32, (c_len, n_z), 0)
    sc_ = lax.broadcasted_iota(I32, (c_len, n_z), 1)
    row_sel = jnp.where(sr == sc_ // sub, 1.0, 0.0).astype(BF16)
    vr = lax.broadcasted_iota(I32, (GLA_HEADS * c_len, GLA_W), 0) // c_len
    vc = lax.broadcasted_iota(I32, (GLA_HEADS * c_len, GLA_W), 1) // GLA_DV
    v_diag = vr == vc
    spans = [c_len >> (lv + 1) for lv in range((c_len // sub).bit_length() - 1)]

    units = range(nb * (gqk_ref.shape[1] // c_len))
    bat = lambda u: u % nb
    tsl = lambda u: slice((u // nb) * c_len, (u // nb + 1) * c_len)
    cur_state = {bi: state_scr[bi] for bi in range(nb)}
    q, k, vb, b, b_last, state_t, o, a, rep = ({} for _ in range(9))

    for u in units:
        gqk = gqk_ref[bat(u), tsl(u), :]
        q[u], k[u] = gqk[:, :GLA_QKW], gqk[:, GLA_QKW:]
        vb[u] = gv_ref[bat(u), tsl(u), :].astype(BF16)
        g = glog_ref[bat(u), tsl(u), :]
        g1 = g.astype(BF16)
        r1 = g - g1.astype(F32)
        g2 = r1.astype(BF16)
        g3 = (r1 - g2.astype(F32)).astype(BF16)
        b[u] = (jnp.dot(tril, g1, preferred_element_type=F32) + jnp.dot(tril, g2, preferred_element_type=F32)
                 + jnp.dot(tril, g3, preferred_element_type=F32))
        b_last[u] = b[u][c_len - 1:c_len, :]

    for u in units:
        state_t[u] = cur_state[bat(u)]
        o_stack = _nt_dot(stack_heads(q[u] * jnp.exp(b[u])), state_t[u].astype(BF16))
        o[u] = jnp.concatenate([o_stack[h * c_len:(h + 1) * c_len] for h in range(GLA_HEADS)], axis=1)

        k_bd = stack_heads(k[u] * jnp.exp(b_last[u] - b[u]))
        v_st = jnp.concatenate([vb[u][:, h * GLA_DV:(h + 1) * GLA_DV] for h in range(GLA_HEADS)], axis=0)
        cur_state[bat(u)] = jnp.exp(b_last[u]) * state_t[u] + _tn_dot(v_st, k_bd)

        for span in spans:
            blk = 2 * span
            ref = jnp.concatenate([jnp.broadcast_to(b[u][m * blk + span - 1:m * blk + span, :], (blk, GLA_QKW))
                                   for m in range(c_len // blk)], axis=0)
            upper = (t_i % blk) >= span
            q_l = q[u] * jnp.exp(jnp.where(upper, b[u] - ref, NEG))
            k_l = k[u] * jnp.exp(jnp.where(upper, NEG, ref - b[u]))
            a_l = _nt_dot(q_l.astype(BF16), stack_heads(k_l))
            if blk < c_len:
                a_l = jnp.where(t_i // blk == j_i // blk, a_l, 0.0)
            a[u] = a_l if u not in a else a[u] + a_l

    for bi in range(nb):
        state_scr[bi] = cur_state[bi]

    for u in units:
        zs = []
        for t in range(c_len):
            blk0 = (t // sub) * sub
            rs = slice(blk0, blk0 + sub)
            dec = jnp.exp(jnp.where(s_i <= t - blk0, b[u][t:t + 1, :] - b[u][rs], NEG))
            zs.append(q[u][t:t + 1, :] * k[u][rs] * dec)
        zmat = jnp.concatenate(zs, axis=0).astype(BF16)
        rep[u] = jnp.dot(zmat, head_rep, preferred_element_type=F32)

    for u in units:
        a_all = a[u] + jnp.dot(row_sel, jnp.where(pick, rep[u], 0.0).astype(BF16), preferred_element_type=F32)
        v_bd = jnp.where(v_diag, jnp.concatenate([vb[u]] * GLA_HEADS, axis=0), jnp.zeros((), BF16))
        o_u = o[u] + jnp.dot(a_all.astype(BF16), v_bd, preferred_element_type=F32)
        outs = []
        for h in range(GLA_HEADS):
            oh = o_u[:, h * GLA_DV:(h + 1) * GLA_DV]
            oh = oh * lax.rsqrt(jnp.mean(oh * oh, axis=-1, keepdims=True) + NORM_EPS)
            outs.append(oh * nw_ref[...])
        out_ref[bat(u), tsl(u), :] = (jnp.concatenate(outs, axis=1) * gg_ref[bat(u), tsl(u), :]).astype(BF16)


def _gla(gqk, gv, glog, gg, norm_w):
    b, s, _ = gqk.shape
    c_len = GLA_CHUNK
    step = min(GLA_STEP, s)
    blk = lambda width: pl.BlockSpec((b, step, width), lambda c: (0, c, 0))
    nw2 = norm_w.reshape(1, GLA_DV)
    return pl.pallas_call(
        _gla_kernel,
        grid=(s // step,),
        in_specs=[blk(2 * GLA_QKW), blk(GLA_W), blk(GLA_QKW), blk(GLA_W),
                  pl.BlockSpec(nw2.shape, lambda c: (0, 0))],
        out_specs=blk(GLA_W),
        out_shape=jax.ShapeDtypeStruct((b, s, GLA_W), BF16),
        scratch_shapes=[pltpu.VMEM((b, GLA_DV, GLA_QKW), F32)],
        compiler_params=pltpu.CompilerParams(dimension_semantics=("arbitrary",),
                                             vmem_limit_bytes=VMEM_LIMIT),
        name="gla_chunked",
    )(gqk, gv, glog, gg, nw2)


def _out_kernel(att_ref, gla_ref, wa_ref, wg_ref, x_ref, nw_ref, out_ref):
    mixed = jnp.dot(att_ref[...], wa_ref[...], preferred_element_type=F32) + \
        jnp.dot(gla_ref[...], wg_ref[...], preferred_element_type=F32)
    y = mixed * lax.rsqrt(jnp.mean(mixed * mixed, axis=-1, keepdims=True) + NORM_EPS)
    out_ref[...] = x_ref[...] + y * nw_ref[...]


def _out_project(att2, gla2, w_out, x2, post_norm_w):
    n, d = x2.shape
    tm = min(OUT_TM, n)
    wa, wg = w_out[:ATT_W].astype(BF16), w_out[ATT_W:].astype(BF16)
    row = lambda width: pl.BlockSpec((tm, width), lambda i: (i, 0))
    full = lambda a: pl.BlockSpec(a.shape, lambda i: (0, 0))
    return pl.pallas_call(
        _out_kernel,
        grid=(n // tm,),
        in_specs=[row(ATT_W), row(GLA_W), full(wa), full(wg), row(d), full(post_norm_w)],
        out_specs=row(d),
        out_shape=jax.ShapeDtypeStruct((n, d), F32),
        compiler_params=pltpu.CompilerParams(dimension_semantics=("arbitrary",),
                                             vmem_limit_bytes=VMEM_LIMIT),
        name="out_proj_postnorm",
    )(att2, gla2, wa, wg, x2, post_norm_w)


def kernel(x, positions, w_in, w_gla_gate_up, b_gla_gate, gla_norm_w, w_out, pre_norm_w, post_norm_w):
    b, s, d = x.shape
    n = b * s
    cos8, sin8 = _rope_tables(positions)
    aqt, ak, avt, iqt, ik, iwt, ag, gqk, gv, glog, gg = _project(x, pre_norm_w.reshape(1, d), cos8, sin8, w_in,
                                                                  w_gla_gate_up, b_gla_gate)
    att = _dsa_attention(aqt, ak, avt, iqt, ik, iwt, ag)
    gla = _gla(gqk, gv, glog, gg, gla_norm_w)
    out = _out_project(att.reshape(n, ATT_W), gla.reshape(n, GLA_W), w_out, x.reshape(n, d),
                       post_norm_w.reshape(1, d))
    return out.reshape(b, s, d)
```

```python
import functools

import jax
import jax.numpy as jnp
import numpy as np
from jax import lax
from jax.experimental import pallas as pl
from jax.experimental.pallas import tpu as pltpu

F32 = jnp.float32
BF16 = jnp.bfloat16
I32 = jnp.int32

ATT_HEADS = 8
ATT_HD = 64
ATT_W = ATT_HEADS * ATT_HD
IDX_HEADS = 4
IDX_HD = 64
IDX_QW = IDX_HEADS * IDX_HD
TOPK_MAX = 256
GLA_HEADS = 4
GLA_DK = 64
GLA_DV = 128
GLA_QKW = GLA_HEADS * GLA_DK
GLA_W = GLA_HEADS * GLA_DV
GLA_GATE_RANK = 16
GLA_TAU = 16.0
GLA_CHUNK = 64
GLA_SUB = 8
GLA_STEP = 256
ROPE_THETA = 500000.0
ROT_DIM = ATT_HD // 4
ROT_HALF = ROT_DIM // 2
NORM_EPS = 1e-6
NEG = float(np.float32(-1e30))
LOG2E = 1.4426950408889634

SUBLANES = 8
LANES = 128
MXU_DIM = 256
VMEM_LIMIT = 56 * 1024 * 1024

PROJ_TM = 512
OUT_TM = 1024
DSA_TQ = 512
DSA_TK = 256
ATT_SQ = 512
COUNT_CHAINS = 4
ATT_VROWS = ATT_HD + 16
INT_MIN = -(2 ** 31)


def _nt_dot(a, b):
    return lax.dot_general(a, b, (((1,), (1,)), ((), ())), preferred_element_type=F32)


def _tn_dot(a, b):
    return lax.dot_general(a, b, (((0,), (0,)), ((), ())), preferred_element_type=F32)


def _rope_table_kernel(pos_ref, inv_ref, cos_ref, sin_ref):
    ang = pos_ref[...].astype(F32) * inv_ref[...]
    cos_ref[...] = jnp.cos(ang)
    sin_ref[...] = jnp.sin(ang)


def _rope_tables(positions):
    n = positions.size
    inv = ROPE_THETA ** (-jnp.arange(0, ROT_DIM, 2, dtype=F32) / ROT_DIM)
    return pl.pallas_call(
        _rope_table_kernel,
        out_shape=(jax.ShapeDtypeStruct((ROT_HALF, n), F32),) * 2,
        name="rope_tables",
    )(positions.reshape(1, n), inv.reshape(ROT_HALF, 1))


def _proj_kernel(x_ref, pnw_ref, cos_ref, sin_ref,
                 wt_all, w_ag, w_gqk, w_gv, w_gg,
                 wupt_hi, wupt_lo, bup_ref,
                 aqt_o, ak_o, avt_o, iqt_o, ik_o, iwt_o, ag_o, gqk_o, gv_o, glog_o, gg_o):
    x = x_ref[0]
    ms = jnp.mean(x * x, axis=-1, keepdims=True)
    h = (x * lax.rsqrt(ms + NORM_EPS)) * pnw_ref[...]
    hb = h.astype(BF16)
    cos8, sin8 = cos_ref[...], sin_ref[...]

    def mm(w_ref):
        return jnp.dot(hb, w_ref[...], preferred_element_type=F32)

    def rope_t(yt):
        pieces = []
        for hd in range(yt.shape[0] // ATT_HD):
            rows = yt[hd * ATT_HD:(hd + 1) * ATT_HD]
            x1, x2 = rows[:ROT_HALF], rows[ROT_HALF:ROT_DIM]
            pieces += [x1 * cos8 - x2 * sin8, x2 * cos8 + x1 * sin8, rows[ROT_DIM:]]
        return jnp.concatenate(pieces, axis=0)

    yt = _nt_dot(wt_all[...], hb)
    ikt, o = yt[:LANES], LANES
    aq_t, o = yt[o:o + ATT_W], o + ATT_W
    ak_t, o = yt[o:o + ATT_W], o + ATT_W
    iq_t, o = yt[o:o + IDX_QW], o + IDX_QW
    av_t = yt[o:o + ATT_W]
    iw_row = IDX_HD + GLA_GATE_RANK
    iwt_o[0] = ikt[iw_row:iw_row + SUBLANES] * (IDX_HEADS ** -0.5 * IDX_HD ** -0.5)
    aqt_o[0] = (rope_t(aq_t) * (ATT_HD ** -0.5 * LOG2E)).astype(BF16)
    ik_o[0] = jnp.concatenate([rope_t(ikt[:IDX_HD]), jnp.zeros_like(ikt[IDX_HD:])], axis=0).T.astype(BF16)
    ak_o[0] = rope_t(ak_t).T.astype(BF16)
    ikt_hi = ikt.astype(BF16)
    ikt_lo = (ikt - ikt_hi.astype(F32)).astype(BF16)
    zt = (jnp.dot(wupt_hi[...], ikt_hi, preferred_element_type=F32)
          + jnp.dot(wupt_lo[...], ikt_hi, preferred_element_type=F32)
          + jnp.dot(wupt_hi[...], ikt_lo, preferred_element_type=F32))
    a_gate = mm(w_ag)
    z = zt.T + bup_ref[...]
    glog_o[0] = (jnp.minimum(z, 0.0) - jnp.log1p(jnp.exp(-jnp.abs(z)))) * (1.0 / GLA_TAU)
    g_gate = mm(w_gg)
    ag_o[0] = a_gate * jax.nn.sigmoid(a_gate)
    iqt_o[0] = rope_t(iq_t).astype(BF16)
    gg_o[0] = g_gate * jax.nn.sigmoid(g_gate)
    gqk = mm(w_gqk)
    lane = lax.broadcasted_iota(I32, gqk.shape, 1)
    gqk_o[0] = jnp.where(lane < GLA_QKW, gqk * (GLA_DK ** -0.5), gqk)
    avt = av_t
    ones_rows = jnp.ones((ATT_VROWS - ATT_HD, avt.shape[1]), F32)
    avt = jnp.concatenate([piece for hd in range(ATT_HEADS)
                           for piece in (avt[hd * ATT_HD:(hd + 1) * ATT_HD], ones_rows)], axis=0).astype(BF16)
    for c in range(avt_o.shape[1]):
        avt_o[0, c] = avt[:, c * DSA_TK:(c + 1) * DSA_TK]
    gv_o[0] = mm(w_gv)


def _split_w_in(w_in):
    sizes = (ATT_W, ATT_W, ATT_W, IDX_QW, IDX_HD, IDX_HEADS, ATT_W,
             GLA_QKW, GLA_QKW, GLA_W, GLA_GATE_RANK, GLA_W)
    offs = np.cumsum((0,) + sizes)
    seg = [w_in[:, offs[i]:offs[i + 1]] for i in range(len(sizes))]
    a_q, a_k, a_v, i_q, i_k, i_w, a_g, g_q, g_k, g_v, g_lr, g_g = seg
    d = w_in.shape[0]
    pad = lambda w, n: jnp.concatenate([w, jnp.zeros((d, n - w.shape[1]), w.dtype)], axis=1)
    small = pad(jnp.concatenate([i_k, g_lr, i_w], axis=1), LANES)
    transposed = jnp.concatenate([small, a_q, a_k, i_q, a_v], axis=1).T
    natural = [a_g, jnp.concatenate([g_q, g_k], axis=1), g_v, g_g]
    return [transposed.astype(BF16)], [w.astype(BF16) for w in natural]


def _project(x, pre_norm_w, cos8, sin8, w_in, w_up, b_up):
    b, s, d = x.shape
    wupt = jnp.concatenate([jnp.zeros((GLA_QKW, IDX_HD), F32), w_up.T,
                            jnp.zeros((GLA_QKW, LANES - IDX_HD - GLA_GATE_RANK), F32)], axis=1)
    wupt_hi = wupt.astype(BF16)
    wupt_lo = (wupt - wupt_hi.astype(F32)).astype(BF16)
    gate_ops = [wupt_hi, wupt_lo, b_up.reshape(1, GLA_QKW)]
    tm = min(PROJ_TM, s)
    nt = s // tm
    cpt = tm // DSA_TK
    wts, wns = _split_w_in(w_in)
    full = lambda a: pl.BlockSpec(a.shape, lambda bi, i: (0,) * a.ndim)
    nat = lambda width: pl.BlockSpec((1, tm, width), lambda bi, i: (bi, i, 0))
    tr = lambda rows: pl.BlockSpec((1, rows, tm), lambda bi, i: (bi, 0, i))
    tab = pl.BlockSpec((ROT_HALF, tm), lambda bi, i: (0, bi * nt + i))
    vrows = ATT_HEADS * ATT_VROWS
    out_specs = [tr(ATT_W), nat(ATT_W), pl.BlockSpec((1, cpt, vrows, DSA_TK), lambda bi, i: (bi, i, 0, 0)),
                 tr(IDX_QW), nat(LANES), tr(SUBLANES),
                 nat(ATT_W), nat(2 * GLA_QKW), nat(GLA_W), nat(GLA_QKW), nat(GLA_W)]
    sds = jax.ShapeDtypeStruct
    out_shape = [sds((b, ATT_W, s), BF16), sds((b, s, ATT_W), BF16), sds((b, s // DSA_TK, vrows, DSA_TK), BF16),
                 sds((b, IDX_QW, s), BF16), sds((b, s, LANES), BF16), sds((b, SUBLANES, s), F32),
                 sds((b, s, ATT_W), F32), sds((b, s, 2 * GLA_QKW), F32), sds((b, s, GLA_W), F32),
                 sds((b, s, GLA_QKW), F32), sds((b, s, GLA_W), F32)]
    return pl.pallas_call(
        _proj_kernel,
        grid=(b, nt),
        in_specs=[nat(d), full(pre_norm_w), tab, tab] + [full(w) for w in wts + wns + gate_ops],
        out_specs=out_specs,
        out_shape=out_shape,
        compiler_params=pltpu.CompilerParams(dimension_semantics=("arbitrary", "arbitrary"),
                                             vmem_limit_bytes=VMEM_LIMIT),
        name="prenorm_in_proj",
    )(x, pre_norm_w, cos8, sin8, *wts, *wns, *gate_ops)


def _key_to_f32(key):
    return pltpu.bitcast(key ^ ((key >> 31) & 0x7FFFFFFF), F32)


def _dsa_kernel(aqt_ref, ak_ref, avt_ref, iqt_ref, ik_ref, iwt_ref, ag_ref, out_ref,
                score_scr, coarse_scr, bias_scr, qbd_scr, m_scr, acc_scr, *, ksel):
    tq, tk = DSA_TQ, DSA_TK
    i = pl.program_id(1)
    nck = (i * tq + tq + tk - 1) // tk
    groups = tk // SUBLANES

    key_iota = lax.broadcasted_iota(I32, (tk, tq), 0)
    q_pos = i * tq + lax.broadcasted_iota(I32, (tk, tq), 1)

    iqt = iqt_ref[0]
    q_cat = jnp.concatenate([iqt[h * IDX_HD:(h + 1) * IDX_HD] for h in range(IDX_HEADS)], axis=1)
    q_cat = jnp.concatenate([q_cat, jnp.zeros((LANES - IDX_HD, IDX_HEADS * tq), BF16)], axis=0)
    iwt = iwt_ref[0]
    w_rows = [iwt[h:h + 1, :] for h in range(IDX_HEADS)]

    n_pair = (nck + 1) // 2

    def score_pair(c2, carry):
        chunks = (2 * c2, 2 * c2 + 1)
        rels = [jnp.dot(ik_ref[0, pl.ds(pl.multiple_of(c * tk, tk), tk), :], q_cat,
                        preferred_element_type=F32) for c in chunks]
        for c, rel in zip(chunks, rels):
            sc = w_rows[0] * jnp.maximum(rel[:, 0:tq], 0.0)
            for h in range(1, IDX_HEADS):
                sc = sc + w_rows[h] * jnp.maximum(rel[:, h * tq:(h + 1) * tq], 0.0)
            sc = jnp.where(c * tk + key_iota <= q_pos, sc, NEG)
            score_scr[c] = sc
            coarse_scr[c] = sc.astype(BF16)
        return carry

    lax.fori_loop(0, n_pair, score_pair, 0)

    def count(pred):
        def body(c2, acc):
            sc = score_scr[pl.ds(2 * c2, 2)].reshape(COUNT_CHAINS, 2 * groups // COUNT_CHAINS, SUBLANES, tq)
            return acc + jnp.sum(jnp.where(pred(sc), 1, 0), axis=1)
        acc = lax.fori_loop(0, n_pair, body, jnp.zeros((COUNT_CHAINS, SUBLANES, tq), I32))
        return jnp.sum(acc.reshape(COUNT_CHAINS * SUBLANES, tq), axis=0, keepdims=True)

    def count_coarse(cand_b):
        packed_rows = 2 * SUBLANES
        def body(c2, acc):
            hi = coarse_scr[pl.ds(2 * c2, 2)].reshape(COUNT_CHAINS, 2 * tk // (COUNT_CHAINS * packed_rows),
                                                      packed_rows, tq)
            for g in range(hi.shape[1]):
                acc = acc + jnp.where(hi[:, g] >= cand_b[None], jnp.ones((), BF16), jnp.zeros((), BF16))
            return acc
        acc = lax.fori_loop(0, n_pair, body, jnp.zeros((COUNT_CHAINS, packed_rows, tq), BF16))
        return jnp.sum(acc.astype(F32).reshape(COUNT_CHAINS * packed_rows, tq), axis=0, keepdims=True)

    def bisect_coarse(it, prefix):
        cand = prefix + lax.shift_left(jnp.int32(1), 15 - it)
        pattern = cand ^ ((cand >> 31) & 0x7FFF)
        cand_b = pltpu.bitcast(lax.shift_left(pattern, 16), F32).astype(BF16)
        cnt = count_coarse(cand_b)
        return jnp.where(cnt >= ksel, cand, prefix)

    assert (tk // (COUNT_CHAINS * SUBLANES)) * (coarse_scr.shape[0] // 2) <= 2 ** 8, "bf16 counts must stay exact"
    key16 = lax.fori_loop(0, 16, bisect_coarse, jnp.full((2 * SUBLANES, tq), -(2 ** 15), I32))[:SUBLANES]
    key_p = lax.shift_left(key16, 16) | ((key16 >> 31) & 0xFFFF)
    key_lo = key_p - 2 ** 16

    def bisect(it, prefix):
        cand = prefix + lax.shift_left(jnp.int32(1), 16 - it)
        cand_f = _key_to_f32(cand)
        cnt = count(lambda sc: sc >= cand_f[None])
        return jnp.where(cnt >= ksel, cand, prefix)

    vstar = _key_to_f32(lax.fori_loop(0, 17, bisect, key_lo))
    need_f = (ksel - count(lambda sc: sc > vstar[None])).astype(F32)

    tri = jnp.where(lax.broadcasted_iota(I32, (tk, tk), 1) <= lax.broadcasted_iota(I32, (tk, tk), 0),
                    1.0, 0.0).astype(BF16)

    def select_pair(c2, run):
        chunks = (2 * c2, 2 * c2 + 1)
        scs = [score_scr[c] for c in chunks]
        eqs = [sc == vstar[0:1] for sc in scs]
        ranks = [jnp.dot(tri, jnp.where(eq, 1.0, 0.0).astype(BF16), preferred_element_type=F32) for eq in eqs]
        for c, sc, eq, rank in zip(chunks, scs, eqs, ranks):
            rank = rank + run
            tie_bias = jnp.where(rank <= need_f, jnp.where(c * tk + key_iota <= q_pos, 0.0, NEG), NEG)
            bias_scr[c] = jnp.where(sc > vstar[0:1], 0.0, jnp.where(eq, tie_bias, NEG))
            run = rank[tk - 1:tk, :]
        return run

    lax.fori_loop(0, n_pair, select_pair, jnp.zeros((1, tq), F32))

    heads_per_half = MXU_DIM // ATT_HD
    n_half = ATT_W // MXU_DIM
    row_head = lax.broadcasted_iota(I32, (MXU_DIM, tq), 0) // ATT_HD
    for h in range(ATT_HEADS):
        half, hh = divmod(h, heads_per_half)
        qh = aqt_ref[0, half * MXU_DIM:(half + 1) * MXU_DIM, :]
        qbd_scr[half, :, hh * tq:(hh + 1) * tq] = jnp.where(row_head == hh, qh, jnp.zeros_like(qh))
    m_scr[...] = jnp.full_like(m_scr, NEG)
    acc_scr[...] = jnp.zeros_like(acc_scr)

    def att_pair(c2, carry):
        s_all = {}
        for sub in range(2):
            for half in range(n_half):
                rows = pl.ds(pl.multiple_of((2 * c2 + sub) * tk, tk), tk)
                s_all[sub, half] = jnp.dot(ak_ref[0, rows, half * MXU_DIM:(half + 1) * MXU_DIM], qbd_scr[half],
                                           preferred_element_type=F32)
        for sub in range(2):
            for half in range(n_half):
                for hh in range(heads_per_half):
                    h = half * heads_per_half + hh
                    vt = avt_ref[0, 2 * c2 + sub, h * ATT_VROWS:(h + 1) * ATT_VROWS, :]
                    for qs in range(tq // ATT_SQ):
                        ql = slice(qs * ATT_SQ, (qs + 1) * ATT_SQ)
                        s = s_all[sub, half][:, hh * tq + qs * ATT_SQ:hh * tq + (qs + 1) * ATT_SQ] + \
                            bias_scr[2 * c2 + sub, :, ql]
                        m_old = m_scr[h, :, ql]
                        m_new = jnp.maximum(m_old, jnp.max(s, axis=0, keepdims=True))
                        alpha = jnp.exp2(m_old - m_new)
                        p = jnp.exp2(s - m_new).astype(BF16)
                        acc_scr[h, :, ql] = alpha * acc_scr[h, :, ql] + jnp.dot(vt, p, preferred_element_type=F32)
                        m_scr[h, :, ql] = m_new
        return carry

    lax.fori_loop(0, n_pair, att_pair, 0)

    att_t = jnp.concatenate([acc_scr[h, :ATT_HD] * (1.0 / acc_scr[h, ATT_HD:ATT_HD + 1])
                             for h in range(ATT_HEADS)], axis=0)
    out_ref[0] = (att_t.T * ag_ref[0]).astype(BF16)


def _dsa_attention(aqt, ak, avt, iqt, ik, iwt, ag):
    b, s, _ = ak.shape
    tq, tk = DSA_TQ, DSA_TK
    nck = s // tk
    ksel = min(TOPK_MAX, s // 4)
    qt = lambda rows: pl.BlockSpec((1, rows, tq), lambda bi, i: (bi, 0, i))
    seq = lambda width: pl.BlockSpec((1, s, width), lambda bi, i: (bi, 0, 0))
    qn = pl.BlockSpec((1, tq, ATT_W), lambda bi, i: (bi, i, 0))
    return pl.pallas_call(
        functools.partial(_dsa_kernel, ksel=ksel),
        grid=(b, s // tq),
        in_specs=[qt(ATT_W), seq(ATT_W),
                  pl.BlockSpec((1, nck, ATT_HEADS * ATT_VROWS, tk), lambda bi, i: (bi, 0, 0, 0)),
                  qt(IDX_QW), seq(LANES), qt(SUBLANES), qn],
        out_specs=qn,
        out_shape=jax.ShapeDtypeStruct((b, s, ATT_W), BF16),
        scratch_shapes=[pltpu.VMEM((nck, tk, tq), F32), pltpu.VMEM((nck, tk, tq), BF16),
                        pltpu.VMEM((nck, tk, tq), F32),
                        pltpu.VMEM((ATT_W // MXU_DIM, MXU_DIM, (MXU_DIM // ATT_HD) * tq), BF16),
                        pltpu.VMEM((ATT_HEADS, 1, tq), F32),
                        pltpu.VMEM((ATT_HEADS, ATT_VROWS, tq), F32)],
        compiler_params=pltpu.CompilerParams(dimension_semantics=("arbitrary", "arbitrary"),
                                             vmem_limit_bytes=VMEM_LIMIT),
        name="dsa_attention",
    )(aqt, ak, avt, iqt, ik, iwt, ag)


def _gla_kernel(gqk_ref, gv_ref, glog_ref, gg_ref, nw_ref, out_ref, state_scr):
    c_len, sub = GLA_CHUNK, GLA_SUB
    nb = gqk_ref.shape[0]
    n_z = c_len * sub

    @pl.when(pl.program_id(0) == 0)
    def _():
        state_scr[...] = jnp.zeros_like(state_scr)

    t_i = lax.broadcasted_iota(I32, (c_len, GLA_QKW), 0)
    s_i = lax.broadcasted_iota(I32, (sub, GLA_QKW), 0)
    j_i = lax.broadcasted_iota(I32, (c_len, GLA_QKW), 1) % c_len
    tril = jnp.where(lax.broadcasted_iota(I32, (c_len, c_len), 0)
                     >= lax.broadcasted_iota(I32, (c_len, c_len), 1), 1.0, 0.0).astype(BF16)
    lane_head = lax.broadcasted_iota(I32, (c_len, GLA_QKW), 1) // GLA_DK

    def stack_heads(a):
        return jnp.concatenate([jnp.where(lane_head == h, a, 0.0) for h in range(GLA_HEADS)],
                               axis=0).astype(BF16)

    zr = lax.broadcasted_iota(I32, (n_z, GLA_QKW), 0)
    zc = lax.broadcasted_iota(I32, (n_z, GLA_QKW), 1)
    pick = (zc % c_len) == (zr // (sub * sub)) * sub + zr % sub
    hr = lax.broadcasted_iota(I32, (GLA_QKW, GLA_QKW), 0) // GLA_DK
    hc = lax.broadcasted_iota(I32, (GLA_QKW, GLA_QKW), 1) // c_len
    head_rep = jnp.where(hr == hc, 1.0, 0.0).astype(BF16)
    sr = lax.broadcasted_iota(I32, (c_len, n_z), 0)
    sc_ = lax.broadcasted_iota(I32, (c_len, n_z), 1)
    row_sel = jnp.where(sr == sc_ // sub, 1.0, 0.0).astype(BF16)
    vr = lax.broadcasted_iota(I32, (GLA_HEADS * c_len, GLA_W), 0) // c_len
    vc = lax.broadcasted_iota(I32, (GLA_HEADS * c_len, GLA_W), 1) // GLA_DV
    v_diag = vr == vc
    spans = [c_len >> (lv + 1) for lv in range((c_len // sub).bit_length() - 1)]

    units = range(nb * (gqk_ref.shape[1] // c_len))
    bat = lambda u: u % nb
    tsl = lambda u: slice((u // nb) * c_len, (u // nb + 1) * c_len)
    cur_state = {bi: state_scr[bi] for bi in range(nb)}
    q, k, vb, b, b_last, state_t, o, a, rep = ({} for _ in range(9))

    for u in units:
        gqk = gqk_ref[bat(u), tsl(u), :]
        q[u], k[u] = gqk[:, :GLA_QKW], gqk[:, GLA_QKW:]
        vb[u] = gv_ref[bat(u), tsl(u), :].astype(BF16)
        g = glog_ref[bat(u), tsl(u), :]
        g1 = g.astype(BF16)
        r1 = g - g1.astype(F32)
        g2 = r1.astype(BF16)
        g3 = (r1 - g2.astype(F32)).astype(BF16)
        b[u] = (jnp.dot(tril, g1, preferred_element_type=F32) + jnp.dot(tril, g2, preferred_element_type=F32)
                 + jnp.dot(tril, g3, preferred_element_type=F32))
        b_last[u] = b[u][c_len - 1:c_len, :]

    for u in units:
        state_t[u] = cur_state[bat(u)]
        o_stack = _nt_dot(stack_heads(q[u] * jnp.exp(b[u])), state_t[u].astype(BF16))
        o[u] = jnp.concatenate([o_stack[h * c_len:(h + 1) * c_len] for h in range(GLA_HEADS)], axis=1)

        k_bd = stack_heads(k[u] * jnp.exp(b_last[u] - b[u]))
        v_st = jnp.concatenate([vb[u][:, h * GLA_DV:(h + 1) * GLA_DV] for h in range(GLA_HEADS)], axis=0)
        cur_state[bat(u)] = jnp.exp(b_last[u]) * state_t[u] + _tn_dot(v_st, k_bd)

        for span in spans:
            blk = 2 * span
            ref = jnp.concatenate([jnp.broadcast_to(b[u][m * blk + span - 1:m * blk + span, :], (blk, GLA_QKW))
                                   for m in range(c_len // blk)], axis=0)
            upper = (t_i % blk) >= span
            q_l = q[u] * jnp.exp(jnp.where(upper, b[u] - ref, NEG))
            k_l = k[u] * jnp.exp(jnp.where(upper, NEG, ref - b[u]))
            a_l = _nt_dot(q_l.astype(BF16), stack_heads(k_l))
            if blk < c_len:
                a_l = jnp.where(t_i // blk == j_i // blk, a_l, 0.0)
            a[u] = a_l if u not in a else a[u] + a_l

    for bi in range(nb):
        state_scr[bi] = cur_state[bi]

    for u in units:
        zs = []
        for t in range(c_len):
            blk0 = (t // sub) * sub
            rs = slice(blk0, blk0 + sub)
            dec = jnp.exp(jnp.where(s_i <= t - blk0, b[u][t:t + 1, :] - b[u][rs], NEG))
            zs.append(q[u][t:t + 1, :] * k[u][rs] * dec)
        zmat = jnp.concatenate(zs, axis=0).astype(BF16)
        rep[u] = jnp.dot(zmat, head_rep, preferred_element_type=F32)

    for u in units:
        a_all = a[u] + jnp.dot(row_sel, jnp.where(pick, rep[u], 0.0).astype(BF16), preferred_element_type=F32)
        v_bd = jnp.where(v_diag, jnp.concatenate([vb[u]] * GLA_HEADS, axis=0), jnp.zeros((), BF16))
        o_u = o[u] + jnp.dot(a_all.astype(BF16), v_bd, preferred_element_type=F32)
        outs = []
        for h in range(GLA_HEADS):
            oh = o_u[:, h * GLA_DV:(h + 1) * GLA_DV]
            oh = oh * lax.rsqrt(jnp.mean(oh * oh, axis=-1, keepdims=True) + NORM_EPS)
            outs.append(oh * nw_ref[...])
        out_ref[bat(u), tsl(u), :] = (jnp.concatenate(outs, axis=1) * gg_ref[bat(u), tsl(u), :]).astype(BF16)


def _gla(gqk, gv, glog, gg, norm_w):
    b, s, _ = gqk.shape
    c_len = GLA_CHUNK
    step = min(GLA_STEP, s)
    blk = lambda width: pl.BlockSpec((b, step, width), lambda c: (0, c, 0))
    nw2 = norm_w.reshape(1, GLA_DV)
    return pl.pallas_call(
        _gla_kernel,
        grid=(s // step,),
        in_specs=[blk(2 * GLA_QKW), blk(GLA_W), blk(GLA_QKW), blk(GLA_W),
                  pl.BlockSpec(nw2.shape, lambda c: (0, 0))],
        out_specs=blk(GLA_W),
        out_shape=jax.ShapeDtypeStruct((b, s, GLA_W), BF16),
        scratch_shapes=[pltpu.VMEM((b, GLA_DV, GLA_QKW), F32)],
        compiler_params=pltpu.CompilerParams(dimension_semantics=("arbitrary",),
                                             vmem_limit_bytes=VMEM_LIMIT),
        name="gla_chunked",
    )(gqk, gv, glog, gg, nw2)


def _out_kernel(att_ref, gla_ref, wa_ref, wg_ref, x_ref, nw_ref, out_ref):
    mixed = jnp.dot(att_ref[...], wa_ref[...], preferred_element_type=F32) + \
        jnp.dot(gla_ref[...], wg_ref[...], preferred_element_type=F32)
    y = mixed * lax.rsqrt(jnp.mean(mixed * mixed, axis=-1, keepdims=True) + NORM_EPS)
    out_ref[...] = x_ref[...] + y * nw_ref[...]


def _out_project(att2, gla2, w_out, x2, post_norm_w):
    n, d = x2.shape
    tm = min(OUT_TM, n)
    wa, wg = w_out[:ATT_W].astype(BF16), w_out[ATT_W:].astype(BF16)
    row = lambda width: pl.BlockSpec((tm, width), lambda i: (i, 0))
    full = lambda a: pl.BlockSpec(a.shape, lambda i: (0, 0))
    return pl.pallas_call(
        _out_kernel,
        grid=(n // tm,),
        in_specs=[row(ATT_W), row(GLA_W), full(wa), full(wg), row(d), full(post_norm_w)],
        out_specs=row(d),
        out_shape=jax.ShapeDtypeStruct((n, d), F32),
        compiler_params=pltpu.CompilerParams(dimension_semantics=("arbitrary",),
                                             vmem_limit_bytes=VMEM_LIMIT),
        name="out_proj_postnorm",
    )(att2, gla2, wa, wg, x2, post_norm_w)


def kernel(x, positions, w_in, w_gla_gate_up, b_gla_gate, gla_norm_w, w_out, pre_norm_w, post_norm_w):
    b, s, d = x.shape
    n = b * s
    cos8, sin8 = _rope_tables(positions)
    aqt, ak, avt, iqt, ik, iwt, ag, gqk, gv, glog, gg = _project(x, pre_norm_w.reshape(1, d), cos8, sin8, w_in,
                                                                  w_gla_gate_up, b_gla_gate)
    att = _dsa_attention(aqt, ak, avt, iqt, ik, iwt, ag)
    gla = _gla(gqk, gv, glog, gg, gla_norm_w)
    out = _out_project(att.reshape(n, ATT_W), gla.reshape(n, GLA_W), w_out, x.reshape(n, d),
                       post_norm_w.reshape(1, d))
    return out.reshape(b, s, d)
```

```python
import functools

import jax
import jax.numpy as jnp
import numpy as np
from jax import lax
from jax.experimental import pallas as pl
from jax.experimental.pallas import tpu as pltpu

F32 = jnp.float32
BF16 = jnp.bfloat16
I32 = jnp.int32

ATT_HEADS = 8
ATT_HD = 64
ATT_W = ATT_HEADS * ATT_HD
IDX_HEADS = 4
IDX_HD = 64
IDX_QW = IDX_HEADS * IDX_HD
TOPK_MAX = 256
GLA_HEADS = 4
GLA_DK = 64
GLA_DV = 128
GLA_QKW = GLA_HEADS * GLA_DK
GLA_W = GLA_HEADS * GLA_DV
GLA_GATE_RANK = 16
GLA_TAU = 16.0
GLA_CHUNK = 64
GLA_SUB = 8
GLA_STEP = 256
ROPE_THETA = 500000.0
ROT_DIM = ATT_HD // 4
ROT_HALF = ROT_DIM // 2
NORM_EPS = 1e-6
NEG = float(np.float32(-1e30))
LOG2E = 1.4426950408889634

SUBLANES = 8
LANES = 128
MXU_DIM = 256
VMEM_LIMIT = 56 * 1024 * 1024

PROJ_TM = 512
OUT_TM = 1024
DSA_TQ = 512
DSA_TK = 256
ATT_SQ = 512
COUNT_CHAINS = 4
ATT_VROWS = ATT_HD + 16
INT_MIN = -(2 ** 31)


def _nt_dot(a, b):
    return lax.dot_general(a, b, (((1,), (1,)), ((), ())), preferred_element_type=F32)


def _tn_dot(a, b):
    return lax.dot_general(a, b, (((0,), (0,)), ((), ())), preferred_element_type=F32)


def _rope_table_kernel(pos_ref, inv_ref, cos_ref, sin_ref):
    ang = pos_ref[...].astype(F32) * inv_ref[...]
    cos_ref[...] = jnp.cos(ang)
    sin_ref[...] = jnp.sin(ang)


def _rope_tables(positions):
    n = positions.size
    inv = ROPE_THETA ** (-jnp.arange(0, ROT_DIM, 2, dtype=F32) / ROT_DIM)
    return pl.pallas_call(
        _rope_table_kernel,
        out_shape=(jax.ShapeDtypeStruct((ROT_HALF, n), F32),) * 2,
        name="rope_tables",
    )(positions.reshape(1, n), inv.reshape(ROT_HALF, 1))


def _proj_kernel(x_ref, pnw_ref, cos_ref, sin_ref,
                 wt_all, w_ag, w_gqk, w_gv, w_gg,
                 wupt_hi, wupt_lo, bup_ref,
                 aqt_o, ak_o, avt_o, iqt_o, ik_o, iwt_o, ag_o, gqk_o, gv_o, glog_o, gg_o):
    x = x_ref[0]
    ms = jnp.mean(x * x, axis=-1, keepdims=True)
    h = (x * lax.rsqrt(ms + NORM_EPS)) * pnw_ref[...]
    hb = h.astype(BF16)
    cos8, sin8 = cos_ref[...], sin_ref[...]

    def mm(w_ref):
        return jnp.dot(hb, w_ref[...], preferred_element_type=F32)

    def rope_t(yt):
        pieces = []
        for hd in range(yt.shape[0] // ATT_HD):
            rows = yt[hd * ATT_HD:(hd + 1) * ATT_HD]
            x1, x2 = rows[:ROT_HALF], rows[ROT_HALF:ROT_DIM]
            pieces += [x1 * cos8 - x2 * sin8, x2 * cos8 + x1 * sin8, rows[ROT_DIM:]]
        return jnp.concatenate(pieces, axis=0)

    yt = _nt_dot(wt_all[...], hb)
    ikt, o = yt[:LANES], LANES
    aq_t, o = yt[o:o + ATT_W], o + ATT_W
    ak_t, o = yt[o:o + ATT_W], o + ATT_W
    iq_t, o = yt[o:o + IDX_QW], o + IDX_QW
    av_t = yt[o:o + ATT_W]
    iw_row = IDX_HD + GLA_GATE_RANK
    iwt_o[0] = ikt[iw_row:iw_row + SUBLANES] * (IDX_HEADS ** -0.5 * IDX_HD ** -0.5)
    aqt_o[0] = (rope_t(aq_t) * (ATT_HD ** -0.5 * LOG2E)).astype(BF16)
    ik_o[0] = jnp.concatenate([rope_t(ikt[:IDX_HD]), jnp.zeros_like(ikt[IDX_HD:])], axis=0).T.astype(BF16)
    ak_o[0] = rope_t(ak_t).T.astype(BF16)
    ikt_hi = ikt.astype(BF16)
    ikt_lo = (ikt - ikt_hi.astype(F32)).astype(BF16)
    zt = (jnp.dot(wupt_hi[...], ikt_hi, preferred_element_type=F32)
          + jnp.dot(wupt_lo[...], ikt_hi, preferred_element_type=F32)
          + jnp.dot(wupt_hi[...], ikt_lo, preferred_element_type=F32))
    a_gate = mm(w_ag)
    z = zt.T + bup_ref[...]
    glog_o[0] = (jnp.minimum(z, 0.0) - jnp.log1p(jnp.exp(-jnp.abs(z)))) * (1.0 / GLA_TAU)
    g_gate = mm(w_gg)
    ag_o[0] = a_gate * jax.nn.sigmoid(a_gate)
    iqt_o[0] = rope_t(iq_t).astype(BF16)
    gg_o[0] = g_gate * jax.nn.sigmoid(g_gate)
    gqk = mm(w_gqk)
    lane = lax.broadcasted_iota(I32, gqk.shape, 1)
    gqk_o[0] = jnp.where(lane < GLA_QKW, gqk * (GLA_DK ** -0.5), gqk)
    avt = av_t
    ones_rows = jnp.ones((ATT_VROWS - ATT_HD, avt.shape[1]), F32)
    avt = jnp.concatenate([piece for hd in range(ATT_HEADS)
                           for piece in (avt[hd * ATT_HD:(hd + 1) * ATT_HD], ones_rows)], axis=0).astype(BF16)
    for c in range(avt_o.shape[1]):
        avt_o[0, c] = avt[:, c * DSA_TK:(c + 1) * DSA_TK]
    gv_o[0] = mm(w_gv)


def _split_w_in(w_in):
    sizes = (ATT_W, ATT_W, ATT_W, IDX_QW, IDX_HD, IDX_HEADS, ATT_W,
             GLA_QKW, GLA_QKW, GLA_W, GLA_GATE_RANK, GLA_W)
    offs = np.cumsum((0,) + sizes)
    seg = [w_in[:, offs[i]:offs[i + 1]] for i in range(len(sizes))]
    a_q, a_k, a_v, i_q, i_k, i_w, a_g, g_q, g_k, g_v, g_lr, g_g = seg
    d = w_in.shape[0]
    pad = lambda w, n: jnp.concatenate([w, jnp.zeros((d, n - w.shape[1]), w.dtype)], axis=1)
    small = pad(jnp.concatenate([i_k, g_lr, i_w], axis=1), LANES)
    transposed = jnp.concatenate([small, a_q, a_k, i_q, a_v], axis=1).T
    natural = [a_g, jnp.concatenate([g_q, g_k], axis=1), g_v, g_g]
    return [transposed.astype(BF16)], [w.astype(BF16) for w in natural]


def _project(x, pre_norm_w, cos8, sin8, w_in, w_up, b_up):
    b, s, d = x.shape
    wupt = jnp.concatenate([jnp.zeros((GLA_QKW, IDX_HD), F32), w_up.T,
                            jnp.zeros((GLA_QKW, LANES - IDX_HD - GLA_GATE_RANK), F32)], axis=1)
    wupt_hi = wupt.astype(BF16)
    wupt_lo = (wupt - wupt_hi.astype(F32)).astype(BF16)
    gate_ops = [wupt_hi, wupt_lo, b_up.reshape(1, GLA_QKW)]
    tm = min(PROJ_TM, s)
    nt = s // tm
    cpt = tm // DSA_TK
    wts, wns = _split_w_in(w_in)
    full = lambda a: pl.BlockSpec(a.shape, lambda bi, i: (0,) * a.ndim)
    nat = lambda width: pl.BlockSpec((1, tm, width), lambda bi, i: (bi, i, 0))
    tr = lambda rows: pl.BlockSpec((1, rows, tm), lambda bi, i: (bi, 0, i))
    tab = pl.BlockSpec((ROT_HALF, tm), lambda bi, i: (0, bi * nt + i))
    vrows = ATT_HEADS * ATT_VROWS
    out_specs = [tr(ATT_W), nat(ATT_W), pl.BlockSpec((1, cpt, vrows, DSA_TK), lambda bi, i: (bi, i, 0, 0)),
                 tr(IDX_QW), nat(LANES), tr(SUBLANES),
                 nat(ATT_W), nat(2 * GLA_QKW), nat(GLA_W), nat(GLA_QKW), nat(GLA_W)]
    sds = jax.ShapeDtypeStruct
    out_shape = [sds((b, ATT_W, s), BF16), sds((b, s, ATT_W), BF16), sds((b, s // DSA_TK, vrows, DSA_TK), BF16),
                 sds((b, IDX_QW, s), BF16), sds((b, s, LANES), BF16), sds((b, SUBLANES, s), F32),
                 sds((b, s, ATT_W), F32), sds((b, s, 2 * GLA_QKW), F32), sds((b, s, GLA_W), F32),
                 sds((b, s, GLA_QKW), F32), sds((b, s, GLA_W), F32)]
    return pl.pallas_call(
        _proj_kernel,
        grid=(b, nt),
        in_specs=[nat(d), full(pre_norm_w), tab, tab] + [full(w) for w in wts + wns + gate_ops],
        out_specs=out_specs,
        out_shape=out_shape,
        compiler_params=pltpu.CompilerParams(dimension_semantics=("arbitrary", "arbitrary"),
                                             vmem_limit_bytes=VMEM_LIMIT),
        name="prenorm_in_proj",
    )(x, pre_norm_w, cos8, sin8, *wts, *wns, *gate_ops)


def _key_to_f32(key):
    return pltpu.bitcast(key ^ ((key >> 31) & 0x7FFFFFFF), F32)


def _dsa_kernel(aqt_ref, ak_ref, avt_ref, iqt_ref, ik_ref, iwt_ref, ag_ref, out_ref,
                score_scr, coarse_scr, bias_scr, qbd_scr, m_scr, acc_scr, *, ksel):
    tq, tk = DSA_TQ, DSA_TK
    i = pl.program_id(1)
    nck = (i * tq + tq + tk - 1) // tk
    groups = tk // SUBLANES

    key_iota = lax.broadcasted_iota(I32, (tk, tq), 0)
    q_pos = i * tq + lax.broadcasted_iota(I32, (tk, tq), 1)

    iqt = iqt_ref[0]
    q_cat = jnp.concatenate([iqt[h * IDX_HD:(h + 1) * IDX_HD] for h in range(IDX_HEADS)], axis=1)
    q_cat = jnp.concatenate([q_cat, jnp.zeros((LANES - IDX_HD, IDX_HEADS * tq), BF16)], axis=0)
    iwt = iwt_ref[0]
    w_rows = [iwt[h:h + 1, :] for h in range(IDX_HEADS)]

    n_pair = (nck + 1) // 2

    def score_pair(c2, carry):
        chunks = (2 * c2, 2 * c2 + 1)
        rels = [jnp.dot(ik_ref[0, pl.ds(pl.multiple_of(c * tk, tk), tk), :], q_cat,
                        preferred_element_type=F32) for c in chunks]
        for c, rel in zip(chunks, rels):
            sc = w_rows[0] * jnp.maximum(rel[:, 0:tq], 0.0)
            for h in range(1, IDX_HEADS):
                sc = sc + w_rows[h] * jnp.maximum(rel[:, h * tq:(h + 1) * tq], 0.0)
            sc = jnp.where(c * tk + key_iota <= q_pos, sc, NEG)
            score_scr[c] = sc
            coarse_scr[c] = sc.astype(BF16)
        return carry

    lax.fori_loop(0, n_pair, score_pair, 0)

    def count(pred):
        def body(c2, acc):
            sc = score_scr[pl.ds(2 * c2, 2)].reshape(COUNT_CHAINS, 2 * groups // COUNT_CHAINS, SUBLANES, tq)
            return acc + jnp.sum(jnp.where(pred(sc), 1, 0), axis=1)
        acc = lax.fori_loop(0, n_pair, body, jnp.zeros((COUNT_CHAINS, SUBLANES, tq), I32))
        return jnp.sum(acc.reshape(COUNT_CHAINS * SUBLANES, tq), axis=0, keepdims=True)

    def count_coarse(cand_b):
        packed_rows = 2 * SUBLANES
        def body(c2, acc):
            hi = coarse_scr[pl.ds(2 * c2, 2)].reshape(COUNT_CHAINS, 2 * tk // (COUNT_CHAINS * packed_rows),
                                                      packed_rows, tq)
            for g in range(hi.shape[1]):
                acc = acc + jnp.where(hi[:, g] >= cand_b[None], jnp.ones((), BF16), jnp.zeros((), BF16))
            return acc
        acc = lax.fori_loop(0, n_pair, body, jnp.zeros((COUNT_CHAINS, packed_rows, tq), BF16))
        return jnp.sum(acc.astype(F32).reshape(COUNT_CHAINS * packed_rows, tq), axis=0, keepdims=True)

    def bisect_coarse(it, prefix):
        cand = prefix + lax.shift_left(jnp.int32(1), 15 - it)
        pattern = cand ^ ((cand >> 31) & 0x7FFF)
        cand_b = pltpu.bitcast(lax.shift_left(pattern, 16), F32).astype(BF16)
        cnt = count_coarse(cand_b)
        return jnp.where(cnt >= ksel, cand, prefix)

    assert (tk // (COUNT_CHAINS * SUBLANES)) * (coarse_scr.shape[0] // 2) <= 2 ** 8, "bf16 counts must stay exact"
    key16 = lax.fori_loop(0, 16, bisect_coarse, jnp.full((2 * SUBLANES, tq), -(2 ** 15), I32))[:SUBLANES]
    key_p = lax.shift_left(key16, 16) | ((key16 >> 31) & 0xFFFF)
    key_lo = key_p - 2 ** 16

    def bisect(it, prefix):
        cand = prefix + lax.shift_left(jnp.int32(1), 16 - it)
        cand_f = _key_to_f32(cand)
        cnt = count(lambda sc: sc >= cand_f[None])
        return jnp.where(cnt >= ksel, cand, prefix)

    vstar = _key_to_f32(lax.fori_loop(0, 17, bisect, key_lo))
    need_f = (ksel - count(lambda sc: sc > vstar[None])).astype(F32)

    tri = jnp.where(lax.broadcasted_iota(I32, (tk, tk), 1) <= lax.broadcasted_iota(I32, (tk, tk), 0),
                    1.0, 0.0).astype(BF16)

    def select_pair(c2, run):
        chunks = (2 * c2, 2 * c2 + 1)
        scs = [score_scr[c] for c in chunks]
        eqs = [sc == vstar[0:1] for sc in scs]
        ranks = [jnp.dot(tri, jnp.where(eq, 1.0, 0.0).astype(BF16), preferred_element_type=F32) for eq in eqs]
        for c, sc, eq, rank in zip(chunks, scs, eqs, ranks):
            rank = rank + run
            tie_bias = jnp.where(rank <= need_f, jnp.where(c * tk + key_iota <= q_pos, 0.0, NEG), NEG)
            bias_scr[c] = jnp.where(sc > vstar[0:1], 0.0, jnp.where(eq, tie_bias, NEG))
            run = rank[tk - 1:tk, :]
        return run

    lax.fori_loop(0, n_pair, select_pair, jnp.zeros((1, tq), F32))

    heads_per_half = MXU_DIM // ATT_HD
    n_half = ATT_W // MXU_DIM
    row_head = lax.broadcasted_iota(I32, (MXU_DIM, tq), 0) // ATT_HD
    for h in range(ATT_HEADS):
        half, hh = divmod(h, heads_per_half)
        qh = aqt_ref[0, half * MXU_DIM:(half + 1) * MXU_DIM, :]
        qbd_scr[half, :, hh * tq:(hh + 1) * tq] = jnp.where(row_head == hh, qh, jnp.zeros_like(qh))
    m_scr[...] = jnp.full_like(m_scr, NEG)
    acc_scr[...] = jnp.zeros_like(acc_scr)

    def att_pair(c2, carry):
        s_all = {}
        for sub in range(2):
            for half in range(n_half):
                rows = pl.ds(pl.multiple_of((2 * c2 + sub) * tk, tk), tk)
                s_all[sub, half] = jnp.dot(ak_ref[0, rows, half * MXU_DIM:(half + 1) * MXU_DIM], qbd_scr[half],
                                           preferred_element_type=F32)
        for sub in range(2):
            for half in range(n_half):
                for hh in range(heads_per_half):
                    h = half * heads_per_half + hh
                    vt = avt_ref[0, 2 * c2 + sub, h * ATT_VROWS:(h + 1) * ATT_VROWS, :]
                    for qs in range(tq // ATT_SQ):
                        ql = slice(qs * ATT_SQ, (qs + 1) * ATT_SQ)
                        s = s_all[sub, half][:, hh * tq + qs * ATT_SQ:hh * tq + (qs + 1) * ATT_SQ] + \
                            bias_scr[2 * c2 + sub, :, ql]
                        m_old = m_scr[h, :, ql]
                        m_new = jnp.maximum(m_old, jnp.max(s, axis=0, keepdims=True))
                        alpha = jnp.exp2(m_old - m_new)
                        p = jnp.exp2(s - m_new).astype(BF16)
                        acc_scr[h, :, ql] = alpha * acc_scr[h, :, ql] + jnp.dot(vt, p, preferred_element_type=F32)
                        m_scr[h, :, ql] = m_new
        return carry

    lax.fori_loop(0, n_pair, att_pair, 0)

    att_t = jnp.concatenate([acc_scr[h, :ATT_HD] * (1.0 / acc_scr[h, ATT_HD:ATT_HD + 1])
                             for h in range(ATT_HEADS)], axis=0)
    out_ref[0] = (att_t.T * ag_ref[0]).astype(BF16)


def _dsa_attention(aqt, ak, avt, iqt, ik, iwt, ag):
    b, s, _ = ak.shape
    tq, tk = DSA_TQ, DSA_TK
    nck = s // tk
    ksel = min(TOPK_MAX, s // 4)
    qt = lambda rows: pl.BlockSpec((1, rows, tq), lambda bi, i: (bi, 0, i))
    seq = lambda width: pl.BlockSpec((1, s, width), lambda bi, i: (bi, 0, 0))
    qn = pl.BlockSpec((1, tq, ATT_W), lambda bi, i: (bi, i, 0))
    return pl.pallas_call(
        functools.partial(_dsa_kernel, ksel=ksel),
        grid=(b, s // tq),
        in_specs=[qt(ATT_W), seq(ATT_W),
                  pl.BlockSpec((1, nck, ATT_HEADS * ATT_VROWS, tk), lambda bi, i: (bi, 0, 0, 0)),
                  qt(IDX_QW), seq(LANES), qt(SUBLANES), qn],
        out_specs=qn,
        out_shape=jax.ShapeDtypeStruct((b, s, ATT_W), BF16),
        scratch_shapes=[pltpu.VMEM((nck, tk, tq), F32), pltpu.VMEM((nck, tk, tq), BF16),
                        pltpu.VMEM((nck, tk, tq), F32),
                        pltpu.VMEM((ATT_W // MXU_DIM, MXU_DIM, (MXU_DIM // ATT_HD) * tq), BF16),
                        pltpu.VMEM((ATT_HEADS, 1, tq), F32),
                        pltpu.VMEM((ATT_HEADS, ATT_VROWS, tq), F32)],
        compiler_params=pltpu.CompilerParams(dimension_semantics=("arbitrary", "arbitrary"),
                                             vmem_limit_bytes=VMEM_LIMIT),
        name="dsa_attention",
    )(aqt, ak, avt, iqt, ik, iwt, ag)


def _gla_kernel(gqk_ref, gv_ref, glog_ref, gg_ref, nw_ref, out_ref, state_scr):
    c_len, sub = GLA_CHUNK, GLA_SUB
    nb = gqk_ref.shape[0]
    n_z = c_len * sub

    @pl.when(pl.program_id(0) == 0)
    def _():
        state_scr[...] = jnp.zeros_like(state_scr)

    t_i = lax.broadcasted_iota(I32, (c_len, GLA_QKW), 0)
    s_i = lax.broadcasted_iota(I32, (sub, GLA_QKW), 0)
    j_i = lax.broadcasted_iota(I32, (c_len, GLA_QKW), 1) % c_len
    tril = jnp.where(lax.broadcasted_iota(I32, (c_len, c_len), 0)
                     >= lax.broadcasted_iota(I32, (c_len, c_len), 1), 1.0, 0.0).astype(BF16)
    lane_head = lax.broadcasted_iota(I32, (c_len, GLA_QKW), 1) // GLA_DK

    def stack_heads(a):
        return jnp.concatenate([jnp.where(lane_head == h, a, 0.0) for h in range(GLA_HEADS)],
                               axis=0).astype(BF16)

    zr = lax.broadcasted_iota(I32, (n_z, GLA_QKW), 0)
    zc = lax.broadcasted_iota(I32, (n_z, GLA_QKW), 1)
    pick = (zc % c_len) == (zr // (sub * sub)) * sub + zr % sub
    hr = lax.broadcasted_iota(I32, (GLA_QKW, GLA_QKW), 0) // GLA_DK
    hc = lax.broadcasted_iota(I32, (GLA_QKW, GLA_QKW), 1) // c_len
    head_rep = jnp.where(hr == hc, 1.0, 0.0).astype(BF16)
    sr = lax.broadcasted_iota(I32, (c_len, n_z), 0)
    sc_ = lax.broadcasted_iota(I32, (c_len, n_z), 1)
    row_sel = jnp.where(sr == sc_ // sub, 1.0, 0.0).astype(BF16)
    vr = lax.broadcasted_iota(I32, (GLA_HEADS * c_len, GLA_W), 0) // c_len
    vc = lax.broadcasted_iota(I32, (GLA_HEADS * c_len, GLA_W), 1) // GLA_DV
    v_diag = vr == vc
    spans = [c_len >> (lv + 1) for lv in range((c_len // sub).bit_length() - 1)]

    units = range(nb * (gqk_ref.shape[1] // c_len))
    bat = lambda u: u % nb
    tsl = lambda u: slice((u // nb) * c_len, (u // nb + 1) * c_len)
    cur_state = {bi: state_scr[bi] for bi in range(nb)}
    q, k, vb, b, b_last, state_t, o, a, rep = ({} for _ in range(9))

    for u in units:
        gqk = gqk_ref[bat(u), tsl(u), :]
        q[u], k[u] = gqk[:, :GLA_QKW], gqk[:, GLA_QKW:]
        vb[u] = gv_ref[bat(u), tsl(u), :].astype(BF16)
        g = glog_ref[bat(u), tsl(u), :]
        g1 = g.astype(BF16)
        r1 = g - g1.astype(F32)
        g2 = r1.astype(BF16)
        g3 = (r1 - g2.astype(F32)).astype(BF16)
        b[u] = (jnp.dot(tril, g1, preferred_element_type=F32) + jnp.dot(tril, g2, preferred_element_type=F32)
                 + jnp.dot(tril, g3, preferred_element_type=F32))
        b_last[u] = b[u][c_len - 1:c_len, :]

    for u in units:
        state_t[u] = cur_state[bat(u)]
        o_stack = _nt_dot(stack_heads(q[u] * jnp.exp(b[u])), state_t[u].astype(BF16))
        o[u] = jnp.concatenate([o_stack[h * c_len:(h + 1) * c_len] for h in range(GLA_HEADS)], axis=1)

        k_bd = stack_heads(k[u] * jnp.exp(b_last[u] - b[u]))
        v_st = jnp.concatenate([vb[u][:, h * GLA_DV:(h + 1) * GLA_DV] for h in range(GLA_HEADS)], axis=0)
        cur_state[bat(u)] = jnp.exp(b_last[u]) * state_t[u] + _tn_dot(v_st, k_bd)

        for span in spans:
            blk = 2 * span
            ref = jnp.concatenate([jnp.broadcast_to(b[u][m * blk + span - 1:m * blk + span, :], (blk, GLA_QKW))
                                   for m in range(c_len // blk)], axis=0)
            upper = (t_i % blk) >= span
            q_l = q[u] * jnp.exp(jnp.where(upper, b[u] - ref, NEG))
            k_l = k[u] * jnp.exp(jnp.where(upper, NEG, ref - b[u]))
            a_l = _nt_dot(q_l.astype(BF16), stack_heads(k_l))
            if blk < c_len:
                a_l = jnp.where(t_i // blk == j_i // blk, a_l, 0.0)
            a[u] = a_l if u not in a else a[u] + a_l

    for bi in range(nb):
        state_scr[bi] = cur_state[bi]

    for u in units:
        zs = []
        for t in range(c_len):
            blk0 = (t // sub) * sub
            rs = slice(blk0, blk0 + sub)
            dec = jnp.exp(jnp.where(s_i <= t - blk0, b[u][t:t + 1, :] - b[u][rs], NEG))
            zs.append(q[u][t:t + 1, :] * k[u][rs] * dec)
        zmat = jnp.concatenate(zs, axis=0).astype(BF16)
        rep[u] = jnp.dot(zmat, head_rep, preferred_element_type=F32)

    for u in units:
        a_all = a[u] + jnp.dot(row_sel, jnp.where(pick, rep[u], 0.0).astype(BF16), preferred_element_type=F32)
        v_bd = jnp.where(v_diag, jnp.concatenate([vb[u]] * GLA_HEADS, axis=0), jnp.zeros((), BF16))
        o_u = o[u] + jnp.dot(a_all.astype(BF16), v_bd, preferred_element_type=F32)
        outs = []
        for h in range(GLA_HEADS):
            oh = o_u[:, h * GLA_DV:(h + 1) * GLA_DV]
            oh = oh * lax.rsqrt(jnp.mean(oh * oh, axis=-1, keepdims=True) + NORM_EPS)
            outs.append(oh * nw_ref[...])
        out_ref[bat(u), tsl(u), :] = (jnp.concatenate(outs, axis=1) * gg_ref[bat(u), tsl(u), :]).astype(BF16)


def _gla(gqk, gv, glog, gg, norm_w):
    b, s, _ = gqk.shape
    c_len = GLA_CHUNK
    step = min(GLA_STEP, s)
    blk = lambda width: pl.BlockSpec((b, step, width), lambda c: (0, c, 0))
    nw2 = norm_w.reshape(1, GLA_DV)
    return pl.pallas_call(
        _gla_kernel,
        grid=(s // step,),
        in_specs=[blk(2 * GLA_QKW), blk(GLA_W), blk(GLA_QKW), blk(GLA_W),
                  pl.BlockSpec(nw2.shape, lambda c: (0, 0))],
        out_specs=blk(GLA_W),
        out_shape=jax.ShapeDtypeStruct((b, s, GLA_W), BF16),
        scratch_shapes=[pltpu.VMEM((b, GLA_DV, GLA_QKW), F32)],
        compiler_params=pltpu.CompilerParams(dimension_semantics=("arbitrary",),
                                             vmem_limit_bytes=VMEM_LIMIT),
        name="gla_chunked",
    )(gqk, gv, glog, gg, nw2)


def _out_kernel(att_ref, gla_ref, wa_ref, wg_ref, x_ref, nw_ref, out_ref):
    mixed = jnp.dot(att_ref[...], wa_ref[...], preferred_element_type=F32) + \
        jnp.dot(gla_ref[...], wg_ref[...], preferred_element_type=F32)
    y = mixed * lax.rsqrt(jnp.mean(mixed * mixed, axis=-1, keepdims=True) + NORM_EPS)
    out_ref[...] = x_ref[...] + y * nw_ref[...]


def _out_project(att2, gla2, w_out, x2, post_norm_w):
    n, d = x2.shape
    tm = min(OUT_TM, n)
    wa, wg = w_out[:ATT_W].astype(BF16), w_out[ATT_W:].astype(BF16)
    row = lambda width: pl.BlockSpec((tm, width), lambda i: (i, 0))
    full = lambda a: pl.BlockSpec(a.shape, lambda i: (0, 0))
    return pl.pallas_call(
        _out_kernel,
        grid=(n // tm,),
        in_specs=[row(ATT_W), row(GLA_W), full(wa), full(wg), row(d), full(post_norm_w)],
        out_specs=row(d),
        out_shape=jax.ShapeDtypeStruct((n, d), F32),
        compiler_params=pltpu.CompilerParams(dimension_semantics=("arbitrary",),
                                             vmem_limit_bytes=VMEM_LIMIT),
        name="out_proj_postnorm",
    )(att2, gla2, wa, wg, x2, post_norm_w)


def kernel(x, positions, w_in, w_gla_gate_up, b_gla_gate, gla_norm_w, w_out, pre_norm_w, post_norm_w):
    b, s, d = x.shape
    n = b * s
    assert w_in.shape[0] == d and w_out.shape == (ATT_W + GLA_W, d)
    assert s % PROJ_TM == 0 and s % DSA_TQ == 0 and (s // DSA_TK) % 2 == 0 and DSA_TQ % ATT_SQ == 0
    assert s % min(GLA_STEP, s) == 0 and min(GLA_STEP, s) % GLA_CHUNK == 0 and n % min(OUT_TM, n) == 0
    cos8, sin8 = _rope_tables(positions)
    aqt, ak, avt, iqt, ik, iwt, ag, gqk, gv, glog, gg = _project(x, pre_norm_w.reshape(1, d), cos8, sin8, w_in,
                                                                  w_gla_gate_up, b_gla_gate)
    att = _dsa_attention(aqt, ak, avt, iqt, ik, iwt, ag)
    gla = _gla(gqk, gv, glog, gg, gla_norm_w)
    out = _out_project(att.reshape(n, ATT_W), gla.reshape(n, GLA_W), w_out, x.reshape(n, d),
                       post_norm_w.reshape(1, d))
    return out.reshape(b, s, d)
```

```python
import functools

import jax
import jax.numpy as jnp
import numpy as np
from jax import lax
from jax.experimental import pallas as pl
from jax.experimental.pallas import tpu as pltpu

F32 = jnp.float32
BF16 = jnp.bfloat16
I32 = jnp.int32

ATT_HEADS = 8
ATT_HD = 64
ATT_W = ATT_HEADS * ATT_HD
IDX_HEADS = 4
IDX_HD = 64
IDX_QW = IDX_HEADS * IDX_HD
TOPK_MAX = 256
GLA_HEADS = 4
GLA_DK = 64
GLA_DV = 128
GLA_QKW = GLA_HEADS * GLA_DK
GLA_W = GLA_HEADS * GLA_DV
GLA_GATE_RANK = 16
GLA_TAU = 16.0
GLA_CHUNK = 64
GLA_SUB = 8
GLA_STEP = 256
ROPE_THETA = 500000.0
ROT_DIM = ATT_HD // 4
ROT_HALF = ROT_DIM // 2
NORM_EPS = 1e-6
NEG = float(np.float32(-1e30))
LOG2E = 1.4426950408889634

SUBLANES = 8
LANES = 128
MXU_DIM = 256
VMEM_LIMIT = 56 * 1024 * 1024

PROJ_TM = 1024
OUT_TM = 1024
DSA_TQ = 512
DSA_TK = 256
ATT_SQ = 512
COUNT_CHAINS = 4
ATT_VROWS = ATT_HD + 16
INT_MIN = -(2 ** 31)


def _nt_dot(a, b):
    return lax.dot_general(a, b, (((1,), (1,)), ((), ())), preferred_element_type=F32)


def _tn_dot(a, b):
    return lax.dot_general(a, b, (((0,), (0,)), ((), ())), preferred_element_type=F32)


def _rope_table_kernel(pos_ref, inv_ref, cos_ref, sin_ref):
    ang = pos_ref[...].astype(F32) * inv_ref[...]
    cos_ref[...] = jnp.cos(ang)
    sin_ref[...] = jnp.sin(ang)


def _rope_tables(positions):
    n = positions.size
    inv = ROPE_THETA ** (-jnp.arange(0, ROT_DIM, 2, dtype=F32) / ROT_DIM)
    return pl.pallas_call(
        _rope_table_kernel,
        out_shape=(jax.ShapeDtypeStruct((ROT_HALF, n), F32),) * 2,
        name="rope_tables",
    )(positions.reshape(1, n), inv.reshape(ROT_HALF, 1))


def _proj_kernel(x_ref, pnw_ref, cos_ref, sin_ref,
                 wt_all, w_ag, w_gqk, w_gv, w_gg,
                 wupt_hi, wupt_lo, bup_ref,
                 aqt_o, ak_o, avt_o, iqt_o, ik_o, iwt_o, ag_o, gqk_o, gv_o, glog_o, gg_o):
    x = x_ref[0]
    ms = jnp.mean(x * x, axis=-1, keepdims=True)
    h = (x * lax.rsqrt(ms + NORM_EPS)) * pnw_ref[...]
    hb = h.astype(BF16)
    cos8, sin8 = cos_ref[...], sin_ref[...]

    def mm(w_ref):
        return jnp.dot(hb, w_ref[...], preferred_element_type=F32)

    def rope_t(yt):
        pieces = []
        for hd in range(yt.shape[0] // ATT_HD):
            rows = yt[hd * ATT_HD:(hd + 1) * ATT_HD]
            x1, x2 = rows[:ROT_HALF], rows[ROT_HALF:ROT_DIM]
            pieces += [x1 * cos8 - x2 * sin8, x2 * cos8 + x1 * sin8, rows[ROT_DIM:]]
        return jnp.concatenate(pieces, axis=0)

    yt = _nt_dot(wt_all[...], hb)
    ikt, o = yt[:LANES], LANES
    aq_t, o = yt[o:o + ATT_W], o + ATT_W
    ak_t, o = yt[o:o + ATT_W], o + ATT_W
    iq_t, o = yt[o:o + IDX_QW], o + IDX_QW
    av_t = yt[o:o + ATT_W]
    iw_row = IDX_HD + GLA_GATE_RANK
    iwt_o[0] = ikt[iw_row:iw_row + SUBLANES] * (IDX_HEADS ** -0.5 * IDX_HD ** -0.5)
    aqt_o[0] = (rope_t(aq_t) * (ATT_HD ** -0.5 * LOG2E)).astype(BF16)
    ik_o[0] = jnp.concatenate([rope_t(ikt[:IDX_HD]), jnp.zeros_like(ikt[IDX_HD:])], axis=0).T.astype(BF16)
    ak_o[0] = rope_t(ak_t).T.astype(BF16)
    ikt_hi = ikt.astype(BF16)
    ikt_lo = (ikt - ikt_hi.astype(F32)).astype(BF16)
    zt = (jnp.dot(wupt_hi[...], ikt_hi, preferred_element_type=F32)
          + jnp.dot(wupt_lo[...], ikt_hi, preferred_element_type=F32)
          + jnp.dot(wupt_hi[...], ikt_lo, preferred_element_type=F32))
    a_gate = mm(w_ag)
    z = zt.T + bup_ref[...]
    glog_o[0] = (jnp.minimum(z, 0.0) - jnp.log1p(jnp.exp(-jnp.abs(z)))) * (1.0 / GLA_TAU)
    g_gate = mm(w_gg)
    ag_o[0] = a_gate * jax.nn.sigmoid(a_gate)
    iqt_o[0] = rope_t(iq_t).astype(BF16)
    gg_o[0] = g_gate * jax.nn.sigmoid(g_gate)
    gqk = mm(w_gqk)
    lane = lax.broadcasted_iota(I32, gqk.shape, 1)
    gqk_o[0] = jnp.where(lane < GLA_QKW, gqk * (GLA_DK ** -0.5), gqk)
    avt = av_t
    ones_rows = jnp.ones((ATT_VROWS - ATT_HD, avt.shape[1]), F32)
    avt = jnp.concatenate([piece for hd in range(ATT_HEADS)
                           for piece in (avt[hd * ATT_HD:(hd + 1) * ATT_HD], ones_rows)], axis=0).astype(BF16)
    for c in range(avt_o.shape[1]):
        avt_o[0, c] = avt[:, c * DSA_TK:(c + 1) * DSA_TK]
    gv_o[0] = mm(w_gv)


def _split_w_in(w_in):
    sizes = (ATT_W, ATT_W, ATT_W, IDX_QW, IDX_HD, IDX_HEADS, ATT_W,
             GLA_QKW, GLA_QKW, GLA_W, GLA_GATE_RANK, GLA_W)
    offs = np.cumsum((0,) + sizes)
    seg = [w_in[:, offs[i]:offs[i + 1]] for i in range(len(sizes))]
    a_q, a_k, a_v, i_q, i_k, i_w, a_g, g_q, g_k, g_v, g_lr, g_g = seg
    d = w_in.shape[0]
    pad = lambda w, n: jnp.concatenate([w, jnp.zeros((d, n - w.shape[1]), w.dtype)], axis=1)
    small = pad(jnp.concatenate([i_k, g_lr, i_w], axis=1), LANES)
    transposed = jnp.concatenate([small, a_q, a_k, i_q, a_v], axis=1).astype(BF16).T
    natural = [a_g, jnp.concatenate([g_q, g_k], axis=1), g_v, g_g]
    return [transposed], [w.astype(BF16) for w in natural]


def _project(x, pre_norm_w, cos8, sin8, w_in, w_up, b_up):
    b, s, d = x.shape
    wupt = jnp.concatenate([jnp.zeros((GLA_QKW, IDX_HD), F32), w_up.T,
                            jnp.zeros((GLA_QKW, LANES - IDX_HD - GLA_GATE_RANK), F32)], axis=1)
    wupt_hi = wupt.astype(BF16)
    wupt_lo = (wupt - wupt_hi.astype(F32)).astype(BF16)
    gate_ops = [wupt_hi, wupt_lo, b_up.reshape(1, GLA_QKW)]
    tm = min(PROJ_TM, s)
    nt = s // tm
    cpt = tm // DSA_TK
    wts, wns = _split_w_in(w_in)
    full = lambda a: pl.BlockSpec(a.shape, lambda bi, i: (0,) * a.ndim)
    nat = lambda width: pl.BlockSpec((1, tm, width), lambda bi, i: (bi, i, 0))
    tr = lambda rows: pl.BlockSpec((1, rows, tm), lambda bi, i: (bi, 0, i))
    tab = pl.BlockSpec((ROT_HALF, tm), lambda bi, i: (0, bi * nt + i))
    vrows = ATT_HEADS * ATT_VROWS
    out_specs = [tr(ATT_W), nat(ATT_W), pl.BlockSpec((1, cpt, vrows, DSA_TK), lambda bi, i: (bi, i, 0, 0)),
                 tr(IDX_QW), nat(LANES), tr(SUBLANES),
                 nat(ATT_W), nat(2 * GLA_QKW), nat(GLA_W), nat(GLA_QKW), nat(GLA_W)]
    sds = jax.ShapeDtypeStruct
    out_shape = [sds((b, ATT_W, s), BF16), sds((b, s, ATT_W), BF16), sds((b, s // DSA_TK, vrows, DSA_TK), BF16),
                 sds((b, IDX_QW, s), BF16), sds((b, s, LANES), BF16), sds((b, SUBLANES, s), F32),
                 sds((b, s, ATT_W), F32), sds((b, s, 2 * GLA_QKW), F32), sds((b, s, GLA_W), F32),
                 sds((b, s, GLA_QKW), F32), sds((b, s, GLA_W), F32)]
    return pl.pallas_call(
        _proj_kernel,
        grid=(b, nt),
        in_specs=[nat(d), full(pre_norm_w), tab, tab] + [full(w) for w in wts + wns + gate_ops],
        out_specs=out_specs,
        out_shape=out_shape,
        compiler_params=pltpu.CompilerParams(dimension_semantics=("arbitrary", "arbitrary"),
                                             vmem_limit_bytes=VMEM_LIMIT),
        name="prenorm_in_proj",
    )(x, pre_norm_w, cos8, sin8, *wts, *wns, *gate_ops)


def _key_to_f32(key):
    return pltpu.bitcast(key ^ ((key >> 31) & 0x7FFFFFFF), F32)


def _dsa_kernel(aqt_ref, ak_ref, avt_ref, iqt_ref, ik_ref, iwt_ref, ag_ref, out_ref,
                score_scr, coarse_scr, bias_scr, qbd_scr, m_scr, acc_scr, *, ksel):
    tq, tk = DSA_TQ, DSA_TK
    i = pl.program_id(1)
    nck = (i * tq + tq + tk - 1) // tk
    groups = tk // SUBLANES

    key_iota = lax.broadcasted_iota(I32, (tk, tq), 0)
    q_pos = i * tq + lax.broadcasted_iota(I32, (tk, tq), 1)

    iqt = iqt_ref[0]
    q_cat = jnp.concatenate([iqt[h * IDX_HD:(h + 1) * IDX_HD] for h in range(IDX_HEADS)], axis=1)
    q_cat = jnp.concatenate([q_cat, jnp.zeros((LANES - IDX_HD, IDX_HEADS * tq), BF16)], axis=0)
    iwt = iwt_ref[0]
    w_rows = [iwt[h:h + 1, :] for h in range(IDX_HEADS)]

    n_pair = (nck + 1) // 2

    def score_pair(c2, carry):
        chunks = (2 * c2, 2 * c2 + 1)
        rels = [jnp.dot(ik_ref[0, pl.ds(pl.multiple_of(c * tk, tk), tk), :], q_cat,
                        preferred_element_type=F32) for c in chunks]
        for c, rel in zip(chunks, rels):
            sc = w_rows[0] * jnp.maximum(rel[:, 0:tq], 0.0)
            for h in range(1, IDX_HEADS):
                sc = sc + w_rows[h] * jnp.maximum(rel[:, h * tq:(h + 1) * tq], 0.0)
            sc = jnp.where(c * tk + key_iota <= q_pos, sc, NEG)
            score_scr[c] = sc
            coarse_scr[c] = sc.astype(BF16)
        return carry

    lax.fori_loop(0, n_pair, score_pair, 0)

    def count(pred):
        def body(c2, acc):
            sc = score_scr[pl.ds(2 * c2, 2)].reshape(COUNT_CHAINS, 2 * groups // COUNT_CHAINS, SUBLANES, tq)
            return acc + jnp.sum(jnp.where(pred(sc), 1, 0), axis=1)
        acc = lax.fori_loop(0, n_pair, body, jnp.zeros((COUNT_CHAINS, SUBLANES, tq), I32))
        return jnp.sum(acc.reshape(COUNT_CHAINS * SUBLANES, tq), axis=0, keepdims=True)

    def count_coarse(cand_b):
        packed_rows = 2 * SUBLANES
        def body(c2, acc):
            hi = coarse_scr[pl.ds(2 * c2, 2)].reshape(COUNT_CHAINS, 2 * tk // (COUNT_CHAINS * packed_rows),
                                                      packed_rows, tq)
            for g in range(hi.shape[1]):
                acc = acc + jnp.where(hi[:, g] >= cand_b[None], jnp.ones((), BF16), jnp.zeros((), BF16))
            return acc
        acc = lax.fori_loop(0, n_pair, body, jnp.zeros((COUNT_CHAINS, packed_rows, tq), BF16))
        return jnp.sum(acc.astype(F32).reshape(COUNT_CHAINS * packed_rows, tq), axis=0, keepdims=True)

    def bisect_coarse(it, prefix):
        cand = prefix + lax.shift_left(jnp.int32(1), 15 - it)
        pattern = cand ^ ((cand >> 31) & 0x7FFF)
        cand_b = pltpu.bitcast(lax.shift_left(pattern, 16), F32).astype(BF16)
        cnt = count_coarse(cand_b)
        return jnp.where(cnt >= ksel, cand, prefix)

    assert (tk // (COUNT_CHAINS * SUBLANES)) * (coarse_scr.shape[0] // 2) <= 2 ** 8, "bf16 counts must stay exact"
    key16 = lax.fori_loop(0, 16, bisect_coarse, jnp.full((2 * SUBLANES, tq), -(2 ** 15), I32))[:SUBLANES]
    key_p = lax.shift_left(key16, 16) | ((key16 >> 31) & 0xFFFF)
    key_lo = key_p - 2 ** 16

    def bisect(it, prefix):
        cand = prefix + lax.shift_left(jnp.int32(1), 16 - it)
        cand_f = _key_to_f32(cand)
        cnt = count(lambda sc: sc >= cand_f[None])
        return jnp.where(cnt >= ksel, cand, prefix)

    vstar = _key_to_f32(lax.fori_loop(0, 17, bisect, key_lo))
    need_f = (ksel - count(lambda sc: sc > vstar[None])).astype(F32)

    tri = jnp.where(lax.broadcasted_iota(I32, (tk, tk), 1) <= lax.broadcasted_iota(I32, (tk, tk), 0),
                    1.0, 0.0).astype(BF16)

    def select_pair(c2, run):
        chunks = (2 * c2, 2 * c2 + 1)
        scs = [score_scr[c] for c in chunks]
        eqs = [sc == vstar[0:1] for sc in scs]
        ranks = [jnp.dot(tri, jnp.where(eq, 1.0, 0.0).astype(BF16), preferred_element_type=F32) for eq in eqs]
        for c, sc, eq, rank in zip(chunks, scs, eqs, ranks):
            rank = rank + run
            tie_bias = jnp.where(rank <= need_f, jnp.where(c * tk + key_iota <= q_pos, 0.0, NEG), NEG)
            bias_scr[c] = jnp.where(sc > vstar[0:1], 0.0, jnp.where(eq, tie_bias, NEG))
            run = rank[tk - 1:tk, :]
        return run

    lax.fori_loop(0, n_pair, select_pair, jnp.zeros((1, tq), F32))

    heads_per_half = MXU_DIM // ATT_HD
    n_half = ATT_W // MXU_DIM
    row_head = lax.broadcasted_iota(I32, (MXU_DIM, tq), 0) // ATT_HD
    for h in range(ATT_HEADS):
        half, hh = divmod(h, heads_per_half)
        qh = aqt_ref[0, half * MXU_DIM:(half + 1) * MXU_DIM, :]
        qbd_scr[half, :, hh * tq:(hh + 1) * tq] = jnp.where(row_head == hh, qh, jnp.zeros_like(qh))
    m_scr[...] = jnp.full_like(m_scr, NEG)
    acc_scr[...] = jnp.zeros_like(acc_scr)

    def att_pair(c2, carry):
        s_all = {}
        for sub in range(2):
            for half in range(n_half):
                rows = pl.ds(pl.multiple_of((2 * c2 + sub) * tk, tk), tk)
                s_all[sub, half] = jnp.dot(ak_ref[0, rows, half * MXU_DIM:(half + 1) * MXU_DIM], qbd_scr[half],
                                           preferred_element_type=F32)
        for sub in range(2):
            for half in range(n_half):
                for hh in range(heads_per_half):
                    h = half * heads_per_half + hh
                    vt = avt_ref[0, 2 * c2 + sub, h * ATT_VROWS:(h + 1) * ATT_VROWS, :]
                    for qs in range(tq // ATT_SQ):
                        ql = slice(qs * ATT_SQ, (qs + 1) * ATT_SQ)
                        s = s_all[sub, half][:, hh * tq + qs * ATT_SQ:hh * tq + (qs + 1) * ATT_SQ] + \
                            bias_scr[2 * c2 + sub, :, ql]
                        m_old = m_scr[h, :, ql]
                        m_new = jnp.maximum(m_old, jnp.max(s, axis=0, keepdims=True))
                        alpha = jnp.exp2(m_old - m_new)
                        p = jnp.exp2(s - m_new).astype(BF16)
                        acc_scr[h, :, ql] = alpha * acc_scr[h, :, ql] + jnp.dot(vt, p, preferred_element_type=F32)
                        m_scr[h, :, ql] = m_new
        return carry

    lax.fori_loop(0, n_pair, att_pair, 0)

    att_t = jnp.concatenate([acc_scr[h, :ATT_HD] * (1.0 / acc_scr[h, ATT_HD:ATT_HD + 1])
                             for h in range(ATT_HEADS)], axis=0)
    out_ref[0] = (att_t.T * ag_ref[0]).astype(BF16)


def _dsa_attention(aqt, ak, avt, iqt, ik, iwt, ag):
    b, s, _ = ak.shape
    tq, tk = DSA_TQ, DSA_TK
    nck = s // tk
    ksel = min(TOPK_MAX, s // 4)
    qt = lambda rows: pl.BlockSpec((1, rows, tq), lambda bi, i: (bi, 0, i))
    seq = lambda width: pl.BlockSpec((1, s, width), lambda bi, i: (bi, 0, 0))
    qn = pl.BlockSpec((1, tq, ATT_W), lambda bi, i: (bi, i, 0))
    return pl.pallas_call(
        functools.partial(_dsa_kernel, ksel=ksel),
        grid=(b, s // tq),
        in_specs=[qt(ATT_W), seq(ATT_W),
                  pl.BlockSpec((1, nck, ATT_HEADS * ATT_VROWS, tk), lambda bi, i: (bi, 0, 0, 0)),
                  qt(IDX_QW), seq(LANES), qt(SUBLANES), qn],
        out_specs=qn,
        out_shape=jax.ShapeDtypeStruct((b, s, ATT_W), BF16),
        scratch_shapes=[pltpu.VMEM((nck, tk, tq), F32), pltpu.VMEM((nck, tk, tq), BF16),
                        pltpu.VMEM((nck, tk, tq), F32),
                        pltpu.VMEM((ATT_W // MXU_DIM, MXU_DIM, (MXU_DIM // ATT_HD) * tq), BF16),
                        pltpu.VMEM((ATT_HEADS, 1, tq), F32),
                        pltpu.VMEM((ATT_HEADS, ATT_VROWS, tq), F32)],
        compiler_params=pltpu.CompilerParams(dimension_semantics=("arbitrary", "arbitrary"),
                                             vmem_limit_bytes=VMEM_LIMIT),
        name="dsa_attention",
    )(aqt, ak, avt, iqt, ik, iwt, ag)


def _gla_kernel(gqk_ref, gv_ref, glog_ref, gg_ref, nw_ref, out_ref, state_scr):
    c_len, sub = GLA_CHUNK, GLA_SUB
    nb = gqk_ref.shape[0]
    n_z = c_len * sub

    @pl.when(pl.program_id(0) == 0)
    def _():
        state_scr[...] = jnp.zeros_like(state_scr)

    t_i = lax.broadcasted_iota(I32, (c_len, GLA_QKW), 0)
    s_i = lax.broadcasted_iota(I32, (sub, GLA_QKW), 0)
    j_i = lax.broadcasted_iota(I32, (c_len, GLA_QKW), 1) % c_len
    tril = jnp.where(lax.broadcasted_iota(I32, (c_len, c_len), 0)
                     >= lax.broadcasted_iota(I32, (c_len, c_len), 1), 1.0, 0.0).astype(BF16)
    lane_head = lax.broadcasted_iota(I32, (c_len, GLA_QKW), 1) // GLA_DK

    def stack_heads(a):
        return jnp.concatenate([jnp.where(lane_head == h, a, 0.0) for h in range(GLA_HEADS)],
                               axis=0).astype(BF16)

    zr = lax.broadcasted_iota(I32, (n_z, GLA_QKW), 0)
    zc = lax.broadcasted_iota(I32, (n_z, GLA_QKW), 1)
    pick = (zc % c_len) == (zr // (sub * sub)) * sub + zr % sub
    hr = lax.broadcasted_iota(I32, (GLA_QKW, GLA_QKW), 0) // GLA_DK
    hc = lax.broadcasted_iota(I32, (GLA_QKW, GLA_QKW), 1) // c_len
    head_rep = jnp.where(hr == hc, 1.0, 0.0).astype(BF16)
    sr = lax.broadcasted_iota(I32, (c_len, n_z), 0)
    sc_ = lax.broadcasted_iota(I32, (c_len, n_z), 1)
    row_sel = jnp.where(sr == sc_ // sub, 1.0, 0.0).astype(BF16)
    vr = lax.broadcasted_iota(I32, (GLA_HEADS * c_len, GLA_W), 0) // c_len
    vc = lax.broadcasted_iota(I32, (GLA_HEADS * c_len, GLA_W), 1) // GLA_DV
    v_diag = vr == vc
    spans = [c_len >> (lv + 1) for lv in range((c_len // sub).bit_length() - 1)]

    units = range(nb * (gqk_ref.shape[1] // c_len))
    bat = lambda u: u % nb
    tsl = lambda u: slice((u // nb) * c_len, (u // nb + 1) * c_len)
    cur_state = {bi: state_scr[bi] for bi in range(nb)}
    q, k, vb, b, b_last, state_t, o, a, rep = ({} for _ in range(9))

    for u in units:
        gqk = gqk_ref[bat(u), tsl(u), :]
        q[u], k[u] = gqk[:, :GLA_QKW], gqk[:, GLA_QKW:]
        vb[u] = gv_ref[bat(u), tsl(u), :].astype(BF16)
        g = glog_ref[bat(u), tsl(u), :]
        g1 = g.astype(BF16)
        r1 = g - g1.astype(F32)
        g2 = r1.astype(BF16)
        g3 = (r1 - g2.astype(F32)).astype(BF16)
        b[u] = (jnp.dot(tril, g1, preferred_element_type=F32) + jnp.dot(tril, g2, preferred_element_type=F32)
                 + jnp.dot(tril, g3, preferred_element_type=F32))
        b_last[u] = b[u][c_len - 1:c_len, :]

    for u in units:
        state_t[u] = cur_state[bat(u)]
        o_stack = _nt_dot(stack_heads(q[u] * jnp.exp(b[u])), state_t[u].astype(BF16))
        o[u] = jnp.concatenate([o_stack[h * c_len:(h + 1) * c_len] for h in range(GLA_HEADS)], axis=1)

        k_bd = stack_heads(k[u] * jnp.exp(b_last[u] - b[u]))
        v_st = jnp.concatenate([vb[u][:, h * GLA_DV:(h + 1) * GLA_DV] for h in range(GLA_HEADS)], axis=0)
        cur_state[bat(u)] = jnp.exp(b_last[u]) * state_t[u] + _tn_dot(v_st, k_bd)

        for span in spans:
            blk = 2 * span
            ref = jnp.concatenate([jnp.broadcast_to(b[u][m * blk + span - 1:m * blk + span, :], (blk, GLA_QKW))
                                   for m in range(c_len // blk)], axis=0)
            upper = (t_i % blk) >= span
            q_l = q[u] * jnp.exp(jnp.where(upper, b[u] - ref, NEG))
            k_l = k[u] * jnp.exp(jnp.where(upper, NEG, ref - b[u]))
            a_l = _nt_dot(q_l.astype(BF16), stack_heads(k_l))
            if blk < c_len:
                a_l = jnp.where(t_i // blk == j_i // blk, a_l, 0.0)
            a[u] = a_l if u not in a else a[u] + a_l

    for bi in range(nb):
        state_scr[bi] = cur_state[bi]

    for u in units:
        zs = []
        for t in range(c_len):
            blk0 = (t // sub) * sub
            rs = slice(blk0, blk0 + sub)
            dec = jnp.exp(jnp.where(s_i <= t - blk0, b[u][t:t + 1, :] - b[u][rs], NEG))
            zs.append(q[u][t:t + 1, :] * k[u][rs] * dec)
        zmat = jnp.concatenate(zs, axis=0).astype(BF16)
        rep[u] = jnp.dot(zmat, head_rep, preferred_element_type=F32)

    for u in units:
        a_all = a[u] + jnp.dot(row_sel, jnp.where(pick, rep[u], 0.0).astype(BF16), preferred_element_type=F32)
        v_bd = jnp.where(v_diag, jnp.concatenate([vb[u]] * GLA_HEADS, axis=0), jnp.zeros((), BF16))
        o_u = o[u] + jnp.dot(a_all.astype(BF16), v_bd, preferred_element_type=F32)
        outs = []
        for h in range(GLA_HEADS):
            oh = o_u[:, h * GLA_DV:(h + 1) * GLA_DV]
            oh = oh * lax.rsqrt(jnp.mean(oh * oh, axis=-1, keepdims=True) + NORM_EPS)
            outs.append(oh * nw_ref[...])
        out_ref[bat(u), tsl(u), :] = (jnp.concatenate(outs, axis=1) * gg_ref[bat(u), tsl(u), :]).astype(BF16)


def _gla(gqk, gv, glog, gg, norm_w):
    b, s, _ = gqk.shape
    c_len = GLA_CHUNK
    step = min(GLA_STEP, s)
    blk = lambda width: pl.BlockSpec((b, step, width), lambda c: (0, c, 0))
    nw2 = norm_w.reshape(1, GLA_DV)
    return pl.pallas_call(
        _gla_kernel,
        grid=(s // step,),
        in_specs=[blk(2 * GLA_QKW), blk(GLA_W), blk(GLA_QKW), blk(GLA_W),
                  pl.BlockSpec(nw2.shape, lambda c: (0, 0))],
        out_specs=blk(GLA_W),
        out_shape=jax.ShapeDtypeStruct((b, s, GLA_W), BF16),
        scratch_shapes=[pltpu.VMEM((b, GLA_DV, GLA_QKW), F32)],
        compiler_params=pltpu.CompilerParams(dimension_semantics=("arbitrary",),
                                             vmem_limit_bytes=VMEM_LIMIT),
        name="gla_chunked",
    )(gqk, gv, glog, gg, nw2)


def _out_kernel(att_ref, gla_ref, wa_ref, wg_ref, x_ref, nw_ref, out_ref):
    mixed = jnp.dot(att_ref[...], wa_ref[...], preferred_element_type=F32) + \
        jnp.dot(gla_ref[...], wg_ref[...], preferred_element_type=F32)
    y = mixed * lax.rsqrt(jnp.mean(mixed * mixed, axis=-1, keepdims=True) + NORM_EPS)
    out_ref[...] = x_ref[...] + y * nw_ref[...]


def _out_project(att2, gla2, w_out, x2, post_norm_w):
    n, d = x2.shape
    tm = min(OUT_TM, n)
    wa, wg = w_out[:ATT_W].astype(BF16), w_out[ATT_W:].astype(BF16)
    row = lambda width: pl.BlockSpec((tm, width), lambda i: (i, 0))
    full = lambda a: pl.BlockSpec(a.shape, lambda i: (0, 0))
    return pl.pallas_call(
        _out_kernel,
        grid=(n // tm,),
        in_specs=[row(ATT_W), row(GLA_W), full(wa), full(wg), row(d), full(post_norm_w)],
        out_specs=row(d),
        out_shape=jax.ShapeDtypeStruct((n, d), F32),
        compiler_params=pltpu.CompilerParams(dimension_semantics=("arbitrary",),
                                             vmem_limit_bytes=VMEM_LIMIT),
        name="out_proj_postnorm",
    )(att2, gla2, wa, wg, x2, post_norm_w)


def kernel(x, positions, w_in, w_gla_gate_up, b_gla_gate, gla_norm_w, w_out, pre_norm_w, post_norm_w):
    b, s, d = x.shape
    n = b * s
    assert w_in.shape[0] == d and w_out.shape == (ATT_W + GLA_W, d)
    assert s % PROJ_TM == 0 and s % DSA_TQ == 0 and (s // DSA_TK) % 2 == 0 and DSA_TQ % ATT_SQ == 0
    assert s % min(GLA_STEP, s) == 0 and min(GLA_STEP, s) % GLA_CHUNK == 0 and n % min(OUT_TM, n) == 0
    cos8, sin8 = _rope_tables(positions)
    aqt, ak, avt, iqt, ik, iwt, ag, gqk, gv, glog, gg = _project(x, pre_norm_w.reshape(1, d), cos8, sin8, w_in,
                                                                  w_gla_gate_up, b_gla_gate)
    att = _dsa_attention(aqt, ak, avt, iqt, ik, iwt, ag)
    gla = _gla(gqk, gv, glog, gg, gla_norm_w)
    out = _out_project(att.reshape(n, ATT_W), gla.reshape(n, GLA_W), w_out, x.reshape(n, d),
                       post_norm_w.reshape(1, d))
    return out.reshape(b, s, d)
```

```python
import functools

import jax
import jax.numpy as jnp
import numpy as np
from jax import lax
from jax.experimental import pallas as pl
from jax.experimental.pallas import tpu as pltpu

F32 = jnp.float32
BF16 = jnp.bfloat16
I32 = jnp.int32

ATT_HEADS = 8
ATT_HD = 64
ATT_W = ATT_HEADS * ATT_HD
IDX_HEADS = 4
IDX_HD = 64
IDX_QW = IDX_HEADS * IDX_HD
TOPK_MAX = 256
GLA_HEADS = 4
GLA_DK = 64
GLA_DV = 128
GLA_QKW = GLA_HEADS * GLA_DK
GLA_W = GLA_HEADS * GLA_DV
GLA_GATE_RANK = 16
GLA_TAU = 16.0
GLA_CHUNK = 64
GLA_SUB = 8
GLA_STEP = 256
ROPE_THETA = 500000.0
ROT_DIM = ATT_HD // 4
ROT_HALF = ROT_DIM // 2
NORM_EPS = 1e-6
NEG = float(np.float32(-1e30))
LOG2E = 1.4426950408889634

SUBLANES = 8
LANES = 128
MXU_DIM = 256
VMEM_LIMIT = 56 * 1024 * 1024

PROJ_TM = 1024
OUT_TM = 1024
DSA_TQ = 512
DSA_TK = 256
ATT_SQ = 512
COUNT_CHAINS = 4
ATT_VROWS = ATT_HD + 16
INT_MIN = -(2 ** 31)


def _nt_dot(a, b):
    return lax.dot_general(a, b, (((1,), (1,)), ((), ())), preferred_element_type=F32)


def _tn_dot(a, b):
    return lax.dot_general(a, b, (((0,), (0,)), ((), ())), preferred_element_type=F32)


def _rope_table_kernel(pos_ref, inv_ref, cos_ref, sin_ref):
    ang = pos_ref[...].astype(F32) * inv_ref[...]
    cos_ref[...] = jnp.cos(ang)
    sin_ref[...] = jnp.sin(ang)


def _rope_tables(positions):
    n = positions.size
    inv = ROPE_THETA ** (-jnp.arange(0, ROT_DIM, 2, dtype=F32) / ROT_DIM)
    return pl.pallas_call(
        _rope_table_kernel,
        out_shape=(jax.ShapeDtypeStruct((ROT_HALF, n), F32),) * 2,
        name="rope_tables",
    )(positions.reshape(1, n), inv.reshape(ROT_HALF, 1))


def _proj_kernel(x_ref, pnw_ref, cos_ref, sin_ref,
                 wt_all, w_ag, w_gqk, w_gv, w_gg,
                 wupt_hi, wupt_lo, bup_ref,
                 aqt_o, ak_o, avt_o, iqt_o, ik_o, iwt_o, ag_o, gqk_o, gv_o, glog_o, gg_o):
    x = x_ref[0]
    ms = jnp.mean(x * x, axis=-1, keepdims=True)
    h = (x * lax.rsqrt(ms + NORM_EPS)) * pnw_ref[...]
    hb = h.astype(BF16)
    cos8, sin8 = cos_ref[...], sin_ref[...]

    def mm(w_ref):
        return jnp.dot(hb, w_ref[...], preferred_element_type=F32)

    def rope_t(yt):
        pieces = []
        for hd in range(yt.shape[0] // ATT_HD):
            rows = yt[hd * ATT_HD:(hd + 1) * ATT_HD]
            x1, x2 = rows[:ROT_HALF], rows[ROT_HALF:ROT_DIM]
            pieces += [x1 * cos8 - x2 * sin8, x2 * cos8 + x1 * sin8, rows[ROT_DIM:]]
        return jnp.concatenate(pieces, axis=0)

    yt = _nt_dot(wt_all[...], hb)
    ikt, o = yt[:LANES], LANES
    aq_t, o = yt[o:o + ATT_W], o + ATT_W
    ak_t, o = yt[o:o + ATT_W], o + ATT_W
    iq_t, o = yt[o:o + IDX_QW], o + IDX_QW
    av_t = yt[o:o + ATT_W]
    iw_row = IDX_HD + GLA_GATE_RANK
    iwt_o[0] = ikt[iw_row:iw_row + SUBLANES] * (IDX_HEADS ** -0.5 * IDX_HD ** -0.5)
    aqt_o[0] = (rope_t(aq_t) * (ATT_HD ** -0.5 * LOG2E)).astype(BF16)
    ik_o[0] = jnp.concatenate([rope_t(ikt[:IDX_HD]), jnp.zeros_like(ikt[IDX_HD:])], axis=0).T.astype(BF16)
    ak_o[0] = rope_t(ak_t).T.astype(BF16)
    ikt_hi = ikt.astype(BF16)
    ikt_lo = (ikt - ikt_hi.astype(F32)).astype(BF16)
    zt = (jnp.dot(wupt_hi[...], ikt_hi, preferred_element_type=F32)
          + jnp.dot(wupt_lo[...], ikt_hi, preferred_element_type=F32)
          + jnp.dot(wupt_hi[...], ikt_lo, preferred_element_type=F32))
    a_gate = mm(w_ag)
    z = zt.T + bup_ref[...]
    glog_o[0] = (jnp.minimum(z, 0.0) - jnp.log1p(jnp.exp(-jnp.abs(z)))) * (1.0 / GLA_TAU)
    g_gate = mm(w_gg)
    ag_o[0] = a_gate * jax.nn.sigmoid(a_gate)
    iqt_o[0] = rope_t(iq_t).astype(BF16)
    gg_o[0] = g_gate * jax.nn.sigmoid(g_gate)
    gqk = mm(w_gqk)
    lane = lax.broadcasted_iota(I32, gqk.shape, 1)
    gqk_o[0] = jnp.where(lane < GLA_QKW, gqk * (GLA_DK ** -0.5), gqk)
    avt = av_t
    ones_rows = jnp.ones((ATT_VROWS - ATT_HD, avt.shape[1]), F32)
    avt = jnp.concatenate([piece for hd in range(ATT_HEADS)
                           for piece in (avt[hd * ATT_HD:(hd + 1) * ATT_HD], ones_rows)], axis=0).astype(BF16)
    for c in range(avt_o.shape[1]):
        avt_o[0, c] = avt[:, c * DSA_TK:(c + 1) * DSA_TK]
    gv_o[0] = mm(w_gv)


def _split_w_in(w_in):
    sizes = (ATT_W, ATT_W, ATT_W, IDX_QW, IDX_HD, IDX_HEADS, ATT_W,
             GLA_QKW, GLA_QKW, GLA_W, GLA_GATE_RANK, GLA_W)
    offs = np.cumsum((0,) + sizes)
    seg = [w_in[:, offs[i]:offs[i + 1]] for i in range(len(sizes))]
    a_q, a_k, a_v, i_q, i_k, i_w, a_g, g_q, g_k, g_v, g_lr, g_g = seg
    d = w_in.shape[0]
    pad = lambda w, n: jnp.concatenate([w, jnp.zeros((d, n - w.shape[1]), w.dtype)], axis=1)
    small = pad(jnp.concatenate([i_k, g_lr, i_w], axis=1), LANES)
    transposed = jnp.concatenate([small, a_q, a_k, i_q, a_v], axis=1).astype(BF16).T
    natural = [a_g, jnp.concatenate([g_q, g_k], axis=1), g_v, g_g]
    return [transposed], [w.astype(BF16) for w in natural]


def _project(x, pre_norm_w, cos8, sin8, w_in, w_up, b_up):
    b, s, d = x.shape
    wupt = jnp.concatenate([jnp.zeros((GLA_QKW, IDX_HD), F32), w_up.T,
                            jnp.zeros((GLA_QKW, LANES - IDX_HD - GLA_GATE_RANK), F32)], axis=1)
    wupt_hi = wupt.astype(BF16)
    wupt_lo = (wupt - wupt_hi.astype(F32)).astype(BF16)
    gate_ops = [wupt_hi, wupt_lo, b_up.reshape(1, GLA_QKW)]
    tm = min(PROJ_TM, s)
    nt = s // tm
    cpt = tm // DSA_TK
    wts, wns = _split_w_in(w_in)
    full = lambda a: pl.BlockSpec(a.shape, lambda bi, i: (0,) * a.ndim)
    nat = lambda width: pl.BlockSpec((1, tm, width), lambda bi, i: (bi, i, 0))
    tr = lambda rows: pl.BlockSpec((1, rows, tm), lambda bi, i: (bi, 0, i))
    tab = pl.BlockSpec((ROT_HALF, tm), lambda bi, i: (0, bi * nt + i))
    vrows = ATT_HEADS * ATT_VROWS
    out_specs = [tr(ATT_W), nat(ATT_W), pl.BlockSpec((1, cpt, vrows, DSA_TK), lambda bi, i: (bi, i, 0, 0)),
                 tr(IDX_QW), nat(LANES), tr(SUBLANES),
                 nat(ATT_W), nat(2 * GLA_QKW), nat(GLA_W), nat(GLA_QKW), nat(GLA_W)]
    sds = jax.ShapeDtypeStruct
    out_shape = [sds((b, ATT_W, s), BF16), sds((b, s, ATT_W), BF16), sds((b, s // DSA_TK, vrows, DSA_TK), BF16),
                 sds((b, IDX_QW, s), BF16), sds((b, s, LANES), BF16), sds((b, SUBLANES, s), F32),
                 sds((b, s, ATT_W), F32), sds((b, s, 2 * GLA_QKW), F32), sds((b, s, GLA_W), F32),
                 sds((b, s, GLA_QKW), F32), sds((b, s, GLA_W), F32)]
    return pl.pallas_call(
        _proj_kernel,
        grid=(b, nt),
        in_specs=[nat(d), full(pre_norm_w), tab, tab] + [full(w) for w in wts + wns + gate_ops],
        out_specs=out_specs,
        out_shape=out_shape,
        compiler_params=pltpu.CompilerParams(dimension_semantics=("arbitrary", "arbitrary"),
                                             vmem_limit_bytes=VMEM_LIMIT),
        name="prenorm_in_proj",
    )(x, pre_norm_w, cos8, sin8, *wts, *wns, *gate_ops)


def _key_to_f32(key):
    return pltpu.bitcast(key ^ ((key >> 31) & 0x7FFFFFFF), F32)


def _dsa_kernel(aqt_ref, ak_ref, avt_ref, iqt_ref, ik_ref, iwt_ref, ag_ref, out_ref,
                score_scr, coarse_scr, bias_scr, qbd_scr, m_scr, acc_scr, *, ksel):
    tq, tk = DSA_TQ, DSA_TK
    i = pl.program_id(1)
    nck = (i * tq + tq + tk - 1) // tk
    groups = tk // SUBLANES

    key_iota = lax.broadcasted_iota(I32, (tk, tq), 0)
    q_pos = i * tq + lax.broadcasted_iota(I32, (tk, tq), 1)

    iqt = iqt_ref[0]
    q_cat = jnp.concatenate([iqt[h * IDX_HD:(h + 1) * IDX_HD] for h in range(IDX_HEADS)], axis=1)
    q_cat = jnp.concatenate([q_cat, jnp.zeros((LANES - IDX_HD, IDX_HEADS * tq), BF16)], axis=0)
    iwt = iwt_ref[0]
    w_rows = [iwt[h:h + 1, :] for h in range(IDX_HEADS)]

    n_pair = (nck + 1) // 2

    def score_pair(c2, carry):
        chunks = (2 * c2, 2 * c2 + 1)
        rels = [jnp.dot(ik_ref[0, pl.ds(pl.multiple_of(c * tk, tk), tk), :], q_cat,
                        preferred_element_type=F32) for c in chunks]
        for c, rel in zip(chunks, rels):
            sc = w_rows[0] * jnp.maximum(rel[:, 0:tq], 0.0)
            for h in range(1, IDX_HEADS):
                sc = sc + w_rows[h] * jnp.maximum(rel[:, h * tq:(h + 1) * tq], 0.0)
            sc = jnp.where(c * tk + key_iota <= q_pos, sc, NEG)
            score_scr[c] = sc
            coarse_scr[c] = sc.astype(BF16)
        return carry

    lax.fori_loop(0, n_pair, score_pair, 0)

    assert tq == 2 * tk and ksel % (COUNT_CHAINS * 2 * SUBLANES) == 0
    diag_rows = [max((l + 1) * LANES, ksel) for l in range(tq // LANES)]

    def diag_tiles(scr, rows_per_vreg):
        for l, rows in enumerate(diag_rows):
            sl = slice(l * LANES, (l + 1) * LANES)
            parts = [scr[2 * (n_pair - 1) + c, :min(rows - c * tk, tk), sl] for c in range(2) if rows > c * tk]
            keys = jnp.concatenate(parts, axis=0) if len(parts) > 1 else parts[0]
            yield sl, keys.reshape(COUNT_CHAINS, rows // (COUNT_CHAINS * rows_per_vreg), rows_per_vreg, LANES)

    def count(cmp, thr):
        def body(c2, acc):
            sc = score_scr[pl.ds(2 * c2, 2)].reshape(COUNT_CHAINS, 2 * groups // COUNT_CHAINS, SUBLANES, tq)
            return acc + jnp.sum(jnp.where(cmp(sc, thr[None]), 1, 0), axis=1)
        acc = lax.fori_loop(0, n_pair - 1, body, jnp.zeros((COUNT_CHAINS, SUBLANES, tq), I32))
        diag = [jnp.sum(jnp.where(cmp(sc, thr[None, :, sl]), 1, 0), axis=1) for sl, sc in diag_tiles(score_scr, SUBLANES)]
        acc = acc + jnp.concatenate(diag, axis=2)
        return jnp.sum(acc.reshape(COUNT_CHAINS * SUBLANES, tq), axis=0, keepdims=True)

    def count_coarse(cand_b):
        packed_rows = 2 * SUBLANES
        one, zero = jnp.ones((), BF16), jnp.zeros((), BF16)
        def body(c2, acc):
            hi = coarse_scr[pl.ds(2 * c2, 2)].reshape(COUNT_CHAINS, 2 * tk // (COUNT_CHAINS * packed_rows),
                                                      packed_rows, tq)
            for g in range(hi.shape[1]):
                acc = acc + jnp.where(hi[:, g] >= cand_b[None], one, zero)
            return acc
        acc = lax.fori_loop(0, n_pair - 1, body, jnp.zeros((COUNT_CHAINS, packed_rows, tq), BF16))
        diag = []
        for sl, hi in diag_tiles(coarse_scr, packed_rows):
            part = jnp.zeros((COUNT_CHAINS, packed_rows, LANES), BF16)
            for g in range(hi.shape[1]):
                part = part + jnp.where(hi[:, g] >= cand_b[None, :, sl], one, zero)
            diag.append(part)
        acc = acc + jnp.concatenate(diag, axis=2)
        return jnp.sum(acc.astype(F32).reshape(COUNT_CHAINS * packed_rows, tq), axis=0, keepdims=True)

    def bisect_coarse(it, prefix):
        cand = prefix + lax.shift_left(jnp.int32(1), 15 - it)
        pattern = cand ^ ((cand >> 31) & 0x7FFF)
        cand_b = pltpu.bitcast(lax.shift_left(pattern, 16), F32).astype(BF16)
        cnt = count_coarse(cand_b)
        return jnp.where(cnt >= ksel, cand, prefix)

    assert (tk // (COUNT_CHAINS * SUBLANES)) * (coarse_scr.shape[0] // 2) <= 2 ** 8, "bf16 counts must stay exact"
    key16 = lax.fori_loop(0, 16, bisect_coarse, jnp.full((2 * SUBLANES, tq), -(2 ** 15), I32))[:SUBLANES]
    key_p = lax.shift_left(key16, 16) | ((key16 >> 31) & 0xFFFF)
    key_lo = key_p - 2 ** 16

    def bisect(it, prefix):
        cand = prefix + lax.shift_left(jnp.int32(1), 16 - it)
        cand_f = _key_to_f32(cand)
        cnt = count(lambda sc, t: sc >= t, cand_f)
        return jnp.where(cnt >= ksel, cand, prefix)

    vstar = _key_to_f32(lax.fori_loop(0, 17, bisect, key_lo))
    need_f = (ksel - count(lambda sc, t: sc > t, vstar)).astype(F32)

    tri = jnp.where(lax.broadcasted_iota(I32, (tk, tk), 1) <= lax.broadcasted_iota(I32, (tk, tk), 0),
                    1.0, 0.0).astype(BF16)

    def select_pair(c2, run):
        chunks = (2 * c2, 2 * c2 + 1)
        scs = [score_scr[c] for c in chunks]
        eqs = [sc == vstar[0:1] for sc in scs]
        ranks = [jnp.dot(tri, jnp.where(eq, 1.0, 0.0).astype(BF16), preferred_element_type=F32) for eq in eqs]
        for c, sc, eq, rank in zip(chunks, scs, eqs, ranks):
            rank = rank + run
            tie_bias = jnp.where(rank <= need_f, jnp.where(c * tk + key_iota <= q_pos, 0.0, NEG), NEG)
            bias_scr[c] = jnp.where(sc > vstar[0:1], 0.0, jnp.where(eq, tie_bias, NEG))
            run = rank[tk - 1:tk, :]
        return run

    lax.fori_loop(0, n_pair, select_pair, jnp.zeros((1, tq), F32))

    heads_per_half = MXU_DIM // ATT_HD
    n_half = ATT_W // MXU_DIM
    row_head = lax.broadcasted_iota(I32, (MXU_DIM, tq), 0) // ATT_HD
    for h in range(ATT_HEADS):
        half, hh = divmod(h, heads_per_half)
        qh = aqt_ref[0, half * MXU_DIM:(half + 1) * MXU_DIM, :]
        qbd_scr[half, :, hh * tq:(hh + 1) * tq] = jnp.where(row_head == hh, qh, jnp.zeros_like(qh))
    m_scr[...] = jnp.full_like(m_scr, NEG)
    acc_scr[...] = jnp.zeros_like(acc_scr)

    def att_pair(c2, carry):
        s_all = {}
        for sub in range(2):
            for half in range(n_half):
                rows = pl.ds(pl.multiple_of((2 * c2 + sub) * tk, tk), tk)
                s_all[sub, half] = jnp.dot(ak_ref[0, rows, half * MXU_DIM:(half + 1) * MXU_DIM], qbd_scr[half],
                                           preferred_element_type=F32)
        for sub in range(2):
            for half in range(n_half):
                for hh in range(heads_per_half):
                    h = half * heads_per_half + hh
                    vt = avt_ref[0, 2 * c2 + sub, h * ATT_VROWS:(h + 1) * ATT_VROWS, :]
                    for qs in range(tq // ATT_SQ):
                        ql = slice(qs * ATT_SQ, (qs + 1) * ATT_SQ)
                        s = s_all[sub, half][:, hh * tq + qs * ATT_SQ:hh * tq + (qs + 1) * ATT_SQ] + \
                            bias_scr[2 * c2 + sub, :, ql]
                        m_old = m_scr[h, :, ql]
                        m_new = jnp.maximum(m_old, jnp.max(s, axis=0, keepdims=True))
                        alpha = jnp.exp2(m_old - m_new)
                        p = jnp.exp2(s - m_new).astype(BF16)
                        acc_scr[h, :, ql] = alpha * acc_scr[h, :, ql] + jnp.dot(vt, p, preferred_element_type=F32)
                        m_scr[h, :, ql] = m_new
        return carry

    lax.fori_loop(0, n_pair, att_pair, 0)

    att_t = jnp.concatenate([acc_scr[h, :ATT_HD] * (1.0 / acc_scr[h, ATT_HD:ATT_HD + 1])
                             for h in range(ATT_HEADS)], axis=0)
    out_ref[0] = (att_t.T * ag_ref[0]).astype(BF16)


def _dsa_attention(aqt, ak, avt, iqt, ik, iwt, ag):
    b, s, _ = ak.shape
    tq, tk = DSA_TQ, DSA_TK
    nck = s // tk
    ksel = min(TOPK_MAX, s // 4)
    qt = lambda rows: pl.BlockSpec((1, rows, tq), lambda bi, i: (bi, 0, i))
    seq = lambda width: pl.BlockSpec((1, s, width), lambda bi, i: (bi, 0, 0))
    qn = pl.BlockSpec((1, tq, ATT_W), lambda bi, i: (bi, i, 0))
    return pl.pallas_call(
        functools.partial(_dsa_kernel, ksel=ksel),
        grid=(b, s // tq),
        in_specs=[qt(ATT_W), seq(ATT_W),
                  pl.BlockSpec((1, nck, ATT_HEADS * ATT_VROWS, tk), lambda bi, i: (bi, 0, 0, 0)),
                  qt(IDX_QW), seq(LANES), qt(SUBLANES), qn],
        out_specs=qn,
        out_shape=jax.ShapeDtypeStruct((b, s, ATT_W), BF16),
        scratch_shapes=[pltpu.VMEM((nck, tk, tq), F32), pltpu.VMEM((nck, tk, tq), BF16),
                        pltpu.VMEM((nck, tk, tq), F32),
                        pltpu.VMEM((ATT_W // MXU_DIM, MXU_DIM, (MXU_DIM // ATT_HD) * tq), BF16),
                        pltpu.VMEM((ATT_HEADS, 1, tq), F32),
                        pltpu.VMEM((ATT_HEADS, ATT_VROWS, tq), F32)],
        compiler_params=pltpu.CompilerParams(dimension_semantics=("arbitrary", "arbitrary"),
                                             vmem_limit_bytes=VMEM_LIMIT),
        name="dsa_attention",
    )(aqt, ak, avt, iqt, ik, iwt, ag)


def _gla_kernel(gqk_ref, gv_ref, glog_ref, gg_ref, nw_ref, out_ref, state_scr):
    c_len, sub = GLA_CHUNK, GLA_SUB
    nb = gqk_ref.shape[0]
    n_z = c_len * sub

    @pl.when(pl.program_id(0) == 0)
    def _():
        state_scr[...] = jnp.zeros_like(state_scr)

    t_i = lax.broadcasted_iota(I32, (c_len, GLA_QKW), 0)
    s_i = lax.broadcasted_iota(I32, (sub, GLA_QKW), 0)
    j_i = lax.broadcasted_iota(I32, (c_len, GLA_QKW), 1) % c_len
    tril = jnp.where(lax.broadcasted_iota(I32, (c_len, c_len), 0)
                     >= lax.broadcasted_iota(I32, (c_len, c_len), 1), 1.0, 0.0).astype(BF16)
    lane_head = lax.broadcasted_iota(I32, (c_len, GLA_QKW), 1) // GLA_DK

    def stack_heads(a):
        return jnp.concatenate([jnp.where(lane_head == h, a, 0.0) for h in range(GLA_HEADS)],
                               axis=0).astype(BF16)

    zr = lax.broadcasted_iota(I32, (n_z, GLA_QKW), 0)
    zc = lax.broadcasted_iota(I32, (n_z, GLA_QKW), 1)
    pick = (zc % c_len) == (zr // (sub * sub)) * sub + zr % sub
    hr = lax.broadcasted_iota(I32, (GLA_QKW, GLA_QKW), 0) // GLA_DK
    hc = lax.broadcasted_iota(I32, (GLA_QKW, GLA_QKW), 1) // c_len
    head_rep = jnp.where(hr == hc, 1.0, 0.0).astype(BF16)
    sr = lax.broadcasted_iota(I32, (c_len, n_z), 0)
    sc_ = lax.broadcasted_iota(I32, (c_len, n_z), 1)
    row_sel = jnp.where(sr == sc_ // sub, 1.0, 0.0).astype(BF16)
    vr = lax.broadcasted_iota(I32, (GLA_HEADS * c_len, GLA_W), 0) // c_len
    vc = lax.broadcasted_iota(I32, (GLA_HEADS * c_len, GLA_W), 1) // GLA_DV
    v_diag = vr == vc
    spans = [c_len >> (lv + 1) for lv in range((c_len // sub).bit_length() - 1)]

    units = range(nb * (gqk_ref.shape[1] // c_len))
    bat = lambda u: u % nb
    tsl = lambda u: slice((u // nb) * c_len, (u // nb + 1) * c_len)
    cur_state = {bi: state_scr[bi] for bi in range(nb)}
    q, k, vb, b, b_last, state_t, o, a, rep = ({} for _ in range(9))

    for u in units:
        gqk = gqk_ref[bat(u), tsl(u), :]
        q[u], k[u] = gqk[:, :GLA_QKW], gqk[:, GLA_QKW:]
        vb[u] = gv_ref[bat(u), tsl(u), :].astype(BF16)
        g = glog_ref[bat(u), tsl(u), :]
        g1 = g.astype(BF16)
        r1 = g - g1.astype(F32)
        g2 = r1.astype(BF16)
        g3 = (r1 - g2.astype(F32)).astype(BF16)
        b[u] = (jnp.dot(tril, g1, preferred_element_type=F32) + jnp.dot(tril, g2, preferred_element_type=F32)
                 + jnp.dot(tril, g3, preferred_element_type=F32))
        b_last[u] = b[u][c_len - 1:c_len, :]

    for u in units:
        state_t[u] = cur_state[bat(u)]
        o_stack = _nt_dot(stack_heads(q[u] * jnp.exp(b[u])), state_t[u].astype(BF16))
        o[u] = jnp.concatenate([o_stack[h * c_len:(h + 1) * c_len] for h in range(GLA_HEADS)], axis=1)

        k_bd = stack_heads(k[u] * jnp.exp(b_last[u] - b[u]))
        v_st = jnp.concatenate([vb[u][:, h * GLA_DV:(h + 1) * GLA_DV] for h in range(GLA_HEADS)], axis=0)
        cur_state[bat(u)] = jnp.exp(b_last[u]) * state_t[u] + _tn_dot(v_st, k_bd)

        for span in spans:
            blk = 2 * span
            ref = jnp.concatenate([jnp.broadcast_to(b[u][m * blk + span - 1:m * blk + span, :], (blk, GLA_QKW))
                                   for m in range(c_len // blk)], axis=0)
            upper = (t_i % blk) >= span
            q_l = q[u] * jnp.exp(jnp.where(upper, b[u] - ref, NEG))
            k_l = k[u] * jnp.exp(jnp.where(upper, NEG, ref - b[u]))
            a_l = _nt_dot(q_l.astype(BF16), stack_heads(k_l))
            if blk < c_len:
                a_l = jnp.where(t_i // blk == j_i // blk, a_l, 0.0)
            a[u] = a_l if u not in a else a[u] + a_l

    for bi in range(nb):
        state_scr[bi] = cur_state[bi]

    for u in units:
        zs = []
        for t in range(c_len):
            blk0 = (t // sub) * sub
            rs = slice(blk0, blk0 + sub)
            dec = jnp.exp(jnp.where(s_i <= t - blk0, b[u][t:t + 1, :] - b[u][rs], NEG))
            zs.append(q[u][t:t + 1, :] * k[u][rs] * dec)
        zmat = jnp.concatenate(zs, axis=0).astype(BF16)
        rep[u] = jnp.dot(zmat, head_rep, preferred_element_type=F32)

    for u in units:
        a_all = a[u] + jnp.dot(row_sel, jnp.where(pick, rep[u], 0.0).astype(BF16), preferred_element_type=F32)
        v_bd = jnp.where(v_diag, jnp.concatenate([vb[u]] * GLA_HEADS, axis=0), jnp.zeros((), BF16))
        o_u = o[u] + jnp.dot(a_all.astype(BF16), v_bd, preferred_element_type=F32)
        outs = []
        for h in range(GLA_HEADS):
            oh = o_u[:, h * GLA_DV:(h + 1) * GLA_DV]
            oh = oh * lax.rsqrt(jnp.mean(oh * oh, axis=-1, keepdims=True) + NORM_EPS)
            outs.append(oh * nw_ref[...])
        out_ref[bat(u), tsl(u), :] = (jnp.concatenate(outs, axis=1) * gg_ref[bat(u), tsl(u), :]).astype(BF16)


def _gla(gqk, gv, glog, gg, norm_w):
    b, s, _ = gqk.shape
    c_len = GLA_CHUNK
    step = min(GLA_STEP, s)
    blk = lambda width: pl.BlockSpec((b, step, width), lambda c: (0, c, 0))
    nw2 = norm_w.reshape(1, GLA_DV)
    return pl.pallas_call(
        _gla_kernel,
        grid=(s // step,),
        in_specs=[blk(2 * GLA_QKW), blk(GLA_W), blk(GLA_QKW), blk(GLA_W),
                  pl.BlockSpec(nw2.shape, lambda c: (0, 0))],
        out_specs=blk(GLA_W),
        out_shape=jax.ShapeDtypeStruct((b, s, GLA_W), BF16),
        scratch_shapes=[pltpu.VMEM((b, GLA_DV, GLA_QKW), F32)],
        compiler_params=pltpu.CompilerParams(dimension_semantics=("arbitrary",),
                                             vmem_limit_bytes=VMEM_LIMIT),
        name="gla_chunked",
    )(gqk, gv, glog, gg, nw2)


def _out_kernel(att_ref, gla_ref, wa_ref, wg_ref, x_ref, nw_ref, out_ref):
    mixed = jnp.dot(att_ref[...], wa_ref[...], preferred_element_type=F32) + \
        jnp.dot(gla_ref[...], wg_ref[...], preferred_element_type=F32)
    y = mixed * lax.rsqrt(jnp.mean(mixed * mixed, axis=-1, keepdims=True) + NORM_EPS)
    out_ref[...] = x_ref[...] + y * nw_ref[...]


def _out_project(att2, gla2, w_out, x2, post_norm_w):
    n, d = x2.shape
    tm = min(OUT_TM, n)
    wa, wg = w_out[:ATT_W].astype(BF16), w_out[ATT_W:].astype(BF16)
    row = lambda width: pl.BlockSpec((tm, width), lambda i: (i, 0))
    full = lambda a: pl.BlockSpec(a.shape, lambda i: (0, 0))
    return pl.pallas_call(
        _out_kernel,
        grid=(n // tm,),
        in_specs=[row(ATT_W), row(GLA_W), full(wa), full(wg), row(d), full(post_norm_w)],
        out_specs=row(d),
        out_shape=jax.ShapeDtypeStruct((n, d), F32),
        compiler_params=pltpu.CompilerParams(dimension_semantics=("arbitrary",),
                                             vmem_limit_bytes=VMEM_LIMIT),
        name="out_proj_postnorm",
    )(att2, gla2, wa, wg, x2, post_norm_w)


def kernel(x, positions, w_in, w_gla_gate_up, b_gla_gate, gla_norm_w, w_out, pre_norm_w, post_norm_w):
    b, s, d = x.shape
    n = b * s
    assert w_in.shape[0] == d and w_out.shape == (ATT_W + GLA_W, d)
    assert s % PROJ_TM == 0 and s % DSA_TQ == 0 and (s // DSA_TK) % 2 == 0 and DSA_TQ % ATT_SQ == 0
    assert s % min(GLA_STEP, s) == 0 and min(GLA_STEP, s) % GLA_CHUNK == 0 and n % min(OUT_TM, n) == 0
    cos8, sin8 = _rope_tables(positions)
    aqt, ak, avt, iqt, ik, iwt, ag, gqk, gv, glog, gg = _project(x, pre_norm_w.reshape(1, d), cos8, sin8, w_in,
                                                                  w_gla_gate_up, b_gla_gate)
    att = _dsa_attention(aqt, ak, avt, iqt, ik, iwt, ag)
    gla = _gla(gqk, gv, glog, gg, gla_norm_w)
    out = _out_project(att.reshape(n, ATT_W), gla.reshape(n, GLA_W), w_out, x.reshape(n, d),
                       post_norm_w.reshape(1, d))
    return out.reshape(b, s, d)
```

```python
import functools

import jax
import jax.numpy as jnp
import numpy as np
from jax import lax
from jax.experimental import pallas as pl
from jax.experimental.pallas import tpu as pltpu

F32 = jnp.float32
BF16 = jnp.bfloat16
I32 = jnp.int32

ATT_HEADS = 8
ATT_HD = 64
ATT_W = ATT_HEADS * ATT_HD
IDX_HEADS = 4
IDX_HD = 64
IDX_QW = IDX_HEADS * IDX_HD
TOPK_MAX = 256
GLA_HEADS = 4
GLA_DK = 64
GLA_DV = 128
GLA_QKW = GLA_HEADS * GLA_DK
GLA_W = GLA_HEADS * GLA_DV
GLA_GATE_RANK = 16
GLA_TAU = 16.0
GLA_CHUNK = 64
GLA_SUB = 8
GLA_STEP = 256
ROPE_THETA = 500000.0
ROT_DIM = ATT_HD // 4
ROT_HALF = ROT_DIM // 2
NORM_EPS = 1e-6
NEG = float(np.float32(-1e30))
LOG2E = 1.4426950408889634

SUBLANES = 8
LANES = 128
MXU_DIM = 256
VMEM_LIMIT = 56 * 1024 * 1024

PROJ_TM = 1024
OUT_TM = 1024
DSA_TQ = 512
DSA_TK = 256
COUNT_CHAINS = 4
ATT_VROWS = ATT_HD + 16


def _nt_dot(a, b):
    return lax.dot_general(a, b, (((1,), (1,)), ((), ())), preferred_element_type=F32)


def _tn_dot(a, b):
    return lax.dot_general(a, b, (((0,), (0,)), ((), ())), preferred_element_type=F32)


def _rope_table_kernel(pos_ref, inv_ref, cos_ref, sin_ref):
    ang = pos_ref[...].astype(F32) * inv_ref[...]
    cos_ref[...] = jnp.cos(ang)
    sin_ref[...] = jnp.sin(ang)


def _rope_tables(positions):
    n = positions.size
    inv = ROPE_THETA ** (-jnp.arange(0, ROT_DIM, 2, dtype=F32) / ROT_DIM)
    return pl.pallas_call(
        _rope_table_kernel,
        out_shape=(jax.ShapeDtypeStruct((ROT_HALF, n), F32),) * 2,
        name="rope_tables",
    )(positions.reshape(1, n), inv.reshape(ROT_HALF, 1))


def _proj_kernel(x_ref, pnw_ref, cos_ref, sin_ref,
                 wt_all, w_ag, w_gqk, w_gv, w_gg,
                 wupt_hi, wupt_lo, bup_ref,
                 aqt_o, ak_o, avt_o, iqt_o, ik_o, iwt_o, ag_o, gqk_o, gv_o, glog_o, gg_o):
    x = x_ref[0]
    ms = jnp.mean(x * x, axis=-1, keepdims=True)
    h = (x * lax.rsqrt(ms + NORM_EPS)) * pnw_ref[...]
    hb = h.astype(BF16)
    cos8, sin8 = cos_ref[...], sin_ref[...]

    def mm(w_ref):
        return jnp.dot(hb, w_ref[...], preferred_element_type=F32)

    def rope_t(yt):
        pieces = []
        for hd in range(yt.shape[0] // ATT_HD):
            rows = yt[hd * ATT_HD:(hd + 1) * ATT_HD]
            x1, x2 = rows[:ROT_HALF], rows[ROT_HALF:ROT_DIM]
            pieces += [x1 * cos8 - x2 * sin8, x2 * cos8 + x1 * sin8, rows[ROT_DIM:]]
        return jnp.concatenate(pieces, axis=0)

    yt = _nt_dot(wt_all[...], hb)
    ikt, o = yt[:LANES], LANES
    aq_t, o = yt[o:o + ATT_W], o + ATT_W
    ak_t, o = yt[o:o + ATT_W], o + ATT_W
    iq_t, o = yt[o:o + IDX_QW], o + IDX_QW
    av_t = yt[o:o + ATT_W]
    iw_row = IDX_HD + GLA_GATE_RANK
    iwt_o[0] = ikt[iw_row:iw_row + SUBLANES] * (IDX_HEADS ** -0.5 * IDX_HD ** -0.5)
    aqt_o[0] = (rope_t(aq_t) * (ATT_HD ** -0.5 * LOG2E)).astype(BF16)
    ik_o[0] = jnp.concatenate([rope_t(ikt[:IDX_HD]), jnp.zeros_like(ikt[IDX_HD:])], axis=0).T.astype(BF16)
    ak_o[0] = rope_t(ak_t).T.astype(BF16)
    ikt_hi = ikt.astype(BF16)
    ikt_lo = (ikt - ikt_hi.astype(F32)).astype(BF16)
    zt = (jnp.dot(wupt_hi[...], ikt_hi, preferred_element_type=F32)
          + jnp.dot(wupt_lo[...], ikt_hi, preferred_element_type=F32)
          + jnp.dot(wupt_hi[...], ikt_lo, preferred_element_type=F32))
    a_gate = mm(w_ag)
    z = zt.T + bup_ref[...]
    glog_o[0] = (jnp.minimum(z, 0.0) - jnp.log1p(jnp.exp(-jnp.abs(z)))) * (1.0 / GLA_TAU)
    g_gate = mm(w_gg)
    ag_o[0] = a_gate * jax.nn.sigmoid(a_gate)
    iqt_o[0] = rope_t(iq_t).astype(BF16)
    gg_o[0] = g_gate * jax.nn.sigmoid(g_gate)
    gqk = mm(w_gqk)
    lane = lax.broadcasted_iota(I32, gqk.shape, 1)
    gqk_o[0] = jnp.where(lane < GLA_QKW, gqk * (GLA_DK ** -0.5), gqk)
    avt = av_t
    ones_rows = jnp.ones((ATT_VROWS - ATT_HD, avt.shape[1]), F32)
    avt = jnp.concatenate([piece for hd in range(ATT_HEADS)
                           for piece in (avt[hd * ATT_HD:(hd + 1) * ATT_HD], ones_rows)], axis=0).astype(BF16)
    for c in range(avt_o.shape[1]):
        avt_o[0, c] = avt[:, c * DSA_TK:(c + 1) * DSA_TK]
    gv_o[0] = mm(w_gv)


def _split_w_in(w_in):
    sizes = (ATT_W, ATT_W, ATT_W, IDX_QW, IDX_HD, IDX_HEADS, ATT_W,
             GLA_QKW, GLA_QKW, GLA_W, GLA_GATE_RANK, GLA_W)
    offs = np.cumsum((0,) + sizes)
    seg = [w_in[:, offs[i]:offs[i + 1]] for i in range(len(sizes))]
    a_q, a_k, a_v, i_q, i_k, i_w, a_g, g_q, g_k, g_v, g_lr, g_g = seg
    d = w_in.shape[0]
    pad = lambda w, n: jnp.concatenate([w, jnp.zeros((d, n - w.shape[1]), w.dtype)], axis=1)
    small = pad(jnp.concatenate([i_k, g_lr, i_w], axis=1), LANES)
    transposed = jnp.concatenate([small, a_q, a_k, i_q, a_v], axis=1).astype(BF16).T
    natural = [a_g, jnp.concatenate([g_q, g_k], axis=1), g_v, g_g]
    return [transposed], [w.astype(BF16) for w in natural]


def _project(x, pre_norm_w, cos8, sin8, w_in, w_up, b_up):
    b, s, d = x.shape
    wupt = jnp.concatenate([jnp.zeros((GLA_QKW, IDX_HD), F32), w_up.T,
                            jnp.zeros((GLA_QKW, LANES - IDX_HD - GLA_GATE_RANK), F32)], axis=1)
    wupt_hi = wupt.astype(BF16)
    wupt_lo = (wupt - wupt_hi.astype(F32)).astype(BF16)
    gate_ops = [wupt_hi, wupt_lo, b_up.reshape(1, GLA_QKW)]
    tm = min(PROJ_TM, s)
    nt = s // tm
    cpt = tm // DSA_TK
    wts, wns = _split_w_in(w_in)
    full = lambda a: pl.BlockSpec(a.shape, lambda bi, i: (0,) * a.ndim)
    nat = lambda width: pl.BlockSpec((1, tm, width), lambda bi, i: (bi, i, 0))
    tr = lambda rows: pl.BlockSpec((1, rows, tm), lambda bi, i: (bi, 0, i))
    tab = pl.BlockSpec((ROT_HALF, tm), lambda bi, i: (0, bi * nt + i))
    vrows = ATT_HEADS * ATT_VROWS
    out_specs = [tr(ATT_W), nat(ATT_W), pl.BlockSpec((1, cpt, vrows, DSA_TK), lambda bi, i: (bi, i, 0, 0)),
                 tr(IDX_QW), nat(LANES), tr(SUBLANES),
                 nat(ATT_W), nat(2 * GLA_QKW), nat(GLA_W), nat(GLA_QKW), nat(GLA_W)]
    sds = jax.ShapeDtypeStruct
    out_shape = [sds((b, ATT_W, s), BF16), sds((b, s, ATT_W), BF16), sds((b, s // DSA_TK, vrows, DSA_TK), BF16),
                 sds((b, IDX_QW, s), BF16), sds((b, s, LANES), BF16), sds((b, SUBLANES, s), F32),
                 sds((b, s, ATT_W), F32), sds((b, s, 2 * GLA_QKW), F32), sds((b, s, GLA_W), F32),
                 sds((b, s, GLA_QKW), F32), sds((b, s, GLA_W), F32)]
    return pl.pallas_call(
        _proj_kernel,
        grid=(b, nt),
        in_specs=[nat(d), full(pre_norm_w), tab, tab] + [full(w) for w in wts + wns + gate_ops],
        out_specs=out_specs,
        out_shape=out_shape,
        compiler_params=pltpu.CompilerParams(dimension_semantics=("arbitrary", "arbitrary"),
                                             vmem_limit_bytes=VMEM_LIMIT),
        name="prenorm_in_proj",
    )(x, pre_norm_w, cos8, sin8, *wts, *wns, *gate_ops)


def _key_to_f32(key):
    return pltpu.bitcast(key ^ ((key >> 31) & 0x7FFFFFFF), F32)


def _dsa_kernel(aqt_ref, ak_ref, avt_ref, iqt_ref, ik_ref, iwt_ref, ag_ref, out_ref,
                score_scr, coarse_scr, bias_scr, qbd_scr, m_scr, acc_scr, *, ksel):
    tq, tk = DSA_TQ, DSA_TK
    i = pl.program_id(1)
    nck = (i * tq + tq + tk - 1) // tk
    groups = tk // SUBLANES

    key_iota = lax.broadcasted_iota(I32, (tk, tq), 0)
    q_pos = i * tq + lax.broadcasted_iota(I32, (tk, tq), 1)

    iqt = iqt_ref[0]
    q_cat = jnp.concatenate([iqt[h * IDX_HD:(h + 1) * IDX_HD] for h in range(IDX_HEADS)], axis=1)
    q_cat = jnp.concatenate([q_cat, jnp.zeros((LANES - IDX_HD, IDX_HEADS * tq), BF16)], axis=0)
    iwt = iwt_ref[0]
    w_rows = [iwt[h:h + 1, :] for h in range(IDX_HEADS)]

    n_pair = (nck + 1) // 2

    def score_pair(c2, carry):
        chunks = (2 * c2, 2 * c2 + 1)
        rels = [jnp.dot(ik_ref[0, pl.ds(pl.multiple_of(c * tk, tk), tk), :], q_cat,
                        preferred_element_type=F32) for c in chunks]
        for c, rel in zip(chunks, rels):
            sc = w_rows[0] * jnp.maximum(rel[:, 0:tq], 0.0)
            for h in range(1, IDX_HEADS):
                sc = sc + w_rows[h] * jnp.maximum(rel[:, h * tq:(h + 1) * tq], 0.0)
            sc = jnp.where(c * tk + key_iota <= q_pos, sc, NEG)
            score_scr[c] = sc
            coarse_scr[c] = sc.astype(BF16)
        return carry

    lax.fori_loop(0, n_pair, score_pair, 0)

    assert tq == 2 * tk and ksel % (COUNT_CHAINS * 2 * SUBLANES) == 0
    diag_rows = [max((l + 1) * LANES, ksel) for l in range(tq // LANES)]

    def diag_tiles(scr, rows_per_vreg):
        for l, rows in enumerate(diag_rows):
            sl = slice(l * LANES, (l + 1) * LANES)
            parts = [scr[2 * (n_pair - 1) + c, :min(rows - c * tk, tk), sl] for c in range(2) if rows > c * tk]
            keys = jnp.concatenate(parts, axis=0) if len(parts) > 1 else parts[0]
            yield sl, keys.reshape(COUNT_CHAINS, rows // (COUNT_CHAINS * rows_per_vreg), rows_per_vreg, LANES)

    def count(cmp, thr):
        def body(c2, acc):
            sc = score_scr[pl.ds(2 * c2, 2)].reshape(COUNT_CHAINS, 2 * groups // COUNT_CHAINS, SUBLANES, tq)
            return acc + jnp.sum(jnp.where(cmp(sc, thr[None]), 1, 0), axis=1)
        acc = lax.fori_loop(0, n_pair - 1, body, jnp.zeros((COUNT_CHAINS, SUBLANES, tq), I32))
        diag = [jnp.sum(jnp.where(cmp(sc, thr[None, :, sl]), 1, 0), axis=1) for sl, sc in diag_tiles(score_scr, SUBLANES)]
        acc = acc + jnp.concatenate(diag, axis=2)
        return jnp.sum(acc.reshape(COUNT_CHAINS * SUBLANES, tq), axis=0, keepdims=True)

    def count_coarse(cand_b):
        packed_rows = 2 * SUBLANES
        one, zero = jnp.ones((), BF16), jnp.zeros((), BF16)
        def body(c2, acc):
            hi = coarse_scr[pl.ds(2 * c2, 2)].reshape(COUNT_CHAINS, 2 * tk // (COUNT_CHAINS * packed_rows),
                                                      packed_rows, tq)
            for g in range(hi.shape[1]):
                acc = acc + jnp.where(hi[:, g] >= cand_b[None], one, zero)
            return acc
        acc = lax.fori_loop(0, n_pair - 1, body, jnp.zeros((COUNT_CHAINS, packed_rows, tq), BF16))
        diag = []
        for sl, hi in diag_tiles(coarse_scr, packed_rows):
            part = jnp.zeros((COUNT_CHAINS, packed_rows, LANES), BF16)
            for g in range(hi.shape[1]):
                part = part + jnp.where(hi[:, g] >= cand_b[None, :, sl], one, zero)
            diag.append(part)
        acc = acc + jnp.concatenate(diag, axis=2)
        return jnp.sum(acc.astype(F32).reshape(COUNT_CHAINS * packed_rows, tq), axis=0, keepdims=True)

    def bisect_coarse(it, prefix):
        cand = prefix + lax.shift_left(jnp.int32(1), 15 - it)
        pattern = cand ^ ((cand >> 31) & 0x7FFF)
        cand_b = pltpu.bitcast(lax.shift_left(pattern, 16), F32).astype(BF16)
        cnt = count_coarse(cand_b)
        return jnp.where(cnt >= ksel, cand, prefix)

    assert (tk // (COUNT_CHAINS * SUBLANES)) * (coarse_scr.shape[0] // 2) <= 2 ** 8, "bf16 counts must stay exact"
    key16 = lax.fori_loop(0, 16, bisect_coarse, jnp.full((2 * SUBLANES, tq), -(2 ** 15), I32))[:SUBLANES]
    key_p = lax.shift_left(key16, 16) | ((key16 >> 31) & 0xFFFF)
    key_lo = key_p - 2 ** 16

    def bisect(it, prefix):
        cand = prefix + lax.shift_left(jnp.int32(1), 16 - it)
        cand_f = _key_to_f32(cand)
        cnt = count(lambda sc, t: sc >= t, cand_f)
        return jnp.where(cnt >= ksel, cand, prefix)

    vstar = _key_to_f32(lax.fori_loop(0, 17, bisect, key_lo))
    need_f = (ksel - count(lambda sc, t: sc > t, vstar)).astype(F32)

    tri = jnp.where(lax.broadcasted_iota(I32, (tk, tk), 1) <= lax.broadcasted_iota(I32, (tk, tk), 0),
                    1.0, 0.0).astype(BF16)

    def select_pair(c2, run):
        chunks = (2 * c2, 2 * c2 + 1)
        scs = [score_scr[c] for c in chunks]
        eqs = [sc == vstar[0:1] for sc in scs]
        ranks = [jnp.dot(tri, jnp.where(eq, 1.0, 0.0).astype(BF16), preferred_element_type=F32) for eq in eqs]
        for c, sc, eq, rank in zip(chunks, scs, eqs, ranks):
            rank = rank + run
            tie_bias = jnp.where(rank <= need_f, jnp.where(c * tk + key_iota <= q_pos, 0.0, NEG), NEG)
            bias_scr[c] = jnp.where(sc > vstar[0:1], 0.0, jnp.where(eq, tie_bias, NEG))
            run = rank[tk - 1:tk, :]
        return run

    lax.fori_loop(0, n_pair, select_pair, jnp.zeros((1, tq), F32))

    heads_per_half = MXU_DIM // ATT_HD
    n_half = ATT_W // MXU_DIM
    row_head = lax.broadcasted_iota(I32, (MXU_DIM, tq), 0) // ATT_HD
    for h in range(ATT_HEADS):
        half, hh = divmod(h, heads_per_half)
        qh = aqt_ref[0, half * MXU_DIM:(half + 1) * MXU_DIM, :]
        qbd_scr[half, :, hh * tq:(hh + 1) * tq] = jnp.where(row_head == hh, qh, jnp.zeros_like(qh))
    m_scr[...] = jnp.full_like(m_scr, NEG)
    acc_scr[...] = jnp.zeros_like(acc_scr)

    def att_pair(c2, carry):
        s_all = {}
        for sub in range(2):
            for half in range(n_half):
                rows = pl.ds(pl.multiple_of((2 * c2 + sub) * tk, tk), tk)
                s_all[sub, half] = jnp.dot(ak_ref[0, rows, half * MXU_DIM:(half + 1) * MXU_DIM], qbd_scr[half],
                                           preferred_element_type=F32)
        for sub in range(2):
            for half in range(n_half):
                for hh in range(heads_per_half):
                    h = half * heads_per_half + hh
                    vt = avt_ref[0, 2 * c2 + sub, h * ATT_VROWS:(h + 1) * ATT_VROWS, :]
                    s = s_all[sub, half][:, hh * tq:(hh + 1) * tq] + bias_scr[2 * c2 + sub]
                    m_old = m_scr[h]
                    m_new = jnp.maximum(m_old, jnp.max(s, axis=0, keepdims=True))
                    alpha = jnp.exp2(m_old - m_new)
                    p = jnp.exp2(s - m_new).astype(BF16)
                    acc_scr[h] = alpha * acc_scr[h] + jnp.dot(vt, p, preferred_element_type=F32)
                    m_scr[h] = m_new
        return carry

    lax.fori_loop(0, n_pair, att_pair, 0)

    att_t = jnp.concatenate([acc_scr[h, :ATT_HD] * (1.0 / acc_scr[h, ATT_HD:ATT_HD + 1])
                             for h in range(ATT_HEADS)], axis=0)
    out_ref[0] = (att_t.T * ag_ref[0]).astype(BF16)


def _dsa_attention(aqt, ak, avt, iqt, ik, iwt, ag):
    b, s, _ = ak.shape
    tq, tk = DSA_TQ, DSA_TK
    nck = s // tk
    ksel = min(TOPK_MAX, s // 4)
    qt = lambda rows: pl.BlockSpec((1, rows, tq), lambda bi, i: (bi, 0, i))
    seq = lambda width: pl.BlockSpec((1, s, width), lambda bi, i: (bi, 0, 0))
    qn = pl.BlockSpec((1, tq, ATT_W), lambda bi, i: (bi, i, 0))
    return pl.pallas_call(
        functools.partial(_dsa_kernel, ksel=ksel),
        grid=(b, s // tq),
        in_specs=[qt(ATT_W), seq(ATT_W),
                  pl.BlockSpec((1, nck, ATT_HEADS * ATT_VROWS, tk), lambda bi, i: (bi, 0, 0, 0)),
                  qt(IDX_QW), seq(LANES), qt(SUBLANES), qn],
        out_specs=qn,
        out_shape=jax.ShapeDtypeStruct((b, s, ATT_W), BF16),
        scratch_shapes=[pltpu.VMEM((nck, tk, tq), F32), pltpu.VMEM((nck, tk, tq), BF16),
                        pltpu.VMEM((nck, tk, tq), F32),
                        pltpu.VMEM((ATT_W // MXU_DIM, MXU_DIM, (MXU_DIM // ATT_HD) * tq), BF16),
                        pltpu.VMEM((ATT_HEADS, 1, tq), F32),
                        pltpu.VMEM((ATT_HEADS, ATT_VROWS, tq), F32)],
        compiler_params=pltpu.CompilerParams(dimension_semantics=("arbitrary", "arbitrary"),
                                             vmem_limit_bytes=VMEM_LIMIT),
        name="dsa_attention",
    )(aqt, ak, avt, iqt, ik, iwt, ag)


def _gla_kernel(gqk_ref, gv_ref, glog_ref, gg_ref, nw_ref, out_ref, state_scr):
    c_len, sub = GLA_CHUNK, GLA_SUB
    nb = gqk_ref.shape[0]
    n_z = c_len * sub

    @pl.when(pl.program_id(0) == 0)
    def _():
        state_scr[...] = jnp.zeros_like(state_scr)

    t_i = lax.broadcasted_iota(I32, (c_len, GLA_QKW), 0)
    s_i = lax.broadcasted_iota(I32, (sub, GLA_QKW), 0)
    j_i = lax.broadcasted_iota(I32, (c_len, GLA_QKW), 1) % c_len
    tril = jnp.where(lax.broadcasted_iota(I32, (c_len, c_len), 0)
                     >= lax.broadcasted_iota(I32, (c_len, c_len), 1), 1.0, 0.0).astype(BF16)
    lane_head = lax.broadcasted_iota(I32, (c_len, GLA_QKW), 1) // GLA_DK

    def stack_heads(a):
        return jnp.concatenate([jnp.where(lane_head == h, a, 0.0) for h in range(GLA_HEADS)],
                               axis=0).astype(BF16)

    zr = lax.broadcasted_iota(I32, (n_z, GLA_QKW), 0)
    zc = lax.broadcasted_iota(I32, (n_z, GLA_QKW), 1)
    pick = (zc % c_len) == (zr // (sub * sub)) * sub + zr % sub
    hr = lax.broadcasted_iota(I32, (GLA_QKW, GLA_QKW), 0) // GLA_DK
    hc = lax.broadcasted_iota(I32, (GLA_QKW, GLA_QKW), 1) // c_len
    head_rep = jnp.where(hr == hc, 1.0, 0.0).astype(BF16)
    sr = lax.broadcasted_iota(I32, (c_len, n_z), 0)
    sc_ = lax.broadcasted_iota(I32, (c_len, n_z), 1)
    row_sel = jnp.where(sr == sc_ // sub, 1.0, 0.0).astype(BF16)
    vr = lax.broadcasted_iota(I32, (GLA_HEADS * c_len, GLA_W), 0) // c_len
    vc = lax.broadcasted_iota(I32, (GLA_HEADS * c_len, GLA_W), 1) // GLA_DV
    v_diag = vr == vc
    spans = [c_len >> (lv + 1) for lv in range((c_len // sub).bit_length() - 1)]

    units = range(nb * (gqk_ref.shape[1] // c_len))
    bat = lambda u: u % nb
    tsl = lambda u: slice((u // nb) * c_len, (u // nb + 1) * c_len)
    cur_state = {bi: state_scr[bi] for bi in range(nb)}
    q, k, vb, b, b_last, state_t, o, a, rep = ({} for _ in range(9))

    for u in units:
        gqk = gqk_ref[bat(u), tsl(u), :]
        q[u], k[u] = gqk[:, :GLA_QKW], gqk[:, GLA_QKW:]
        vb[u] = gv_ref[bat(u), tsl(u), :].astype(BF16)
        g = glog_ref[bat(u), tsl(u), :]
        g1 = g.astype(BF16)
        r1 = g - g1.astype(F32)
        g2 = r1.astype(BF16)
        g3 = (r1 - g2.astype(F32)).astype(BF16)
        b[u] = (jnp.dot(tril, g1, preferred_element_type=F32) + jnp.dot(tril, g2, preferred_element_type=F32)
                 + jnp.dot(tril, g3, preferred_element_type=F32))
        b_last[u] = b[u][c_len - 1:c_len, :]

    for u in units:
        state_t[u] = cur_state[bat(u)]
        o_stack = _nt_dot(stack_heads(q[u] * jnp.exp(b[u])), state_t[u].astype(BF16))
        o[u] = jnp.concatenate([o_stack[h * c_len:(h + 1) * c_len] for h in range(GLA_HEADS)], axis=1)

        k_bd = stack_heads(k[u] * jnp.exp(b_last[u] - b[u]))
        v_st = jnp.concatenate([vb[u][:, h * GLA_DV:(h + 1) * GLA_DV] for h in range(GLA_HEADS)], axis=0)
        cur_state[bat(u)] = jnp.exp(b_last[u]) * state_t[u] + _tn_dot(v_st, k_bd)

        for span in spans:
            blk = 2 * span
            ref = jnp.concatenate([jnp.broadcast_to(b[u][m * blk + span - 1:m * blk + span, :], (blk, GLA_QKW))
                                   for m in range(c_len // blk)], axis=0)
            upper = (t_i % blk) >= span
            q_l = q[u] * jnp.exp(jnp.where(upper, b[u] - ref, NEG))
            k_l = k[u] * jnp.exp(jnp.where(upper, NEG, ref - b[u]))
            a_l = _nt_dot(q_l.astype(BF16), stack_heads(k_l))
            if blk < c_len:
                a_l = jnp.where(t_i // blk == j_i // blk, a_l, 0.0)
            a[u] = a_l if u not in a else a[u] + a_l

    for bi in range(nb):
        state_scr[bi] = cur_state[bi]

    for u in units:
        zs = []
        for t in range(c_len):
            blk0 = (t // sub) * sub
            rs = slice(blk0, blk0 + sub)
            dec = jnp.exp(jnp.where(s_i <= t - blk0, b[u][t:t + 1, :] - b[u][rs], NEG))
            zs.append(q[u][t:t + 1, :] * k[u][rs] * dec)
        zmat = jnp.concatenate(zs, axis=0).astype(BF16)
        rep[u] = jnp.dot(zmat, head_rep, preferred_element_type=F32)

    for u in units:
        a_all = a[u] + jnp.dot(row_sel, jnp.where(pick, rep[u], 0.0).astype(BF16), preferred_element_type=F32)
        v_bd = jnp.where(v_diag, jnp.concatenate([vb[u]] * GLA_HEADS, axis=0), jnp.zeros((), BF16))
        o_u = o[u] + jnp.dot(a_all.astype(BF16), v_bd, preferred_element_type=F32)
        outs = []
        for h in range(GLA_HEADS):
            oh = o_u[:, h * GLA_DV:(h + 1) * GLA_DV]
            oh = oh * lax.rsqrt(jnp.mean(oh * oh, axis=-1, keepdims=True) + NORM_EPS)
            outs.append(oh * nw_ref[...])
        out_ref[bat(u), tsl(u), :] = (jnp.concatenate(outs, axis=1) * gg_ref[bat(u), tsl(u), :]).astype(BF16)


def _gla(gqk, gv, glog, gg, norm_w):
    b, s, _ = gqk.shape
    c_len = GLA_CHUNK
    step = min(GLA_STEP, s)
    blk = lambda width: pl.BlockSpec((b, step, width), lambda c: (0, c, 0))
    nw2 = norm_w.reshape(1, GLA_DV)
    return pl.pallas_call(
        _gla_kernel,
        grid=(s // step,),
        in_specs=[blk(2 * GLA_QKW), blk(GLA_W), blk(GLA_QKW), blk(GLA_W),
                  pl.BlockSpec(nw2.shape, lambda c: (0, 0))],
        out_specs=blk(GLA_W),
        out_shape=jax.ShapeDtypeStruct((b, s, GLA_W), BF16),
        scratch_shapes=[pltpu.VMEM((b, GLA_DV, GLA_QKW), F32)],
        compiler_params=pltpu.CompilerParams(dimension_semantics=("arbitrary",),
                                             vmem_limit_bytes=VMEM_LIMIT),
        name="gla_chunked",
    )(gqk, gv, glog, gg, nw2)


def _out_kernel(att_ref, gla_ref, wa_ref, wg_ref, x_ref, nw_ref, out_ref):
    mixed = jnp.dot(att_ref[...], wa_ref[...], preferred_element_type=F32) + \
        jnp.dot(gla_ref[...], wg_ref[...], preferred_element_type=F32)
    y = mixed * lax.rsqrt(jnp.mean(mixed * mixed, axis=-1, keepdims=True) + NORM_EPS)
    out_ref[...] = x_ref[...] + y * nw_ref[...]


def _out_project(att2, gla2, w_out, x2, post_norm_w):
    n, d = x2.shape
    tm = min(OUT_TM, n)
    wa, wg = w_out[:ATT_W].astype(BF16), w_out[ATT_W:].astype(BF16)
    row = lambda width: pl.BlockSpec((tm, width), lambda i: (i, 0))
    full = lambda a: pl.BlockSpec(a.shape, lambda i: (0, 0))
    return pl.pallas_call(
        _out_kernel,
        grid=(n // tm,),
        in_specs=[row(ATT_W), row(GLA_W), full(wa), full(wg), row(d), full(post_norm_w)],
        out_specs=row(d),
        out_shape=jax.ShapeDtypeStruct((n, d), F32),
        compiler_params=pltpu.CompilerParams(dimension_semantics=("arbitrary",),
                                             vmem_limit_bytes=VMEM_LIMIT),
        name="out_proj_postnorm",
    )(att2, gla2, wa, wg, x2, post_norm_w)


def kernel(x, positions, w_in, w_gla_gate_up, b_gla_gate, gla_norm_w, w_out, pre_norm_w, post_norm_w):
    b, s, d = x.shape
    n = b * s
    assert w_in.shape[0] == d and w_out.shape == (ATT_W + GLA_W, d)
    assert s % PROJ_TM == 0 and s % DSA_TQ == 0 and DSA_TQ == 2 * DSA_TK
    assert s % min(GLA_STEP, s) == 0 and min(GLA_STEP, s) % GLA_CHUNK == 0 and n % min(OUT_TM, n) == 0
    cos8, sin8 = _rope_tables(positions)
    aqt, ak, avt, iqt, ik, iwt, ag, gqk, gv, glog, gg = _project(x, pre_norm_w.reshape(1, d), cos8, sin8, w_in,
                                                                  w_gla_gate_up, b_gla_gate)
    att = _dsa_attention(aqt, ak, avt, iqt, ik, iwt, ag)
    gla = _gla(gqk, gv, glog, gg, gla_norm_w)
    out = _out_project(att.reshape(n, ATT_W), gla.reshape(n, GLA_W), w_out, x.reshape(n, d),
                       post_norm_w.reshape(1, d))
    return out.reshape(b, s, d)
```

```python
import functools

import jax
import jax.numpy as jnp
import numpy as np
from jax import lax
from jax.experimental import pallas as pl
from jax.experimental.pallas import tpu as pltpu

F32 = jnp.float32
BF16 = jnp.bfloat16
I32 = jnp.int32

ATT_HEADS = 8
ATT_HD = 64
ATT_W = ATT_HEADS * ATT_HD
IDX_HEADS = 4
IDX_HD = 64
IDX_QW = IDX_HEADS * IDX_HD
TOPK_MAX = 256
GLA_HEADS = 4
GLA_DK = 64
GLA_DV = 128
GLA_QKW = GLA_HEADS * GLA_DK
GLA_W = GLA_HEADS * GLA_DV
GLA_GATE_RANK = 16
GLA_TAU = 16.0
GLA_CHUNK = 64
GLA_SUB = 8
GLA_STEP = 256
ROPE_THETA = 500000.0
ROT_DIM = ATT_HD // 4
ROT_HALF = ROT_DIM // 2
NORM_EPS = 1e-6
NEG = float(np.float32(-1e30))
LOG2E = 1.4426950408889634

SUBLANES = 8
LANES = 128
MXU_DIM = 256
VMEM_LIMIT = 56 * 1024 * 1024

PROJ_TM = 1024
OUT_TM = 1024
DSA_TQ = 512
DSA_TK = 256
COUNT_CHAINS = 4
ATT_VROWS = ATT_HD + 16


def _nt_dot(a, b):
    return lax.dot_general(a, b, (((1,), (1,)), ((), ())), preferred_element_type=F32)


def _tn_dot(a, b):
    return lax.dot_general(a, b, (((0,), (0,)), ((), ())), preferred_element_type=F32)


def _rope_table_kernel(pos_ref, inv_ref, cos_ref, sin_ref):
    ang = pos_ref[...].astype(F32) * inv_ref[...]
    cos_ref[...] = jnp.cos(ang)
    sin_ref[...] = jnp.sin(ang)


def _rope_tables(positions):
    n = positions.size
    inv = ROPE_THETA ** (-jnp.arange(0, ROT_DIM, 2, dtype=F32) / ROT_DIM)
    return pl.pallas_call(
        _rope_table_kernel,
        out_shape=(jax.ShapeDtypeStruct((ROT_HALF, n), F32),) * 2,
        name="rope_tables",
    )(positions.reshape(1, n), inv.reshape(ROT_HALF, 1))


def _proj_kernel(x_ref, pnw_ref, cos_ref, sin_ref,
                 wt_all, w_ag, w_gqk, w_gv, w_gg,
                 wupt_hi, wupt_lo, bup_ref,
                 aqt_o, ak_o, avt_o, iqt_o, ik_o, iwt_o, ag_o, gqk_o, gv_o, glog_o, gg_o):
    x = x_ref[0]
    ms = jnp.mean(x * x, axis=-1, keepdims=True)
    h = (x * lax.rsqrt(ms + NORM_EPS)) * pnw_ref[...]
    hb = h.astype(BF16)
    cos8, sin8 = cos_ref[...], sin_ref[...]

    def mm(w_ref):
        return jnp.dot(hb, w_ref[...], preferred_element_type=F32)

    def rope_t(yt):
        pieces = []
        for hd in range(yt.shape[0] // ATT_HD):
            rows = yt[hd * ATT_HD:(hd + 1) * ATT_HD]
            x1, x2 = rows[:ROT_HALF], rows[ROT_HALF:ROT_DIM]
            pieces += [x1 * cos8 - x2 * sin8, x2 * cos8 + x1 * sin8, rows[ROT_DIM:]]
        return jnp.concatenate(pieces, axis=0)

    yt = _nt_dot(wt_all[...], hb)
    ikt, o = yt[:LANES], LANES
    aq_t, o = yt[o:o + ATT_W], o + ATT_W
    ak_t, o = yt[o:o + ATT_W], o + ATT_W
    iq_t, o = yt[o:o + IDX_QW], o + IDX_QW
    av_t = yt[o:o + ATT_W]
    iw_row = IDX_HD + GLA_GATE_RANK
    iwt_o[0] = ikt[iw_row:iw_row + SUBLANES] * (IDX_HEADS ** -0.5 * IDX_HD ** -0.5)
    aqt_o[0] = (rope_t(aq_t) * (ATT_HD ** -0.5 * LOG2E)).astype(BF16)
    ik_o[0] = jnp.concatenate([rope_t(ikt[:IDX_HD]), jnp.zeros_like(ikt[IDX_HD:])], axis=0).T.astype(BF16)
    ak_o[0] = rope_t(ak_t).T.astype(BF16)
    ikt_hi = ikt.astype(BF16)
    ikt_lo = (ikt - ikt_hi.astype(F32)).astype(BF16)
    zt = (jnp.dot(wupt_hi[...], ikt_hi, preferred_element_type=F32)
          + jnp.dot(wupt_lo[...], ikt_hi, preferred_element_type=F32)
          + jnp.dot(wupt_hi[...], ikt_lo, preferred_element_type=F32))
    a_gate = mm(w_ag)
    z = zt.T + bup_ref[...]
    glog_o[0] = (jnp.minimum(z, 0.0) - jnp.log1p(jnp.exp(-jnp.abs(z)))) * (1.0 / GLA_TAU)
    g_gate = mm(w_gg)
    ag_o[0] = a_gate * jax.nn.sigmoid(a_gate)
    iqt_o[0] = rope_t(iq_t).astype(BF16)
    gg_o[0] = g_gate * jax.nn.sigmoid(g_gate)
    gqk = mm(w_gqk)
    lane = lax.broadcasted_iota(I32, gqk.shape, 1)
    gqk_o[0] = jnp.where(lane < GLA_QKW, gqk * (GLA_DK ** -0.5), gqk)
    avt = av_t
    ones_rows = jnp.ones((ATT_VROWS - ATT_HD, avt.shape[1]), F32)
    avt = jnp.concatenate([piece for hd in range(ATT_HEADS)
                           for piece in (avt[hd * ATT_HD:(hd + 1) * ATT_HD], ones_rows)], axis=0).astype(BF16)
    for c in range(avt_o.shape[1]):
        avt_o[0, c] = avt[:, c * DSA_TK:(c + 1) * DSA_TK]
    gv_o[0] = mm(w_gv)


def _split_w_in(w_in):
    sizes = (ATT_W, ATT_W, ATT_W, IDX_QW, IDX_HD, IDX_HEADS, ATT_W,
             GLA_QKW, GLA_QKW, GLA_W, GLA_GATE_RANK, GLA_W)
    offs = np.cumsum((0,) + sizes)
    seg = [w_in[:, offs[i]:offs[i + 1]] for i in range(len(sizes))]
    a_q, a_k, a_v, i_q, i_k, i_w, a_g, g_q, g_k, g_v, g_lr, g_g = seg
    d = w_in.shape[0]
    pad = lambda w, n: jnp.concatenate([w, jnp.zeros((d, n - w.shape[1]), w.dtype)], axis=1)
    small = pad(jnp.concatenate([i_k, g_lr, i_w], axis=1), LANES)
    transposed = jnp.concatenate([small, a_q, a_k, i_q, a_v], axis=1).astype(BF16).T
    natural = [a_g, jnp.concatenate([g_q, g_k], axis=1), g_v, g_g]
    return [transposed], [w.astype(BF16) for w in natural]


def _project(x, pre_norm_w, cos8, sin8, w_in, w_up, b_up):
    b, s, d = x.shape
    wupt = jnp.concatenate([jnp.zeros((GLA_QKW, IDX_HD), F32), w_up.T,
                            jnp.zeros((GLA_QKW, LANES - IDX_HD - GLA_GATE_RANK), F32)], axis=1)
    wupt_hi = wupt.astype(BF16)
    wupt_lo = (wupt - wupt_hi.astype(F32)).astype(BF16)
    gate_ops = [wupt_hi, wupt_lo, b_up.reshape(1, GLA_QKW)]
    tm = min(PROJ_TM, s)
    nt = s // tm
    cpt = tm // DSA_TK
    wts, wns = _split_w_in(w_in)
    full = lambda a: pl.BlockSpec(a.shape, lambda bi, i: (0,) * a.ndim, pipeline_mode=pl.Buffered(1))
    nat = lambda width: pl.BlockSpec((1, tm, width), lambda bi, i: (bi, i, 0))
    tr = lambda rows: pl.BlockSpec((1, rows, tm), lambda bi, i: (bi, 0, i))
    tab = pl.BlockSpec((ROT_HALF, tm), lambda bi, i: (0, bi * nt + i))
    vrows = ATT_HEADS * ATT_VROWS
    out_specs = [tr(ATT_W), nat(ATT_W), pl.BlockSpec((1, cpt, vrows, DSA_TK), lambda bi, i: (bi, i, 0, 0)),
                 tr(IDX_QW), nat(LANES), tr(SUBLANES),
                 nat(ATT_W), nat(2 * GLA_QKW), nat(GLA_W), nat(GLA_QKW), nat(GLA_W)]
    sds = jax.ShapeDtypeStruct
    out_shape = [sds((b, ATT_W, s), BF16), sds((b, s, ATT_W), BF16), sds((b, s // DSA_TK, vrows, DSA_TK), BF16),
                 sds((b, IDX_QW, s), BF16), sds((b, s, LANES), BF16), sds((b, SUBLANES, s), F32),
                 sds((b, s, ATT_W), F32), sds((b, s, 2 * GLA_QKW), F32), sds((b, s, GLA_W), F32),
                 sds((b, s, GLA_QKW), F32), sds((b, s, GLA_W), F32)]
    return pl.pallas_call(
        _proj_kernel,
        grid=(b, nt),
        in_specs=[nat(d), full(pre_norm_w), tab, tab] + [full(w) for w in wts + wns + gate_ops],
        out_specs=out_specs,
        out_shape=out_shape,
        compiler_params=pltpu.CompilerParams(dimension_semantics=("arbitrary", "arbitrary"),
                                             vmem_limit_bytes=VMEM_LIMIT),
        name="prenorm_in_proj",
    )(x, pre_norm_w, cos8, sin8, *wts, *wns, *gate_ops)


def _key_to_f32(key):
    return pltpu.bitcast(key ^ ((key >> 31) & 0x7FFFFFFF), F32)


def _dsa_kernel(aqt_ref, ak_ref, avt_ref, iqt_ref, ik_ref, iwt_ref, ag_ref, out_ref,
                score_scr, coarse_scr, bias_scr, qbd_scr, m_scr, acc_scr, *, ksel):
    tq, tk = DSA_TQ, DSA_TK
    i = pl.program_id(1)
    nck = (i * tq + tq + tk - 1) // tk
    groups = tk // SUBLANES

    key_iota = lax.broadcasted_iota(I32, (tk, tq), 0)
    q_pos = i * tq + lax.broadcasted_iota(I32, (tk, tq), 1)

    iqt = iqt_ref[0]
    q_cat = jnp.concatenate([iqt[h * IDX_HD:(h + 1) * IDX_HD] for h in range(IDX_HEADS)], axis=1)
    q_cat = jnp.concatenate([q_cat, jnp.zeros((LANES - IDX_HD, IDX_HEADS * tq), BF16)], axis=0)
    iwt = iwt_ref[0]
    w_rows = [iwt[h:h + 1, :] for h in range(IDX_HEADS)]

    n_pair = (nck + 1) // 2

    def score_pair(c2, carry):
        chunks = (2 * c2, 2 * c2 + 1)
        rels = [jnp.dot(ik_ref[0, pl.ds(pl.multiple_of(c * tk, tk), tk), :], q_cat,
                        preferred_element_type=F32) for c in chunks]
        for c, rel in zip(chunks, rels):
            sc = w_rows[0] * jnp.maximum(rel[:, 0:tq], 0.0)
            for h in range(1, IDX_HEADS):
                sc = sc + w_rows[h] * jnp.maximum(rel[:, h * tq:(h + 1) * tq], 0.0)
            sc = jnp.where(c * tk + key_iota <= q_pos, sc, NEG)
            score_scr[c] = sc
            coarse_scr[c] = sc.astype(BF16)
        return carry

    lax.fori_loop(0, n_pair, score_pair, 0)

    assert tq == 2 * tk and ksel % (COUNT_CHAINS * 2 * SUBLANES) == 0
    diag_rows = [max((l + 1) * LANES, ksel) for l in range(tq // LANES)]

    def diag_tiles(scr, rows_per_vreg):
        for l, rows in enumerate(diag_rows):
            sl = slice(l * LANES, (l + 1) * LANES)
            parts = [scr[2 * (n_pair - 1) + c, :min(rows - c * tk, tk), sl] for c in range(2) if rows > c * tk]
            keys = jnp.concatenate(parts, axis=0) if len(parts) > 1 else parts[0]
            yield sl, keys.reshape(COUNT_CHAINS, rows // (COUNT_CHAINS * rows_per_vreg), rows_per_vreg, LANES)

    def count(cmp, thr):
        def body(c2, acc):
            sc = score_scr[pl.ds(2 * c2, 2)].reshape(COUNT_CHAINS, 2 * groups // COUNT_CHAINS, SUBLANES, tq)
            return acc + jnp.sum(jnp.where(cmp(sc, thr[None]), 1, 0), axis=1)
        acc = lax.fori_loop(0, n_pair - 1, body, jnp.zeros((COUNT_CHAINS, SUBLANES, tq), I32))
        diag = [jnp.sum(jnp.where(cmp(sc, thr[None, :, sl]), 1, 0), axis=1) for sl, sc in diag_tiles(score_scr, SUBLANES)]
        acc = acc + jnp.concatenate(diag, axis=2)
        return jnp.sum(acc.reshape(COUNT_CHAINS * SUBLANES, tq), axis=0, keepdims=True)

    def count_coarse(cand_b):
        packed_rows = 2 * SUBLANES
        one, zero = jnp.ones((), BF16), jnp.zeros((), BF16)
        def body(c2, acc):
            hi = coarse_scr[pl.ds(2 * c2, 2)].reshape(COUNT_CHAINS, 2 * tk // (COUNT_CHAINS * packed_rows),
                                                      packed_rows, tq)
            for g in range(hi.shape[1]):
                acc = acc + jnp.where(hi[:, g] >= cand_b[None], one, zero)
            return acc
        acc = lax.fori_loop(0, n_pair - 1, body, jnp.zeros((COUNT_CHAINS, packed_rows, tq), BF16))
        diag = []
        for sl, hi in diag_tiles(coarse_scr, packed_rows):
            part = jnp.zeros((COUNT_CHAINS, packed_rows, LANES), BF16)
            for g in range(hi.shape[1]):
                part = part + jnp.where(hi[:, g] >= cand_b[None, :, sl], one, zero)
            diag.append(part)
        acc = acc + jnp.concatenate(diag, axis=2)
        return jnp.sum(acc.astype(F32).reshape(COUNT_CHAINS * packed_rows, tq), axis=0, keepdims=True)

    def bisect_coarse(it, prefix):
        cand = prefix + lax.shift_left(jnp.int32(1), 15 - it)
        pattern = cand ^ ((cand >> 31) & 0x7FFF)
        cand_b = pltpu.bitcast(lax.shift_left(pattern, 16), F32).astype(BF16)
        cnt = count_coarse(cand_b)
        return jnp.where(cnt >= ksel, cand, prefix)

    assert (tk // (COUNT_CHAINS * SUBLANES)) * (coarse_scr.shape[0] // 2) <= 2 ** 8, "bf16 counts must stay exact"
    key16 = lax.fori_loop(0, 16, bisect_coarse, jnp.full((2 * SUBLANES, tq), -(2 ** 15), I32))[:SUBLANES]
    key_p = lax.shift_left(key16, 16) | ((key16 >> 31) & 0xFFFF)
    key_lo = key_p - 2 ** 16

    def bisect(it, prefix):
        cand = prefix + lax.shift_left(jnp.int32(1), 16 - it)
        cand_f = _key_to_f32(cand)
        cnt = count(lambda sc, t: sc >= t, cand_f)
        return jnp.where(cnt >= ksel, cand, prefix)

    vstar = _key_to_f32(lax.fori_loop(0, 17, bisect, key_lo))
    need_f = (ksel - count(lambda sc, t: sc > t, vstar)).astype(F32)

    tri = jnp.where(lax.broadcasted_iota(I32, (tk, tk), 1) <= lax.broadcasted_iota(I32, (tk, tk), 0),
                    1.0, 0.0).astype(BF16)

    def select_pair(c2, run):
        chunks = (2 * c2, 2 * c2 + 1)
        scs = [score_scr[c] for c in chunks]
        eqs = [sc == vstar[0:1] for sc in scs]
        ranks = [jnp.dot(tri, jnp.where(eq, 1.0, 0.0).astype(BF16), preferred_element_type=F32) for eq in eqs]
        for c, sc, eq, rank in zip(chunks, scs, eqs, ranks):
            rank = rank + run
            tie_bias = jnp.where(rank <= need_f, jnp.where(c * tk + key_iota <= q_pos, 0.0, NEG), NEG)
            bias_scr[c] = jnp.where(sc > vstar[0:1], 0.0, jnp.where(eq, tie_bias, NEG))
            run = rank[tk - 1:tk, :]
        return run

    lax.fori_loop(0, n_pair, select_pair, jnp.zeros((1, tq), F32))

    heads_per_half = MXU_DIM // ATT_HD
    n_half = ATT_W // MXU_DIM
    row_head = lax.broadcasted_iota(I32, (MXU_DIM, tq), 0) // ATT_HD
    for h in range(ATT_HEADS):
        half, hh = divmod(h, heads_per_half)
        qh = aqt_ref[0, half * MXU_DIM:(half + 1) * MXU_DIM, :]
        qbd_scr[half, :, hh * tq:(hh + 1) * tq] = jnp.where(row_head == hh, qh, jnp.zeros_like(qh))
    m_scr[...] = jnp.full_like(m_scr, NEG)
    acc_scr[...] = jnp.zeros_like(acc_scr)

    def att_pair(c2, carry):
        s_all = {}
        for sub in range(2):
            for half in range(n_half):
                rows = pl.ds(pl.multiple_of((2 * c2 + sub) * tk, tk), tk)
                s_all[sub, half] = jnp.dot(ak_ref[0, rows, half * MXU_DIM:(half + 1) * MXU_DIM], qbd_scr[half],
                                           preferred_element_type=F32)
        for sub in range(2):
            for half in range(n_half):
                for hh in range(heads_per_half):
                    h = half * heads_per_half + hh
                    vt = avt_ref[0, 2 * c2 + sub, h * ATT_VROWS:(h + 1) * ATT_VROWS, :]
                    s = s_all[sub, half][:, hh * tq:(hh + 1) * tq] + bias_scr[2 * c2 + sub]
                    m_old = m_scr[h]
                    m_new = jnp.maximum(m_old, jnp.max(s, axis=0, keepdims=True))
                    alpha = jnp.exp2(m_old - m_new)
                    p = jnp.exp2(s - m_new).astype(BF16)
                    acc_scr[h] = alpha * acc_scr[h] + jnp.dot(vt, p, preferred_element_type=F32)
                    m_scr[h] = m_new
        return carry

    lax.fori_loop(0, n_pair, att_pair, 0)

    att_t = jnp.concatenate([acc_scr[h, :ATT_HD] * (1.0 / acc_scr[h, ATT_HD:ATT_HD + 1])
                             for h in range(ATT_HEADS)], axis=0)
    out_ref[0] = (att_t.T * ag_ref[0]).astype(BF16)


def _dsa_attention(aqt, ak, avt, iqt, ik, iwt, ag):
    b, s, _ = ak.shape
    tq, tk = DSA_TQ, DSA_TK
    nck = s // tk
    ksel = min(TOPK_MAX, s // 4)
    qt = lambda rows: pl.BlockSpec((1, rows, tq), lambda bi, i: (bi, 0, i))
    seq = lambda width: pl.BlockSpec((1, s, width), lambda bi, i: (bi, 0, 0))
    qn = pl.BlockSpec((1, tq, ATT_W), lambda bi, i: (bi, i, 0))
    return pl.pallas_call(
        functools.partial(_dsa_kernel, ksel=ksel),
        grid=(b, s // tq),
        in_specs=[qt(ATT_W), seq(ATT_W),
                  pl.BlockSpec((1, nck, ATT_HEADS * ATT_VROWS, tk), lambda bi, i: (bi, 0, 0, 0)),
                  qt(IDX_QW), seq(LANES), qt(SUBLANES), qn],
        out_specs=qn,
        out_shape=jax.ShapeDtypeStruct((b, s, ATT_W), BF16),
        scratch_shapes=[pltpu.VMEM((nck, tk, tq), F32), pltpu.VMEM((nck, tk, tq), BF16),
                        pltpu.VMEM((nck, tk, tq), F32),
                        pltpu.VMEM((ATT_W // MXU_DIM, MXU_DIM, (MXU_DIM // ATT_HD) * tq), BF16),
                        pltpu.VMEM((ATT_HEADS, 1, tq), F32),
                        pltpu.VMEM((ATT_HEADS, ATT_VROWS, tq), F32)],
        compiler_params=pltpu.CompilerParams(dimension_semantics=("arbitrary", "arbitrary"),
                                             vmem_limit_bytes=VMEM_LIMIT),
        name="dsa_attention",
    )(aqt, ak, avt, iqt, ik, iwt, ag)


def _gla_kernel(gqk_ref, gv_ref, glog_ref, gg_ref, nw_ref, out_ref, state_scr):
    c_len, sub = GLA_CHUNK, GLA_SUB
    nb = gqk_ref.shape[0]
    n_z = c_len * sub

    @pl.when(pl.program_id(0) == 0)
    def _():
        state_scr[...] = jnp.zeros_like(state_scr)

    t_i = lax.broadcasted_iota(I32, (c_len, GLA_QKW), 0)
    s_i = lax.broadcasted_iota(I32, (sub, GLA_QKW), 0)
    j_i = lax.broadcasted_iota(I32, (c_len, GLA_QKW), 1) % c_len
    tril = jnp.where(lax.broadcasted_iota(I32, (c_len, c_len), 0)
                     >= lax.broadcasted_iota(I32, (c_len, c_len), 1), 1.0, 0.0).astype(BF16)
    lane_head = lax.broadcasted_iota(I32, (c_len, GLA_QKW), 1) // GLA_DK

    def stack_heads(a):
        return jnp.concatenate([jnp.where(lane_head == h, a, 0.0) for h in range(GLA_HEADS)],
                               axis=0).astype(BF16)

    zr = lax.broadcasted_iota(I32, (n_z, GLA_QKW), 0)
    zc = lax.broadcasted_iota(I32, (n_z, GLA_QKW), 1)
    pick = (zc % c_len) == (zr // (sub * sub)) * sub + zr % sub
    hr = lax.broadcasted_iota(I32, (GLA_QKW, GLA_QKW), 0) // GLA_DK
    hc = lax.broadcasted_iota(I32, (GLA_QKW, GLA_QKW), 1) // c_len
    head_rep = jnp.where(hr == hc, 1.0, 0.0).astype(BF16)
    sr = lax.broadcasted_iota(I32, (c_len, n_z), 0)
    sc_ = lax.broadcasted_iota(I32, (c_len, n_z), 1)
    row_sel = jnp.where(sr == sc_ // sub, 1.0, 0.0).astype(BF16)
    vr = lax.broadcasted_iota(I32, (GLA_HEADS * c_len, GLA_W), 0) // c_len
    vc = lax.broadcasted_iota(I32, (GLA_HEADS * c_len, GLA_W), 1) // GLA_DV
    v_diag = vr == vc
    spans = [c_len >> (lv + 1) for lv in range((c_len // sub).bit_length() - 1)]

    units = range(nb * (gqk_ref.shape[1] // c_len))
    bat = lambda u: u % nb
    tsl = lambda u: slice((u // nb) * c_len, (u // nb + 1) * c_len)
    cur_state = {bi: state_scr[bi] for bi in range(nb)}
    q, k, vb, b, b_last, state_t, o, a, rep = ({} for _ in range(9))

    for u in units:
        gqk = gqk_ref[bat(u), tsl(u), :]
        q[u], k[u] = gqk[:, :GLA_QKW], gqk[:, GLA_QKW:]
        vb[u] = gv_ref[bat(u), tsl(u), :].astype(BF16)
        g = glog_ref[bat(u), tsl(u), :]
        g1 = g.astype(BF16)
        r1 = g - g1.astype(F32)
        g2 = r1.astype(BF16)
        g3 = (r1 - g2.astype(F32)).astype(BF16)
        b[u] = (jnp.dot(tril, g1, preferred_element_type=F32) + jnp.dot(tril, g2, preferred_element_type=F32)
                 + jnp.dot(tril, g3, preferred_element_type=F32))
        b_last[u] = b[u][c_len - 1:c_len, :]

    for u in units:
        state_t[u] = cur_state[bat(u)]
        o_stack = _nt_dot(stack_heads(q[u] * jnp.exp(b[u])), state_t[u].astype(BF16))
        o[u] = jnp.concatenate([o_stack[h * c_len:(h + 1) * c_len] for h in range(GLA_HEADS)], axis=1)

        k_bd = stack_heads(k[u] * jnp.exp(b_last[u] - b[u]))
        v_st = jnp.concatenate([vb[u][:, h * GLA_DV:(h + 1) * GLA_DV] for h in range(GLA_HEADS)], axis=0)
        cur_state[bat(u)] = jnp.exp(b_last[u]) * state_t[u] + _tn_dot(v_st, k_bd)

        for span in spans:
            blk = 2 * span
            ref = jnp.concatenate([jnp.broadcast_to(b[u][m * blk + span - 1:m * blk + span, :], (blk, GLA_QKW))
                                   for m in range(c_len // blk)], axis=0)
            upper = (t_i % blk) >= span
            q_l = q[u] * jnp.exp(jnp.where(upper, b[u] - ref, NEG))
            k_l = k[u] * jnp.exp(jnp.where(upper, NEG, ref - b[u]))
            a_l = _nt_dot(q_l.astype(BF16), stack_heads(k_l))
            if blk < c_len:
                a_l = jnp.where(t_i // blk == j_i // blk, a_l, 0.0)
            a[u] = a_l if u not in a else a[u] + a_l

    for bi in range(nb):
        state_scr[bi] = cur_state[bi]

    for u in units:
        zs = []
        for t in range(c_len):
            blk0 = (t // sub) * sub
            rs = slice(blk0, blk0 + sub)
            dec = jnp.exp(jnp.where(s_i <= t - blk0, b[u][t:t + 1, :] - b[u][rs], NEG))
            zs.append(q[u][t:t + 1, :] * k[u][rs] * dec)
        zmat = jnp.concatenate(zs, axis=0).astype(BF16)
        rep[u] = jnp.dot(zmat, head_rep, preferred_element_type=F32)

    for u in units:
        a_all = a[u] + jnp.dot(row_sel, jnp.where(pick, rep[u], 0.0).astype(BF16), preferred_element_type=F32)
        v_bd = jnp.where(v_diag, jnp.concatenate([vb[u]] * GLA_HEADS, axis=0), jnp.zeros((), BF16))
        o_u = o[u] + jnp.dot(a_all.astype(BF16), v_bd, preferred_element_type=F32)
        outs = []
        for h in range(GLA_HEADS):
            oh = o_u[:, h * GLA_DV:(h + 1) * GLA_DV]
            oh = oh * lax.rsqrt(jnp.mean(oh * oh, axis=-1, keepdims=True) + NORM_EPS)
            outs.append(oh * nw_ref[...])
        out_ref[bat(u), tsl(u), :] = (jnp.concatenate(outs, axis=1) * gg_ref[bat(u), tsl(u), :]).astype(BF16)


def _gla(gqk, gv, glog, gg, norm_w):
    b, s, _ = gqk.shape
    c_len = GLA_CHUNK
    step = min(GLA_STEP, s)
    blk = lambda width: pl.BlockSpec((b, step, width), lambda c: (0, c, 0))
    nw2 = norm_w.reshape(1, GLA_DV)
    return pl.pallas_call(
        _gla_kernel,
        grid=(s // step,),
        in_specs=[blk(2 * GLA_QKW), blk(GLA_W), blk(GLA_QKW), blk(GLA_W),
                  pl.BlockSpec(nw2.shape, lambda c: (0, 0))],
        out_specs=blk(GLA_W),
        out_shape=jax.ShapeDtypeStruct((b, s, GLA_W), BF16),
        scratch_shapes=[pltpu.VMEM((b, GLA_DV, GLA_QKW), F32)],
        compiler_params=pltpu.CompilerParams(dimension_semantics=("arbitrary",),
                                             vmem_limit_bytes=VMEM_LIMIT),
        name="gla_chunked",
    )(gqk, gv, glog, gg, nw2)


def _out_kernel(att_ref, gla_ref, wa_ref, wg_ref, x_ref, nw_ref, out_ref):
    mixed = jnp.dot(att_ref[...], wa_ref[...], preferred_element_type=F32) + \
        jnp.dot(gla_ref[...], wg_ref[...], preferred_element_type=F32)
    y = mixed * lax.rsqrt(jnp.mean(mixed * mixed, axis=-1, keepdims=True) + NORM_EPS)
    out_ref[...] = x_ref[...] + y * nw_ref[...]


def _out_project(att2, gla2, w_out, x2, post_norm_w):
    n, d = x2.shape
    tm = min(OUT_TM, n)
    wa, wg = w_out[:ATT_W].astype(BF16), w_out[ATT_W:].astype(BF16)
    row = lambda width: pl.BlockSpec((tm, width), lambda i: (i, 0))
    full = lambda a: pl.BlockSpec(a.shape, lambda i: (0, 0))
    return pl.pallas_call(
        _out_kernel,
        grid=(n // tm,),
        in_specs=[row(ATT_W), row(GLA_W), full(wa), full(wg), row(d), full(post_norm_w)],
        out_specs=row(d),
        out_shape=jax.ShapeDtypeStruct((n, d), F32),
        compiler_params=pltpu.CompilerParams(dimension_semantics=("arbitrary",),
                                             vmem_limit_bytes=VMEM_LIMIT),
        name="out_proj_postnorm",
    )(att2, gla2, wa, wg, x2, post_norm_w)


def kernel(x, positions, w_in, w_gla_gate_up, b_gla_gate, gla_norm_w, w_out, pre_norm_w, post_norm_w):
    b, s, d = x.shape
    n = b * s
    assert w_in.shape[0] == d and w_out.shape == (ATT_W + GLA_W, d)
    assert s % PROJ_TM == 0 and s % DSA_TQ == 0 and DSA_TQ == 2 * DSA_TK
    assert s % min(GLA_STEP, s) == 0 and min(GLA_STEP, s) % GLA_CHUNK == 0 and n % min(OUT_TM, n) == 0
    cos8, sin8 = _rope_tables(positions)
    aqt, ak, avt, iqt, ik, iwt, ag, gqk, gv, glog, gg = _project(x, pre_norm_w.reshape(1, d), cos8, sin8, w_in,
                                                                  w_gla_gate_up, b_gla_gate)
    att = _dsa_attention(aqt, ak, avt, iqt, ik, iwt, ag)
    gla = _gla(gqk, gv, glog, gg, gla_norm_w)
    out = _out_project(att.reshape(n, ATT_W), gla.reshape(n, GLA_W), w_out, x.reshape(n, d),
                       post_norm_w.reshape(1, d))
    return out.reshape(b, s, d)
```
